```python
import jax, jax.numpy as jnp
from jax import lax
import numpy as np

D_MODEL = 1024
BATCH = 4
SEQ = 8192
DEPTH = 1

CHUNK = 64
A_HEADS = D_MODEL // 128
A_HEAD_DIM = 64
A_LEFT_CHUNKS = 8
A_BAND = A_LEFT_CHUNKS + 1
A_MAX_REL = 128
B_HEADS = D_MODEL // 128
B_Q_LORA = D_MODEL // 4
B_KV_LORA = D_MODEL // 8
B_NOPE = 64
B_ROPE = 32
B_V_DIM = 64
B_QK_DIM = B_NOPE + B_ROPE
ROPE_THETA = 10000.0
Q_BLOCK = 128
A_WIDTH = A_HEADS * A_HEAD_DIM
B_WIDTH = B_HEADS * B_V_DIM
D_MIX = A_WIDTH + B_WIDTH
IN_COLS = 3 * A_WIDTH + B_Q_LORA + B_KV_LORA + B_ROPE
D_FF = ((8 * D_MODEL // 3 + 127) // 128) * 128
EPS = 1e-6
NEG_INF = -1e30

kernel_name = "hymba_chunked_relbias_mla_macaron"


def rmsnorm(x, g):
    xf = x.astype(jnp.float32)
    y = xf * lax.rsqrt(jnp.mean(xf * xf, axis=-1, keepdims=True) + EPS)
    return (y * g.astype(jnp.float32)).astype(x.dtype)


def swiglu(h, w_gate, w_up, w_down):
    return (jax.nn.silu(h @ w_gate) * (h @ w_up)) @ w_down


def rope_tables(seq, dim):
    inv = 1.0 / (ROPE_THETA ** (jnp.arange(0, dim, 2, dtype=jnp.float32) / dim))
    ang = jnp.arange(seq, dtype=jnp.float32)[:, None] * inv[None, :]
    return jnp.cos(ang), jnp.sin(ang)


def apply_rope(x, cos, sin):
    x1, x2 = jnp.split(x, 2, axis=-1)
    c = cos[None, :, None, :].astype(x.dtype)
    s = sin[None, :, None, :].astype(x.dtype)
    return jnp.concatenate([x1 * c - x2 * s, x1 * s + x2 * c], axis=-1)


def chunked_relbias_attention(q, k, v, rel_bias):
    b, s, h, dh = q.shape
    nc = s // CHUNK
    band_len = A_BAND * CHUNK
    pad = ((0, 0), (A_LEFT_CHUNKS * CHUNK, 0), (0, 0), (0, 0))
    kp = jnp.pad(k, pad)
    vp = jnp.pad(v, pad)
    qc = q.reshape(b, nc, CHUNK, h, dh).transpose(1, 0, 2, 3, 4)
    qi = jnp.arange(CHUNK)
    kj = jnp.arange(band_len) - A_LEFT_CHUNKS * CHUNK
    rel = jnp.clip(qi[:, None] - kj[None, :], -A_MAX_REL, A_MAX_REL) + A_MAX_REL
    bias = rel_bias.astype(jnp.float32)[:, rel]
    scale = dh ** -0.5

    def one_chunk(args):
        q_blk, c = args
        kb = lax.dynamic_slice_in_dim(kp, c * CHUNK, band_len, axis=1)
        vb = lax.dynamic_slice_in_dim(vp, c * CHUNK, band_len, axis=1)
        sc = jnp.einsum('bihd,bjhd->bhij', q_blk, kb).astype(jnp.float32) * scale
        sc = sc + bias[None]
        valid = jnp.arange(band_len) >= (A_LEFT_CHUNKS - c) * CHUNK
        sc = jnp.where(valid[None, None, None, :], sc, NEG_INF)
        p = jax.nn.softmax(sc, axis=-1).astype(vb.dtype)
        return jnp.einsum('bhij,bjhd->bihd', p, vb)

    o = lax.map(one_chunk, (qc, jnp.arange(nc, dtype=jnp.int32)))
    return o.transpose(1, 0, 2, 3, 4).reshape(b, s, h * dh)


def block_causal_attention(q, k, v):
    b, s, h, dq = q.shape
    dv = v.shape[-1]
    nb = s // Q_BLOCK
    qb = q.reshape(b, nb, Q_BLOCK, h, dq).transpose(1, 0, 2, 3, 4)
    k_chunk = jnp.arange(s) // CHUNK
    scale = dq ** -0.5

    def one_block(args):
        q_blk, i = args
        q_chunk = (i * Q_BLOCK + jnp.arange(Q_BLOCK)) // CHUNK
        sc = jnp.einsum('bqhd,bkhd->bhqk', q_blk, k).astype(jnp.float32) * scale
        mask = k_chunk[None, :] <= q_chunk[:, None]
        sc = jnp.where(mask[None, None], sc, NEG_INF)
        p = jax.nn.softmax(sc, axis=-1).astype(v.dtype)
        return jnp.einsum('bhqk,bkhd->bqhd', p, v)

    o = lax.map(one_block, (qb, jnp.arange(nb, dtype=jnp.int32)))
    return o.transpose(1, 0, 2, 3, 4).reshape(b, s, h * dv)


def mla_attention(c_q, c_kv, k_rope, q_lat_norm, w_uq, kv_lat_norm, w_ukv,
                  q_nope_norm, q_rope_norm, k_nope_norm, k_rope_norm, cos, sin):
    b, s, _ = c_q.shape
    q = (rmsnorm(c_q, q_lat_norm) @ w_uq).reshape(b, s, B_HEADS, B_QK_DIM)
    kv = (rmsnorm(c_kv, kv_lat_norm) @ w_ukv).reshape(b, s, B_HEADS, B_NOPE + B_V_DIM)
    q_nope = rmsnorm(q[..., :B_NOPE], q_nope_norm)
    q_pe = apply_rope(rmsnorm(q[..., B_NOPE:], q_rope_norm), cos, sin)
    k_nope = rmsnorm(kv[..., :B_NOPE], k_nope_norm)
    v = kv[..., B_NOPE:]
    k_pe = apply_rope(rmsnorm(k_rope, k_rope_norm)[:, :, None, :], cos, sin)
    q_full = jnp.concatenate([q_nope, q_pe], axis=-1)
    k_full = jnp.concatenate([k_nope, jnp.broadcast_to(k_pe, (b, s, B_HEADS, B_ROPE))], axis=-1)
    return block_causal_attention(q_full, k_full, v)


def setup_inputs(seed: int = 0) -> dict:
    key = jax.random.key(seed)
    ks = iter(jax.random.split(key, 32))
    f32 = jnp.float32

    def w(shape, fan_in):
        return jax.random.normal(next(ks), (DEPTH,) + shape, f32) * fan_in ** -0.5

    def gain(n):
        return 1.0 + 0.05 * jax.random.normal(next(ks), (DEPTH, n), f32)

    x = jax.random.normal(next(ks), (BATCH, SEQ, D_MODEL), f32)
    return {
        "x": x,
        "ffn1_norm": gain(D_MODEL),
        "ffn1_w_gate": w((D_MODEL, D_FF), D_MODEL),
        "ffn1_w_up": w((D_MODEL, D_FF), D_MODEL),
        "ffn1_w_down": w((D_FF, D_MODEL), D_FF),
        "mix_norm": gain(D_MODEL),
        "w_in": w((D_MODEL, IN_COLS), D_MODEL),
        "a_q_norm": gain(A_HEAD_DIM),
        "a_k_norm": gain(A_HEAD_DIM),
        "a_rel_bias": 0.5 * jax.random.normal(next(ks), (DEPTH, A_HEADS, 2 * A_MAX_REL + 1), f32),
        "b_q_lat_norm": gain(B_Q_LORA),
        "b_w_uq": w((B_Q_LORA, B_HEADS * B_QK_DIM), B_Q_LORA),
        "b_kv_lat_norm": gain(B_KV_LORA),
        "b_w_ukv": w((B_KV_LORA, B_HEADS * (B_NOPE + B_V_DIM)), B_KV_LORA),
        "b_q_nope_norm": gain(B_NOPE),
        "b_q_rope_norm": gain(B_ROPE),
        "b_k_nope_norm": gain(B_NOPE),
        "b_k_rope_norm": gain(B_ROPE),
        "w_out": w((D_MIX, D_MODEL), D_MIX),
        "ffn2_norm": gain(D_MODEL),
        "ffn2_w_gate": w((D_MODEL, D_FF), D_MODEL),
        "ffn2_w_up": w((D_MODEL, D_FF), D_MODEL),
        "ffn2_w_down": w((D_FF, D_MODEL), D_FF),
        "final_norm": gain(D_MODEL),
    }


def reference(x, ffn1_norm, ffn1_w_gate, ffn1_w_up, ffn1_w_down, mix_norm, w_in,
              a_q_norm, a_k_norm, a_rel_bias, b_q_lat_norm, b_w_uq, b_kv_lat_norm,
              b_w_ukv, b_q_nope_norm, b_q_rope_norm, b_k_nope_norm, b_k_rope_norm,
              w_out, ffn2_norm, ffn2_w_gate, ffn2_w_up, ffn2_w_down, final_norm):
    b, s, _ = x.shape
    cos, sin = rope_tables(s, B_ROPE)
    o_qa, o_ka, o_va = 0, A_WIDTH, 2 * A_WIDTH
    o_cq = 3 * A_WIDTH
    o_ckv = o_cq + B_Q_LORA
    o_kr = o_ckv + B_KV_LORA
    for l in range(DEPTH):
        x = x + 0.5 * swiglu(rmsnorm(x, ffn1_norm[l]), ffn1_w_gate[l], ffn1_w_up[l], ffn1_w_down[l])
        proj = rmsnorm(x, mix_norm[l]) @ w_in[l]
        qa = rmsnorm(proj[..., o_qa:o_ka].reshape(b, s, A_HEADS, A_HEAD_DIM), a_q_norm[l])
        ka = rmsnorm(proj[..., o_ka:o_va].reshape(b, s, A_HEADS, A_HEAD_DIM), a_k_norm[l])
        va = proj[..., o_va:o_cq].reshape(b, s, A_HEADS, A_HEAD_DIM)
        out_a = chunked_relbias_attention(qa, ka, va, a_rel_bias[l])
        out_b = mla_attention(proj[..., o_cq:o_ckv], proj[..., o_ckv:o_kr], proj[..., o_kr:],
                              b_q_lat_norm[l], b_w_uq[l], b_kv_lat_norm[l], b_w_ukv[l],
                              b_q_nope_norm[l], b_q_rope_norm[l], b_k_nope_norm[l],
                              b_k_rope_norm[l], cos, sin)
        x = x + jnp.concatenate([out_a, out_b], axis=-1) @ w_out[l]
        x = x + 0.5 * swiglu(rmsnorm(x, ffn2_norm[l]), ffn2_w_gate[l], ffn2_w_up[l], ffn2_w_down[l])
        x = rmsnorm(x, final_norm[l])
    return x
```

```python
import functools

import jax
import jax.numpy as jnp
from jax import lax
from jax.experimental import pallas as pl
from jax.experimental.pallas import tpu as pltpu

D_MODEL = 1024
D_FF = 2816
CHUNK = 64
A_HEADS = 8
A_HEAD_DIM = 64
A_LEFT_CHUNKS = 8
A_MAX_REL = 128
A_WIDTH = A_HEADS * A_HEAD_DIM
B_HEADS = 8
B_Q_LORA = 256
B_KV_LORA = 128
B_NOPE = 64
B_ROPE = 32
B_V_DIM = 64
B_QK_DIM = B_NOPE + B_ROPE
B_WIDTH = B_HEADS * B_V_DIM
ROPE_THETA = 10000.0
EPS = 1e-6
NEG_INF = -1e30

LANES = 128
MXU_DIM = 256
TOK_TILE = 512
FF_CHUNK = 256
A_Q_TILE = 128
A_WIN = A_Q_TILE + A_LEFT_CHUNKS * CHUNK
A_BIAS_TILES = (A_WIN + A_LEFT_CHUNKS * CHUNK) // LANES
HEAD_PAD = 128
KR_COL = 3 * A_WIDTH + B_Q_LORA + B_KV_LORA
IN_COLS_PAD = KR_COL + LANES
VMEM_LIMIT = 60 * 1024 * 1024

_BF = jnp.bfloat16
_F32 = jnp.float32


def _dot(a, b):
    return jnp.dot(a, b, preferred_element_type=_F32)


def _rms(x, g):
    ms = jnp.mean(x * x, axis=-1, keepdims=True)
    return x * lax.rsqrt(ms + EPS) * g


def _swiglu(h, wg_ref, wu_ref, wd_ref, act_ref):
    for c in range(D_FF // FF_CHUNK):
        sl = slice(c * FF_CHUNK, (c + 1) * FF_CHUNK)
        g = _dot(h, wg_ref[:, sl])
        u = _dot(h, wu_ref[:, sl])
        act_ref[:, sl] = (jax.nn.silu(g) * u).astype(_BF)
    return _dot(act_ref[...], wd_ref[...])


def _seg_sumsq(x, seg_ref):
    sq = x * x
    hi = sq.astype(_BF)
    lo = (sq - hi.astype(_F32)).astype(_BF)
    seg = seg_ref[...]
    parts = []
    for j in range(x.shape[1] // MXU_DIM):
        sl = slice(j * MXU_DIM, (j + 1) * MXU_DIM)
        parts.append(_dot(hi[:, sl], seg) + _dot(lo[:, sl], seg))
    return jnp.concatenate(parts, axis=1)


def _pre_kernel(x_ref, g1_ref, wg_ref, wu_ref, wd_ref, gmix_ref, win_ref,
                gaq_ref, gak_ref, seg64_ref,
                gcq_ref, wqt_ref, gqtab_ref, cost_ref, sint_ref,
                gckv_ref, wkn_ref, wvt_ref, gkn_ref, seg128_ref,
                gkr_ref, ck_ref, sk_ref,
                x1_ref, qa_ref, ka_ref, va_ref, qbt_ref, kb_ref, vbt_ref,
                act_ref):
    x = x_ref[...]
    h = _rms(x, g1_ref[...]).astype(_BF)
    x1 = x + 0.5 * _swiglu(h, wg_ref, wu_ref, wd_ref, act_ref)
    x1_ref[...] = x1
    h2 = _rms(x1, gmix_ref[...]).astype(_BF)

    qa = _dot(h2, win_ref[:, 0:A_WIDTH])
    qa_ref[0] = (qa * lax.rsqrt(_seg_sumsq(qa, seg64_ref) * (1.0 / A_HEAD_DIM) + EPS)
                 * gaq_ref[...]).astype(_BF)
    ka = _dot(h2, win_ref[:, A_WIDTH:2 * A_WIDTH])
    ka_ref[0] = (ka * lax.rsqrt(_seg_sumsq(ka, seg64_ref) * (1.0 / A_HEAD_DIM) + EPS)
                 * gak_ref[...]).astype(_BF)
    va_ref[0] = _dot(h2, win_ref[:, 2 * A_WIDTH:3 * A_WIDTH]).astype(_BF)

    o_cq = 3 * A_WIDTH
    cq = _dot(h2, win_ref[:, o_cq:o_cq + B_Q_LORA])
    cqn_t = _rms(cq, gcq_ref[...]).T.astype(_BF)
    q_t = _dot(wqt_ref[...], cqn_t)
    cos_t = cost_ref[...]
    sin_t = sint_ref[...]
    gq = gqtab_ref[...]
    half = B_ROPE // 2
    for hd in range(B_HEADS):
        r0 = hd * HEAD_PAD
        nope = q_t[r0:r0 + B_NOPE]
        nope = nope * lax.rsqrt(jnp.mean(nope * nope, axis=0, keepdims=True) + EPS) * gq[0:B_NOPE]
        rope = q_t[r0 + B_NOPE:r0 + B_QK_DIM]
        rope = (rope * lax.rsqrt(jnp.mean(rope * rope, axis=0, keepdims=True) + EPS)
                * gq[B_NOPE:B_QK_DIM])
        r1 = rope[0:half]
        r2 = rope[half:B_ROPE]
        blk = jnp.concatenate(
            [nope, r1 * cos_t - r2 * sin_t, r1 * sin_t + r2 * cos_t,
             jnp.zeros((HEAD_PAD - B_QK_DIM, nope.shape[1]), _F32)], axis=0)
        qbt_ref[0, hd] = blk.astype(_BF)

    o_ckv = o_cq + B_Q_LORA
    ckv = _dot(h2, win_ref[:, o_ckv:o_ckv + B_KV_LORA])
    ckvn = _rms(ckv, gckv_ref[...])
    v_t = _dot(wvt_ref[...], ckvn.T.astype(_BF))
    for hd in range(B_HEADS):
        vbt_ref[0, hd, 0] = v_t[hd * B_V_DIM:(hd + 1) * B_V_DIM].astype(_BF)
    kn = _dot(ckvn.astype(_BF), wkn_ref[...])
    kn = kn * lax.rsqrt(_seg_sumsq(kn, seg128_ref) * (1.0 / B_NOPE) + EPS) * gkn_ref[...]
    kr = _dot(h2, win_ref[:, KR_COL:KR_COL + LANES])
    y = kr * lax.rsqrt(jnp.sum(kr * kr, axis=-1, keepdims=True) * (1.0 / B_ROPE) + EPS) * gkr_ref[...]
    lane = lax.broadcasted_iota(jnp.int32, y.shape, 1)
    swapped = jnp.where(lane < B_NOPE + half,
                        pltpu.roll(y, LANES - half, axis=1), pltpu.roll(y, half, axis=1))
    kpe = y * ck_ref[...] + swapped * sk_ref[...]
    for hd in range(B_HEADS):
        kb_ref[0, hd] = (kn[:, hd * HEAD_PAD:(hd + 1) * HEAD_PAD] + kpe).astype(_BF)


def _attn_a_kernel(q_ref, k_ref, v_ref, bias_ref, o_ref):
    i = pl.program_id(1)
    left_tiles = A_LEFT_CHUNKS * CHUNK // A_Q_TILE
    start = pl.multiple_of(jnp.maximum(i - left_tiles, 0) * A_Q_TILE, A_Q_TILE)
    t0 = left_tiles - jnp.minimum(i, left_tiles)
    kwin = k_ref[0, pl.ds(start, A_WIN), :]
    vwin = v_ref[0, pl.ds(start, A_WIN), :]
    q = q_ref[0]
    lane = lax.broadcasted_iota(jnp.int32, (A_Q_TILE, LANES), 1)
    low = lane < A_HEAD_DIM
    for p in range(A_HEADS // 2):
        sl = slice(p * LANES, (p + 1) * LANES)
        qp, kp, vp = q[:, sl], kwin[:, sl], vwin[:, sl]
        o_pair = None
        for e in range(2):
            hd = 2 * p + e
            sel = low if e == 0 else jnp.logical_not(low)
            qm = jnp.where(sel, qp, jnp.zeros_like(qp))
            s = lax.dot_general(qm, kp, (((1,), (1,)), ((), ())), preferred_element_type=_F32)
            bias = jnp.concatenate(
                [bias_ref[hd, t0 + j] for j in range(A_WIN // LANES)], axis=1)
            s = s + bias
            m = jnp.max(s, axis=1, keepdims=True)
            pr = jnp.exp(s - m)
            l = jnp.sum(pr, axis=1, keepdims=True)
            o = _dot(pr.astype(_BF), vp) / l
            o_pair = o if e == 0 else jnp.where(low, o_pair, o)
        o_ref[0, :, sl] = o_pair.astype(_BF)


def _attn_b_kernel(qt_ref, k_ref, vt_ref, o_ref, acc_ref):
    i = pl.program_id(1)
    t = TOK_TILE
    krow = lax.broadcasted_iota(jnp.int32, (t, t), 0) // CHUNK
    qcol = lax.broadcasted_iota(jnp.int32, (t, t), 1) // CHUNK
    diag_ok = krow <= qcol

    def head_body(hd, _):
        q_t = qt_ref[0, hd]

        def step(kt, carry, masked):
            m, l, acc = carry
            k = k_ref[0, hd, pl.ds(pl.multiple_of(kt * t, t), t), :]
            s = _dot(k, q_t)
            if masked:
                s = jnp.where(diag_ok, s, NEG_INF)
            m_new = jnp.maximum(m, jnp.max(s, axis=0, keepdims=True))
            alpha = jnp.exp(m - m_new)
            p = jnp.exp(s - m_new)
            l = alpha * l + jnp.sum(p, axis=0, keepdims=True)
            acc = alpha * acc + _dot(vt_ref[0, hd, kt], p.astype(_BF))
            return m_new, l, acc

        init = (jnp.full((1, t), NEG_INF, _F32), jnp.zeros((1, t), _F32),
                jnp.zeros((B_V_DIM, t), _F32))
        carry = lax.fori_loop(0, i, functools.partial(step, masked=False), init)
        _, l, acc = step(i, carry, masked=True)
        acc_ref[pl.ds(pl.multiple_of(hd * B_V_DIM, B_V_DIM), B_V_DIM), :] = acc / l
        return 0

    lax.fori_loop(0, B_HEADS, head_body, 0)
    o_ref[0] = acc_ref[...].T.astype(_BF)


def _post_kernel(x1_ref, oa_ref, ob_ref, wout_ref, g2_ref, wg_ref, wu_ref, wd_ref,
                 gf_ref, y_ref, act_ref):
    x2 = (x1_ref[...] + _dot(oa_ref[...], wout_ref[0:A_WIDTH, :])
          + _dot(ob_ref[...], wout_ref[A_WIDTH:A_WIDTH + B_WIDTH, :]))
    h = _rms(x2, g2_ref[...]).astype(_BF)
    x3 = x2 + 0.5 * _swiglu(h, wg_ref, wu_ref, wd_ref, act_ref)
    y_ref[...] = _rms(x3, gf_ref[...])


def _const_spec(shape):
    n = len(shape)
    return pl.BlockSpec(shape, lambda *_: (0,) * n, pipeline_mode=pl.Buffered(1))


def _rel_bias_tiles(rel_bias):
    r = jnp.arange(A_Q_TILE)[:, None]
    off = jnp.arange(A_BIAS_TILES * LANES)[None, :] - A_LEFT_CHUNKS * CHUNK
    idx = jnp.clip(r - off, -A_MAX_REL, A_MAX_REL) + A_MAX_REL
    kc = jnp.floor_divide(off, CHUNK)
    qc = r // CHUNK
    valid = (kc <= qc) & (kc >= qc - A_LEFT_CHUNKS)
    full = jnp.where(valid[None], rel_bias.astype(_F32)[:, idx], NEG_INF)
    return full.reshape(A_HEADS, A_Q_TILE, A_BIAS_TILES, LANES).transpose(0, 2, 1, 3)


def kernel(x, ffn1_norm, ffn1_w_gate, ffn1_w_up, ffn1_w_down, mix_norm, w_in, a_q_norm, a_k_norm, a_rel_bias, b_q_lat_norm, b_w_uq, b_kv_lat_norm, b_w_ukv, b_q_nope_norm, b_q_rope_norm, b_k_nope_norm, b_k_rope_norm, w_out, ffn2_norm, ffn2_w_gate, ffn2_w_up, ffn2_w_down, final_norm):
    b, s, d = x.shape
    assert d == D_MODEL and s % TOK_TILE == 0 and ffn1_norm.shape[0] == 1
    tm = TOK_TILE
    nt = s // tm
    half = B_ROPE // 2

    def row(v):
        return v.astype(_F32)[None, :]

    w_in_l = w_in[0]
    kr_cols = jnp.zeros((d, LANES), _F32).at[:, B_NOPE:B_QK_DIM].set(w_in_l[:, KR_COL:KR_COL + B_ROPE])
    w_in_p = jnp.concatenate([w_in_l[:, :KR_COL], kr_cols], axis=1).astype(_BF)

    gaq = row(jnp.tile(a_q_norm[0], A_HEADS) * (A_HEAD_DIM ** -0.5))
    gak = row(jnp.tile(a_k_norm[0], A_HEADS))
    blk = jnp.arange(MXU_DIM)
    seg64 = (blk[:, None] // A_HEAD_DIM == blk[None, :] // A_HEAD_DIM).astype(_BF)
    seg128 = (blk[:, None] // HEAD_PAD == blk[None, :] // HEAD_PAD).astype(_BF)

    w_uq = b_w_uq[0].reshape(B_Q_LORA, B_HEADS, B_QK_DIM)
    w_uq = jnp.pad(w_uq, ((0, 0), (0, 0), (0, HEAD_PAD - B_QK_DIM)))
    wq_t = w_uq.reshape(B_Q_LORA, B_HEADS * HEAD_PAD).T.astype(_BF)
    scale_b = B_QK_DIM ** -0.5
    gq_col = jnp.concatenate([b_q_nope_norm[0] * scale_b, b_q_rope_norm[0] * scale_b,
                              jnp.zeros((HEAD_PAD - B_QK_DIM,), _F32)])
    gq_tab = jnp.broadcast_to(gq_col[:, None], (HEAD_PAD, tm)).astype(_F32)

    inv = 1.0 / (ROPE_THETA ** (jnp.arange(0, B_ROPE, 2, dtype=_F32) / B_ROPE))
    ang = jnp.arange(s, dtype=_F32)[:, None] * inv[None, :]
    cos, sin = jnp.cos(ang), jnp.sin(ang)
    cos_t, sin_t = cos.T, sin.T
    zpad = jnp.zeros((s, B_NOPE), _F32)
    zend = jnp.zeros((s, LANES - B_QK_DIM), _F32)
    ck = jnp.concatenate([zpad, cos, cos, zend], axis=1)
    sk = jnp.concatenate([zpad, -sin, sin, zend], axis=1)

    w_ukv = b_w_ukv[0].reshape(B_KV_LORA, B_HEADS, B_NOPE + B_V_DIM)
    wkn = jnp.pad(w_ukv[..., :B_NOPE], ((0, 0), (0, 0), (0, HEAD_PAD - B_NOPE)))
    wkn = wkn.reshape(B_KV_LORA, B_HEADS * HEAD_PAD).astype(_BF)
    wv_t = w_ukv[..., B_NOPE:].reshape(B_KV_LORA, B_WIDTH).T.astype(_BF)
    gkn = row(jnp.tile(jnp.concatenate([b_k_nope_norm[0], jnp.zeros((HEAD_PAD - B_NOPE,), _F32)]),
                       B_HEADS))
    gkr = row(jnp.concatenate([jnp.zeros((B_NOPE,), _F32), b_k_rope_norm[0],
                               jnp.zeros((LANES - B_QK_DIM,), _F32)]))

    bias_tiles = _rel_bias_tiles(a_rel_bias[0])

    tok = lambda bi, j: (bi, j, 0)
    x3d = x
    pre_in_specs = [
        pl.BlockSpec((None, tm, d), tok),
        _const_spec((1, d)), _const_spec((d, D_FF)), _const_spec((d, D_FF)), _const_spec((D_FF, d)),
        _const_spec((1, d)), _const_spec((d, IN_COLS_PAD)),
        _const_spec((1, A_WIDTH)), _const_spec((1, A_WIDTH)), _const_spec((MXU_DIM, MXU_DIM)),
        _const_spec((1, B_Q_LORA)), _const_spec((B_HEADS * HEAD_PAD, B_Q_LORA)),
        _const_spec((HEAD_PAD, tm)),
        pl.BlockSpec((half, tm), lambda bi, j: (0, j)), pl.BlockSpec((half, tm), lambda bi, j: (0, j)),
        _const_spec((1, B_KV_LORA)), _const_spec((B_KV_LORA, B_HEADS * HEAD_PAD)),
        _const_spec((B_WIDTH, B_KV_LORA)), _const_spec((1, B_HEADS * HEAD_PAD)),
        _const_spec((MXU_DIM, MXU_DIM)),
        _const_spec((1, LANES)),
        pl.BlockSpec((tm, LANES), lambda bi, j: (j, 0)), pl.BlockSpec((tm, LANES), lambda bi, j: (j, 0)),
    ]
    pre_out_shapes = [
        jax.ShapeDtypeStruct((b, s, d), _F32),
        jax.ShapeDtypeStruct((b, s, A_WIDTH), _BF),
        jax.ShapeDtypeStruct((b, s, A_WIDTH), _BF),
        jax.ShapeDtypeStruct((b, s, A_WIDTH), _BF),
        jax.ShapeDtypeStruct((b, B_HEADS, HEAD_PAD, s), _BF),
        jax.ShapeDtypeStruct((b, B_HEADS, s, HEAD_PAD), _BF),
        jax.ShapeDtypeStruct((b, B_HEADS, nt, B_V_DIM, tm), _BF),
    ]
    pre_out_specs = [
        pl.BlockSpec((None, tm, d), tok),
        pl.BlockSpec((1, tm, A_WIDTH), tok),
        pl.BlockSpec((1, tm, A_WIDTH), tok),
        pl.BlockSpec((1, tm, A_WIDTH), tok),
        pl.BlockSpec((1, B_HEADS, HEAD_PAD, tm), lambda bi, j: (bi, 0, 0, j)),
        pl.BlockSpec((1, B_HEADS, tm, HEAD_PAD), lambda bi, j: (bi, 0, j, 0)),
        pl.BlockSpec((1, B_HEADS, 1, B_V_DIM, tm), lambda bi, j: (bi, 0, j, 0, 0)),
    ]
    x1, qa, ka, va, qbt, kb, vbt = pl.pallas_call(
        _pre_kernel,
        grid=(b, nt),
        in_specs=pre_in_specs,
        out_specs=pre_out_specs,
        out_shape=pre_out_shapes,
        scratch_shapes=[pltpu.VMEM((tm, D_FF), _BF)],
        compiler_params=pltpu.CompilerParams(
            dimension_semantics=("arbitrary", "arbitrary"), vmem_limit_bytes=VMEM_LIMIT),
        name="pre",
    )(x3d, row(ffn1_norm[0]), ffn1_w_gate[0].astype(_BF), ffn1_w_up[0].astype(_BF),
      ffn1_w_down[0].astype(_BF), row(mix_norm[0]), w_in_p,
      gaq, gak, seg64,
      row(b_q_lat_norm[0]), wq_t, gq_tab, cos_t, sin_t,
      row(b_kv_lat_norm[0]), wkn, wv_t, gkn, seg128,
      gkr, ck, sk)

    out_a = pl.pallas_call(
        _attn_a_kernel,
        grid=(b, s // A_Q_TILE),
        in_specs=[
            pl.BlockSpec((1, A_Q_TILE, A_WIDTH), tok),
            pl.BlockSpec((1, s, A_WIDTH), lambda bi, j: (bi, 0, 0)),
            pl.BlockSpec((1, s, A_WIDTH), lambda bi, j: (bi, 0, 0)),
            _const_spec((A_HEADS, A_BIAS_TILES, A_Q_TILE, LANES)),
        ],
        out_specs=pl.BlockSpec((1, A_Q_TILE, A_WIDTH), tok),
        out_shape=jax.ShapeDtypeStruct((b, s, A_WIDTH), _BF),
        compiler_params=pltpu.CompilerParams(
            dimension_semantics=("arbitrary", "arbitrary"), vmem_limit_bytes=VMEM_LIMIT),
        name="attn_a",
    )(qa, ka, va, bias_tiles)

    out_b = pl.pallas_call(
        _attn_b_kernel,
        grid=(b, nt),
        in_specs=[
            pl.BlockSpec((1, B_HEADS, HEAD_PAD, tm), lambda bi, j: (bi, 0, 0, j)),
            pl.BlockSpec((1, B_HEADS, s, HEAD_PAD), lambda bi, j: (bi, 0, 0, 0),
                         pipeline_mode=pl.Buffered(1)),
            pl.BlockSpec((1, B_HEADS, nt, B_V_DIM, tm), lambda bi, j: (bi, 0, 0, 0, 0),
                         pipeline_mode=pl.Buffered(1)),
        ],
        out_specs=pl.BlockSpec((1, tm, B_WIDTH), tok),
        out_shape=jax.ShapeDtypeStruct((b, s, B_WIDTH), _BF),
        scratch_shapes=[pltpu.VMEM((B_WIDTH, tm), _F32)],
        compiler_params=pltpu.CompilerParams(
            dimension_semantics=("arbitrary", "arbitrary"), vmem_limit_bytes=VMEM_LIMIT),
        name="attn_b",
    )(qbt, kb, vbt)

    n = b * s
    flat = lambda j: (j, 0)
    y = pl.pallas_call(
        _post_kernel,
        grid=(n // tm,),
        in_specs=[
            pl.BlockSpec((tm, d), flat),
            pl.BlockSpec((tm, A_WIDTH), flat),
            pl.BlockSpec((tm, B_WIDTH), flat),
            _const_spec((A_WIDTH + B_WIDTH, d)),
            _const_spec((1, d)), _const_spec((d, D_FF)), _const_spec((d, D_FF)), _const_spec((D_FF, d)),
            _const_spec((1, d)),
        ],
        out_specs=pl.BlockSpec((tm, d), flat),
        out_shape=jax.ShapeDtypeStruct((n, d), _F32),
        scratch_shapes=[pltpu.VMEM((tm, D_FF), _BF)],
        compiler_params=pltpu.CompilerParams(
            dimension_semantics=("arbitrary",), vmem_limit_bytes=VMEM_LIMIT),
        name="post",
    )(x1.reshape(n, d), out_a.reshape(n, A_WIDTH), out_b.reshape(n, B_WIDTH),
      w_out[0].astype(_BF), row(ffn2_norm[0]), ffn2_w_gate[0].astype(_BF),
      ffn2_w_up[0].astype(_BF), ffn2_w_down[0].astype(_BF), row(final_norm[0]))
    return y.reshape(b, s, d)
```

```python
import jax
import jax.numpy as jnp
from jax import lax
from jax.experimental import pallas as pl
from jax.experimental.pallas import tpu as pltpu

D_MODEL = 1024
D_FF = 2816
CHUNK = 64
A_HEADS = 8
A_HEAD_DIM = 64
A_LEFT_CHUNKS = 8
A_MAX_REL = 128
A_WIDTH = A_HEADS * A_HEAD_DIM
B_HEADS = 8
B_Q_LORA = 256
B_KV_LORA = 128
B_NOPE = 64
B_ROPE = 32
B_V_DIM = 64
B_QK_DIM = B_NOPE + B_ROPE
B_WIDTH = B_HEADS * B_V_DIM
ROPE_THETA = 10000.0
EPS = 1e-6
NEG_INF = -1e30

LANES = 128
MXU_DIM = 256
TOK_TILE = 512
FF_CHUNK = 256
A_Q_TILE = 128
A_WIN = A_Q_TILE + A_LEFT_CHUNKS * CHUNK
A_BIAS_TILES = (A_WIN + A_LEFT_CHUNKS * CHUNK) // LANES
A_BIAS_CENTER = A_WIN
A_BIAS_TAB = A_BIAS_CENTER + A_BIAS_TILES * LANES - A_LEFT_CHUNKS * CHUNK
HEAD_PAD = 128
B_ACC_ROWS = B_V_DIM + 16
LOG2E = 1.4426950408889634
KR_COL = 3 * A_WIDTH + B_Q_LORA + B_KV_LORA
IN_COLS_PAD = KR_COL + LANES
VMEM_LIMIT = 60 * 1024 * 1024

_BF = jnp.bfloat16
_F32 = jnp.float32


def _dot(a, b):
    return jnp.dot(a, b, preferred_element_type=_F32)


def _rms(x, g):
    ms = jnp.mean(x * x, axis=-1, keepdims=True)
    return x * lax.rsqrt(ms + EPS) * g


def _swiglu(h, wg_ref, wu_ref, wd_ref, act_ref):
    for c in range(D_FF // FF_CHUNK):
        sl = slice(c * FF_CHUNK, (c + 1) * FF_CHUNK)
        g = _dot(h, wg_ref[:, sl])
        u = _dot(h, wu_ref[:, sl])
        act_ref[:, sl] = (jax.nn.silu(g) * u).astype(_BF)
    return _dot(act_ref[...], wd_ref[...])


def _seg_sumsq(x, seg_ref):
    sq = x * x
    hi = sq.astype(_BF)
    lo = (sq - hi.astype(_F32)).astype(_BF)
    seg = seg_ref[...]
    parts = []
    for j in range(x.shape[1] // MXU_DIM):
        sl = slice(j * MXU_DIM, (j + 1) * MXU_DIM)
        parts.append(_dot(hi[:, sl], seg) + _dot(lo[:, sl], seg))
    return jnp.concatenate(parts, axis=1)


def _pre_kernel(x_ref, g1_ref, wg_ref, wu_ref, wd_ref, gmix_ref, win_ref,
                gaq_ref, gak_ref, seg64_ref,
                gcq_ref, wqt_ref, gqtab_ref, cost_ref, sint_ref,
                gckv_ref, wkn_ref, wvt_ref, gkn_ref, seg128_ref,
                gkr_ref, ck_ref, sk_ref,
                x1_ref, qa_ref, ka_ref, va_ref, qbt_ref, kb_ref, vbt_ref,
                act_ref):
    x = x_ref[...]
    h = _rms(x, g1_ref[...]).astype(_BF)
    x1 = x + 0.5 * _swiglu(h, wg_ref, wu_ref, wd_ref, act_ref)
    x1_ref[...] = x1
    h2 = _rms(x1, gmix_ref[...]).astype(_BF)

    qa = _dot(h2, win_ref[:, 0:A_WIDTH])
    qa_ref[0] = (qa * lax.rsqrt(_seg_sumsq(qa, seg64_ref) * (1.0 / A_HEAD_DIM) + EPS)
                 * gaq_ref[...]).astype(_BF)
    ka = _dot(h2, win_ref[:, A_WIDTH:2 * A_WIDTH])
    ka_ref[0] = (ka * lax.rsqrt(_seg_sumsq(ka, seg64_ref) * (1.0 / A_HEAD_DIM) + EPS)
                 * gak_ref[...]).astype(_BF)
    va_ref[0] = _dot(h2, win_ref[:, 2 * A_WIDTH:3 * A_WIDTH]).astype(_BF)

    o_cq = 3 * A_WIDTH
    cq = _dot(h2, win_ref[:, o_cq:o_cq + B_Q_LORA])
    cqn_t = _rms(cq, gcq_ref[...]).T.astype(_BF)
    q_t = _dot(wqt_ref[...], cqn_t)
    cos_t = cost_ref[...]
    sin_t = sint_ref[...]
    gq = gqtab_ref[...]
    half = B_ROPE // 2
    for hd in range(B_HEADS):
        r0 = hd * HEAD_PAD
        nope = q_t[r0:r0 + B_NOPE]
        nope = nope * lax.rsqrt(jnp.mean(nope * nope, axis=0, keepdims=True) + EPS) * gq[0:B_NOPE]
        rope = q_t[r0 + B_NOPE:r0 + B_QK_DIM]
        rope = (rope * lax.rsqrt(jnp.mean(rope * rope, axis=0, keepdims=True) + EPS)
                * gq[B_NOPE:B_QK_DIM])
        r1 = rope[0:half]
        r2 = rope[half:B_ROPE]
        blk = jnp.concatenate(
            [nope, r1 * cos_t - r2 * sin_t, r1 * sin_t + r2 * cos_t,
             jnp.zeros((HEAD_PAD - B_QK_DIM, nope.shape[1]), _F32)], axis=0)
        qbt_ref[0, hd] = blk.astype(_BF)

    o_ckv = o_cq + B_Q_LORA
    ckv = _dot(h2, win_ref[:, o_ckv:o_ckv + B_KV_LORA])
    ckvn = _rms(ckv, gckv_ref[...])
    v_t = _dot(wvt_ref[...], ckvn.T.astype(_BF))
    for hd in range(B_HEADS):
        vbt_ref[0, hd, 0] = v_t[hd * B_V_DIM:(hd + 1) * B_V_DIM].astype(_BF)
    kn = _dot(ckvn.astype(_BF), wkn_ref[...])
    kn = kn * lax.rsqrt(_seg_sumsq(kn, seg128_ref) * (1.0 / B_NOPE) + EPS) * gkn_ref[...]
    kr = _dot(h2, win_ref[:, KR_COL:KR_COL + LANES])
    y = kr * lax.rsqrt(jnp.sum(kr * kr, axis=-1, keepdims=True) * (1.0 / B_ROPE) + EPS) * gkr_ref[...]
    lane = lax.broadcasted_iota(jnp.int32, y.shape, 1)
    swapped = jnp.where(lane < B_NOPE + half,
                        pltpu.roll(y, LANES - half, axis=1), pltpu.roll(y, half, axis=1))
    kpe = y * ck_ref[...] + swapped * sk_ref[...]
    for hd in range(B_HEADS):
        kb_ref[0, hd] = (kn[:, hd * HEAD_PAD:(hd + 1) * HEAD_PAD] + kpe).astype(_BF)


def _attn_a_kernel(q_ref, k_ref, v_ref, bias_ref, o_ref):
    i = pl.program_id(1)
    left_tiles = A_LEFT_CHUNKS * CHUNK // A_Q_TILE
    start = pl.multiple_of(jnp.maximum(i - left_tiles, 0) * A_Q_TILE, A_Q_TILE)
    t0 = left_tiles - jnp.minimum(i, left_tiles)
    kwin = k_ref[0, pl.ds(start, A_WIN), :]
    vwin = v_ref[0, pl.ds(start, A_WIN), :]
    q = q_ref[0]
    lane = lax.broadcasted_iota(jnp.int32, (A_Q_TILE, LANES), 1)
    low = lane < A_HEAD_DIM
    for p in range(A_HEADS // 2):
        sl = slice(p * LANES, (p + 1) * LANES)
        qp, kp, vp = q[:, sl], kwin[:, sl], vwin[:, sl]
        o_pair = None
        for e in range(2):
            hd = 2 * p + e
            sel = low if e == 0 else jnp.logical_not(low)
            qm = jnp.where(sel, qp, jnp.zeros_like(qp))
            s = lax.dot_general(qm, kp, (((1,), (1,)), ((), ())), preferred_element_type=_F32)
            bias = jnp.concatenate(
                [bias_ref[hd, t0 + j] for j in range(A_WIN // LANES)], axis=1)
            s = s + bias
            m = jnp.max(s, axis=1, keepdims=True)
            pr = jnp.exp(s - m)
            l = jnp.sum(pr, axis=1, keepdims=True)
            o = _dot(pr.astype(_BF), vp) / l
            o_pair = o if e == 0 else jnp.where(low, o_pair, o)
        o_ref[0, :, sl] = o_pair.astype(_BF)


def _attn_b_kernel(qt_ref, k_ref, vt_ref, o_ref, m_ref, acc_ref):
    i = pl.program_id(1)
    t = TOK_TILE
    m_ref[...] = jnp.full(m_ref.shape, NEG_INF, _F32)
    acc_ref[...] = jnp.zeros(acc_ref.shape, _F32)
    ones_rows = jnp.ones((B_ACC_ROWS - B_V_DIM, t), _BF)

    def step(kt, masked):
        def scores(hd):
            k = k_ref[0, hd, pl.ds(pl.multiple_of(kt * t, t), t), :]
            return _dot(k, qt_ref[0, hd])

        s_next = scores(0)
        for hd in range(B_HEADS):
            s = s_next
            if hd + 1 < B_HEADS:
                s_next = scores(hd + 1)
            if masked:
                krow = lax.broadcasted_iota(jnp.int32, (t, t), 0) // CHUNK
                qcol = lax.broadcasted_iota(jnp.int32, (t, t), 1) // CHUNK
                s = jnp.where(krow <= qcol, s, NEG_INF)
            m = m_ref[hd:hd + 1, :]
            m_new = jnp.maximum(m, jnp.max(s, axis=0, keepdims=True))
            alpha = jnp.exp2(m - m_new)
            p = jnp.exp2(s - m_new).astype(_BF)
            m_ref[hd:hd + 1, :] = m_new
            v_aug = jnp.concatenate([vt_ref[0, hd, kt], ones_rows], axis=0)
            rows = slice(hd * B_ACC_ROWS, (hd + 1) * B_ACC_ROWS)
            acc_ref[rows, :] = alpha * acc_ref[rows, :] + _dot(v_aug, p)

    def full_step(kt, carry):
        step(kt, masked=False)
        return carry

    lax.fori_loop(0, i, full_step, 0)
    step(i, masked=True)
    outs = []
    for hd in range(B_HEADS):
        r0 = hd * B_ACC_ROWS
        outs.append(acc_ref[r0:r0 + B_V_DIM, :] / acc_ref[r0 + B_V_DIM:r0 + B_V_DIM + 1, :])
    o_ref[0] = jnp.concatenate(outs, axis=0).T.astype(_BF)


def _post_kernel(x1_ref, oa_ref, ob_ref, wout_ref, g2_ref, wg_ref, wu_ref, wd_ref,
                 gf_ref, y_ref, act_ref):
    x2 = (x1_ref[...] + _dot(oa_ref[...], wout_ref[0:A_WIDTH, :])
          + _dot(ob_ref[...], wout_ref[A_WIDTH:A_WIDTH + B_WIDTH, :]))
    h = _rms(x2, g2_ref[...]).astype(_BF)
    x3 = x2 + 0.5 * _swiglu(h, wg_ref, wu_ref, wd_ref, act_ref)
    y_ref[...] = _rms(x3, gf_ref[...])


def _const_spec(shape):
    n = len(shape)
    return pl.BlockSpec(shape, lambda *_: (0,) * n, pipeline_mode=pl.Buffered(1))


def _bias_kernel(tab_ref, o_ref):
    r = lax.broadcasted_iota(jnp.int32, (A_Q_TILE, LANES), 0)
    l = lax.broadcasted_iota(jnp.int32, (A_Q_TILE, LANES), 1)
    qc = r // CHUNK
    for tile in range(A_BIAS_TILES):
        a = A_BIAS_CENTER - (A_LEFT_CHUNKS * CHUNK - LANES * tile)
        w = jnp.concatenate([tab_ref[0, :, a:a + LANES], tab_ref[0, :, a - LANES:a]], axis=1)
        rolled = pltpu.roll(jnp.broadcast_to(w, (A_Q_TILE, 2 * LANES)), 0, 1,
                            stride=1, stride_axis=0)
        kc = (LANES * tile + l) // CHUNK - A_LEFT_CHUNKS
        valid = (kc <= qc) & (kc >= qc - A_LEFT_CHUNKS)
        o_ref[0, tile] = jnp.where(valid, rolled[:, :LANES], NEG_INF)


def _rel_bias_tiles(rel_bias):
    pad_hi = A_BIAS_CENTER - A_MAX_REL
    pad_lo = A_BIAS_TAB - pad_hi - (2 * A_MAX_REL + 1)
    rb = rel_bias.astype(_F32)
    table = jnp.concatenate([jnp.broadcast_to(rb[:, -1:], (A_HEADS, pad_hi)), rb[:, ::-1],
                             jnp.broadcast_to(rb[:, :1], (A_HEADS, pad_lo))], axis=1)
    return pl.pallas_call(
        _bias_kernel,
        grid=(A_HEADS,),
        in_specs=[pl.BlockSpec((1, 1, A_BIAS_TAB), lambda h: (h, 0, 0))],
        out_specs=pl.BlockSpec((1, A_BIAS_TILES, A_Q_TILE, LANES), lambda h: (h, 0, 0, 0)),
        out_shape=jax.ShapeDtypeStruct((A_HEADS, A_BIAS_TILES, A_Q_TILE, LANES), _F32),
        name="rel_bias",
    )(table.reshape(A_HEADS, 1, A_BIAS_TAB))


def kernel(x, ffn1_norm, ffn1_w_gate, ffn1_w_up, ffn1_w_down, mix_norm, w_in, a_q_norm, a_k_norm, a_rel_bias, b_q_lat_norm, b_w_uq, b_kv_lat_norm, b_w_ukv, b_q_nope_norm, b_q_rope_norm, b_k_nope_norm, b_k_rope_norm, w_out, ffn2_norm, ffn2_w_gate, ffn2_w_up, ffn2_w_down, final_norm):
    b, s, d = x.shape
    assert d == D_MODEL and s % TOK_TILE == 0 and ffn1_norm.shape[0] == 1
    tm = TOK_TILE
    nt = s // tm
    half = B_ROPE // 2

    def row(v):
        return v.astype(_F32)[None, :]

    w_in_l = w_in[0]
    kr_cols = jnp.zeros((d, LANES), _F32).at[:, B_NOPE:B_QK_DIM].set(w_in_l[:, KR_COL:KR_COL + B_ROPE])
    w_in_p = jnp.concatenate([w_in_l[:, :KR_COL], kr_cols], axis=1).astype(_BF)

    gaq = row(jnp.tile(a_q_norm[0], A_HEADS) * (A_HEAD_DIM ** -0.5))
    gak = row(jnp.tile(a_k_norm[0], A_HEADS))
    blk = jnp.arange(MXU_DIM)
    seg64 = (blk[:, None] // A_HEAD_DIM == blk[None, :] // A_HEAD_DIM).astype(_BF)
    seg128 = (blk[:, None] // HEAD_PAD == blk[None, :] // HEAD_PAD).astype(_BF)

    w_uq = b_w_uq[0].reshape(B_Q_LORA, B_HEADS, B_QK_DIM)
    w_uq = jnp.pad(w_uq, ((0, 0), (0, 0), (0, HEAD_PAD - B_QK_DIM)))
    wq_t = w_uq.reshape(B_Q_LORA, B_HEADS * HEAD_PAD).T.astype(_BF)
    scale_b = (B_QK_DIM ** -0.5) * LOG2E
    gq_col = jnp.concatenate([b_q_nope_norm[0] * scale_b, b_q_rope_norm[0] * scale_b,
                              jnp.zeros((HEAD_PAD - B_QK_DIM,), _F32)])
    gq_tab = jnp.broadcast_to(gq_col[:, None], (HEAD_PAD, tm)).astype(_F32)

    inv = 1.0 / (ROPE_THETA ** (jnp.arange(0, B_ROPE, 2, dtype=_F32) / B_ROPE))
    ang = jnp.arange(s, dtype=_F32)[:, None] * inv[None, :]
    cos, sin = jnp.cos(ang), jnp.sin(ang)
    cos_t, sin_t = cos.T, sin.T
    zpad = jnp.zeros((s, B_NOPE), _F32)
    zend = jnp.zeros((s, LANES - B_QK_DIM), _F32)
    ck = jnp.concatenate([zpad, cos, cos, zend], axis=1)
    sk = jnp.concatenate([zpad, -sin, sin, zend], axis=1)

    w_ukv = b_w_ukv[0].reshape(B_KV_LORA, B_HEADS, B_NOPE + B_V_DIM)
    wkn = jnp.pad(w_ukv[..., :B_NOPE], ((0, 0), (0, 0), (0, HEAD_PAD - B_NOPE)))
    wkn = wkn.reshape(B_KV_LORA, B_HEADS * HEAD_PAD).astype(_BF)
    wv_t = w_ukv[..., B_NOPE:].reshape(B_KV_LORA, B_WIDTH).T.astype(_BF)
    gkn = row(jnp.tile(jnp.concatenate([b_k_nope_norm[0], jnp.zeros((HEAD_PAD - B_NOPE,), _F32)]),
                       B_HEADS))
    gkr = row(jnp.concatenate([jnp.zeros((B_NOPE,), _F32), b_k_rope_norm[0],
                               jnp.zeros((LANES - B_QK_DIM,), _F32)]))

    bias_tiles = _rel_bias_tiles(a_rel_bias[0])

    tok = lambda bi, j: (bi, j, 0)
    x3d = x
    pre_in_specs = [
        pl.BlockSpec((None, tm, d), tok),
        _const_spec((1, d)), _const_spec((d, D_FF)), _const_spec((d, D_FF)), _const_spec((D_FF, d)),
        _const_spec((1, d)), _const_spec((d, IN_COLS_PAD)),
        _const_spec((1, A_WIDTH)), _const_spec((1, A_WIDTH)), _const_spec((MXU_DIM, MXU_DIM)),
        _const_spec((1, B_Q_LORA)), _const_spec((B_HEADS * HEAD_PAD, B_Q_LORA)),
        _const_spec((HEAD_PAD, tm)),
        pl.BlockSpec((half, tm), lambda bi, j: (0, j)), pl.BlockSpec((half, tm), lambda bi, j: (0, j)),
        _const_spec((1, B_KV_LORA)), _const_spec((B_KV_LORA, B_HEADS * HEAD_PAD)),
        _const_spec((B_WIDTH, B_KV_LORA)), _const_spec((1, B_HEADS * HEAD_PAD)),
        _const_spec((MXU_DIM, MXU_DIM)),
        _const_spec((1, LANES)),
        pl.BlockSpec((tm, LANES), lambda bi, j: (j, 0)), pl.BlockSpec((tm, LANES), lambda bi, j: (j, 0)),
    ]
    pre_out_shapes = [
        jax.ShapeDtypeStruct((b, s, d), _F32),
        jax.ShapeDtypeStruct((b, s, A_WIDTH), _BF),
        jax.ShapeDtypeStruct((b, s, A_WIDTH), _BF),
        jax.ShapeDtypeStruct((b, s, A_WIDTH), _BF),
        jax.ShapeDtypeStruct((b, B_HEADS, HEAD_PAD, s), _BF),
        jax.ShapeDtypeStruct((b, B_HEADS, s, HEAD_PAD), _BF),
        jax.ShapeDtypeStruct((b, B_HEADS, nt, B_V_DIM, tm), _BF),
    ]
    pre_out_specs = [
        pl.BlockSpec((None, tm, d), tok),
        pl.BlockSpec((1, tm, A_WIDTH), tok),
        pl.BlockSpec((1, tm, A_WIDTH), tok),
        pl.BlockSpec((1, tm, A_WIDTH), tok),
        pl.BlockSpec((1, B_HEADS, HEAD_PAD, tm), lambda bi, j: (bi, 0, 0, j)),
        pl.BlockSpec((1, B_HEADS, tm, HEAD_PAD), lambda bi, j: (bi, 0, j, 0)),
        pl.BlockSpec((1, B_HEADS, 1, B_V_DIM, tm), lambda bi, j: (bi, 0, j, 0, 0)),
    ]
    x1, qa, ka, va, qbt, kb, vbt = pl.pallas_call(
        _pre_kernel,
        grid=(b, nt),
        in_specs=pre_in_specs,
        out_specs=pre_out_specs,
        out_shape=pre_out_shapes,
        scratch_shapes=[pltpu.VMEM((tm, D_FF), _BF)],
        compiler_params=pltpu.CompilerParams(
            dimension_semantics=("arbitrary", "arbitrary"), vmem_limit_bytes=VMEM_LIMIT),
        name="pre",
    )(x3d, row(ffn1_norm[0]), ffn1_w_gate[0].astype(_BF), ffn1_w_up[0].astype(_BF),
      ffn1_w_down[0].astype(_BF), row(mix_norm[0]), w_in_p,
      gaq, gak, seg64,
      row(b_q_lat_norm[0]), wq_t, gq_tab, cos_t, sin_t,
      row(b_kv_lat_norm[0]), wkn, wv_t, gkn, seg128,
      gkr, ck, sk)

    out_a = pl.pallas_call(
        _attn_a_kernel,
        grid=(b, s // A_Q_TILE),
        in_specs=[
            pl.BlockSpec((1, A_Q_TILE, A_WIDTH), tok),
            pl.BlockSpec((1, s, A_WIDTH), lambda bi, j: (bi, 0, 0)),
            pl.BlockSpec((1, s, A_WIDTH), lambda bi, j: (bi, 0, 0)),
            _const_spec((A_HEADS, A_BIAS_TILES, A_Q_TILE, LANES)),
        ],
        out_specs=pl.BlockSpec((1, A_Q_TILE, A_WIDTH), tok),
        out_shape=jax.ShapeDtypeStruct((b, s, A_WIDTH), _BF),
        compiler_params=pltpu.CompilerParams(
            dimension_semantics=("arbitrary", "arbitrary"), vmem_limit_bytes=VMEM_LIMIT),
        name="attn_a",
    )(qa, ka, va, bias_tiles)

    out_b = pl.pallas_call(
        _attn_b_kernel,
        grid=(b, nt),
        in_specs=[
            pl.BlockSpec((1, B_HEADS, HEAD_PAD, tm), lambda bi, j: (bi, 0, 0, j)),
            pl.BlockSpec((1, B_HEADS, s, HEAD_PAD), lambda bi, j: (bi, 0, 0, 0),
                         pipeline_mode=pl.Buffered(1)),
            pl.BlockSpec((1, B_HEADS, nt, B_V_DIM, tm), lambda bi, j: (bi, 0, 0, 0, 0),
                         pipeline_mode=pl.Buffered(1)),
        ],
        out_specs=pl.BlockSpec((1, tm, B_WIDTH), tok),
        out_shape=jax.ShapeDtypeStruct((b, s, B_WIDTH), _BF),
        scratch_shapes=[pltpu.VMEM((B_HEADS, tm), _F32),
                        pltpu.VMEM((B_HEADS * B_ACC_ROWS, tm), _F32)],
        compiler_params=pltpu.CompilerParams(
            dimension_semantics=("arbitrary", "arbitrary"), vmem_limit_bytes=VMEM_LIMIT),
        name="attn_b",
    )(qbt, kb, vbt)

    n = b * s
    flat = lambda j: (j, 0)
    y = pl.pallas_call(
        _post_kernel,
        grid=(n // tm,),
        in_specs=[
            pl.BlockSpec((tm, d), flat),
            pl.BlockSpec((tm, A_WIDTH), flat),
            pl.BlockSpec((tm, B_WIDTH), flat),
            _const_spec((A_WIDTH + B_WIDTH, d)),
            _const_spec((1, d)), _const_spec((d, D_FF)), _const_spec((d, D_FF)), _const_spec((D_FF, d)),
            _const_spec((1, d)),
        ],
        out_specs=pl.BlockSpec((tm, d), flat),
        out_shape=jax.ShapeDtypeStruct((n, d), _F32),
        scratch_shapes=[pltpu.VMEM((tm, D_FF), _BF)],
        compiler_params=pltpu.CompilerParams(
            dimension_semantics=("arbitrary",), vmem_limit_bytes=VMEM_LIMIT),
        name="post",
    )(x1.reshape(n, d), out_a.reshape(n, A_WIDTH), out_b.reshape(n, B_WIDTH),
      w_out[0].astype(_BF), row(ffn2_norm[0]), ffn2_w_gate[0].astype(_BF),
      ffn2_w_up[0].astype(_BF), ffn2_w_down[0].astype(_BF), row(final_norm[0]))
    return y.reshape(b, s, d)
```

```python
import jax
import jax.numpy as jnp
from jax import lax
from jax.experimental import pallas as pl
from jax.experimental.pallas import tpu as pltpu

D_MODEL = 1024
D_FF = 2816
CHUNK = 64
A_HEADS = 8
A_HEAD_DIM = 64
A_LEFT_CHUNKS = 8
A_MAX_REL = 128
A_WIDTH = A_HEADS * A_HEAD_DIM
B_HEADS = 8
B_Q_LORA = 256
B_KV_LORA = 128
B_NOPE = 64
B_ROPE = 32
B_V_DIM = 64
B_QK_DIM = B_NOPE + B_ROPE
B_WIDTH = B_HEADS * B_V_DIM
ROPE_THETA = 10000.0
EPS = 1e-6
NEG_INF = -1e30
LOG2E = 1.4426950408889634

LANES = 128
BF16_SUBLANES = 16
MXU_DIM = 256
TOK_TILE = 512
FF_CHUNK = 256
QK_AHEAD = 2

A_Q_TILE = 256
A_LEFT = A_LEFT_CHUNKS * CHUNK
A_WIN = A_Q_TILE + A_LEFT
A_BIAS_ROWS = A_WIN + A_LEFT
A_BIAS_TILES = A_BIAS_ROWS // LANES
A_TAB_ZERO = A_BIAS_ROWS - A_LEFT + LANES
A_TAB_LEN = A_TAB_ZERO + A_LEFT + A_Q_TILE

HEAD_PAD = 128
ACC_ROWS = B_V_DIM + BF16_SUBLANES
FM_ROWS = 2 * A_WIDTH + B_Q_LORA
TM_COLS = A_WIDTH + B_KV_LORA + LANES
VMEM_LIMIT = 60 * 1024 * 1024

_BF = jnp.bfloat16
_F32 = jnp.float32


def _dot(a, b):
    return jnp.dot(a, b, preferred_element_type=_F32)


def _rms(x, g):
    ms = jnp.mean(x * x, axis=-1, keepdims=True)
    return x * lax.rsqrt(ms + EPS) * g


def _col_rms(x_t):
    return x_t * lax.rsqrt(jnp.mean(x_t * x_t, axis=0, keepdims=True) + EPS)


def _swiglu(h, wg_ref, wu_ref, wd_ref, act_ref):
    for c in range(D_FF // FF_CHUNK):
        sl = slice(c * FF_CHUNK, (c + 1) * FF_CHUNK)
        g = _dot(h, wg_ref[:, sl])
        u = _dot(h, wu_ref[:, sl])
        act_ref[:, sl] = (jax.nn.silu(g) * u).astype(_BF)
    return _dot(act_ref[...], wd_ref[...])


def _seg_sumsq(x, seg_ref):
    sq = x * x
    hi = sq.astype(_BF)
    lo = (sq - hi.astype(_F32)).astype(_BF)
    seg = seg_ref[...]
    parts = []
    for j in range(x.shape[1] // MXU_DIM):
        sl = slice(j * MXU_DIM, (j + 1) * MXU_DIM)
        parts.append(_dot(hi[:, sl], seg) + _dot(lo[:, sl], seg))
    return jnp.concatenate(parts, axis=1)


def _pre_kernel(x_ref, g1_ref, wg_ref, wu_ref, wd_ref, gmix_ref, wfm_ref, wtm_ref,
                gaq_ref, gak_ref, seg64_ref,
                wqt_ref, gqtab_ref, cost_ref, sint_ref,
                gckv_ref, wkn_ref, wvt_ref, gkn_ref, seg128_ref,
                gkr_ref, ck_ref, sk_ref,
                x1_ref, qat_ref, ka_ref, vat_ref, qbt_ref, kb_ref, vbt_ref,
                act_ref):
    x = x_ref[...]
    h = _rms(x, g1_ref[...]).astype(_BF)
    x1 = x + 0.5 * _swiglu(h, wg_ref, wu_ref, wd_ref, act_ref)
    x1_ref[...] = x1
    h2f = _rms(x1, gmix_ref[...])
    h2 = h2f.astype(_BF)
    tm = h2.shape[0]

    fm = _dot(wfm_ref[...], h2f.T.astype(_BF))
    gaq = gaq_ref[...]
    for hd in range(A_HEADS):
        rows = slice(hd * A_HEAD_DIM, (hd + 1) * A_HEAD_DIM)
        qat_ref[0, rows, :] = (_col_rms(fm[rows]) * gaq).astype(_BF)
        for g in range(tm // A_Q_TILE):
            vat_ref[0, hd, g] = fm[A_WIDTH + hd * A_HEAD_DIM:A_WIDTH + (hd + 1) * A_HEAD_DIM,
                                   g * A_Q_TILE:(g + 1) * A_Q_TILE].astype(_BF)

    q_t = _dot(wqt_ref[...], _col_rms(fm[2 * A_WIDTH:FM_ROWS]).astype(_BF))
    cos_t = cost_ref[...]
    sin_t = sint_ref[...]
    gq = gqtab_ref[...]
    half = B_ROPE // 2
    for hd in range(B_HEADS):
        r0 = hd * HEAD_PAD
        nope = _col_rms(q_t[r0:r0 + B_NOPE]) * gq[0:B_NOPE]
        rope = _col_rms(q_t[r0 + B_NOPE:r0 + B_QK_DIM]) * gq[B_NOPE:B_QK_DIM]
        r1 = rope[0:half]
        r2 = rope[half:B_ROPE]
        blk = jnp.concatenate(
            [nope, r1 * cos_t - r2 * sin_t, r1 * sin_t + r2 * cos_t,
             jnp.zeros((HEAD_PAD - B_QK_DIM, tm), _F32)], axis=0)
        qbt_ref[0, hd] = blk.astype(_BF)

    tmj = _dot(h2, wtm_ref[...])
    ka = tmj[:, 0:A_WIDTH]
    ka_ref[0] = (ka * lax.rsqrt(_seg_sumsq(ka, seg64_ref) * (1.0 / A_HEAD_DIM) + EPS)
                 * gak_ref[...]).astype(_BF)

    ckvn = _rms(tmj[:, A_WIDTH:A_WIDTH + B_KV_LORA], gckv_ref[...])
    v_t = _dot(wvt_ref[...], ckvn.T.astype(_BF))
    for hd in range(B_HEADS):
        vbt_ref[0, hd, 0] = v_t[hd * B_V_DIM:(hd + 1) * B_V_DIM].astype(_BF)
    kn = _dot(ckvn.astype(_BF), wkn_ref[...])
    kn = kn * lax.rsqrt(_seg_sumsq(kn, seg128_ref) * (1.0 / B_NOPE) + EPS) * gkn_ref[...]
    kr = tmj[:, A_WIDTH + B_KV_LORA:TM_COLS]
    y = kr * lax.rsqrt(jnp.sum(kr * kr, axis=-1, keepdims=True) * (1.0 / B_ROPE) + EPS) * gkr_ref[...]
    lane = lax.broadcasted_iota(jnp.int32, y.shape, 1)
    swapped = jnp.where(lane < B_NOPE + half,
                        pltpu.roll(y, LANES - half, axis=1), pltpu.roll(y, half, axis=1))
    kpe = y * ck_ref[...] + swapped * sk_ref[...]
    for hd in range(B_HEADS):
        kb_ref[0, hd] = (kn[:, hd * HEAD_PAD:(hd + 1) * HEAD_PAD] + kpe).astype(_BF)


def _bias_kernel(tab_ref, o_ref):
    u = lax.broadcasted_iota(jnp.int32, (LANES, A_Q_TILE), 0)
    qc = lax.broadcasted_iota(jnp.int32, (LANES, A_Q_TILE), 1) // CHUNK
    for tile in range(A_BIAS_TILES):
        a = A_TAB_ZERO + A_LEFT - LANES * tile
        w = jnp.concatenate([tab_ref[0, :, a:a + A_Q_TILE], tab_ref[0, :, a - A_Q_TILE:a]], axis=1)
        rolled = pltpu.roll(jnp.broadcast_to(w, (LANES, 2 * A_Q_TILE)), 0, 1,
                            stride=1, stride_axis=0)
        kc = (LANES * tile + u) // CHUNK - A_LEFT_CHUNKS
        valid = (kc <= qc) & (kc >= qc - A_LEFT_CHUNKS)
        o_ref[0, tile] = jnp.where(valid, rolled[:, :A_Q_TILE], NEG_INF)


def _rel_bias_tiles(rel_bias):
    pad_lo = A_TAB_ZERO - A_MAX_REL
    pad_hi = A_TAB_LEN - pad_lo - (2 * A_MAX_REL + 1)
    rb = rel_bias.astype(_F32) * LOG2E
    table = jnp.concatenate([jnp.broadcast_to(rb[:, :1], (A_HEADS, pad_lo)), rb,
                             jnp.broadcast_to(rb[:, -1:], (A_HEADS, pad_hi))], axis=1)
    return pl.pallas_call(
        _bias_kernel,
        grid=(A_HEADS,),
        in_specs=[pl.BlockSpec((1, 1, A_TAB_LEN), lambda h: (h, 0, 0))],
        out_specs=pl.BlockSpec((1, A_BIAS_TILES, LANES, A_Q_TILE), lambda h: (h, 0, 0, 0)),
        out_shape=jax.ShapeDtypeStruct((A_HEADS, A_BIAS_TILES, LANES, A_Q_TILE), _F32),
        name="rel_bias",
    )(table.reshape(A_HEADS, 1, A_TAB_LEN))


def _attn_a_kernel(qt_ref, k_ref, vt_ref, bias_ref, o_ref):
    i = pl.program_id(1)
    left_tiles = A_LEFT // A_Q_TILE
    g0 = jnp.maximum(i - left_tiles, 0)
    start = pl.multiple_of(g0 * A_Q_TILE, A_Q_TILE)
    u0 = (left_tiles - jnp.minimum(i, left_tiles)) * (A_Q_TILE // LANES)
    zeros = jnp.zeros((A_HEAD_DIM, A_Q_TILE), _BF)
    ones_rows = jnp.ones((ACC_ROWS - A_HEAD_DIM, A_WIN), _BF)

    def scores(pair):
        q2 = qt_ref[0, pair * LANES:(pair + 1) * LANES, :]
        q_even = jnp.concatenate([q2[:A_HEAD_DIM], zeros], axis=0)
        q_odd = jnp.concatenate([zeros, q2[A_HEAD_DIM:]], axis=0)
        k = k_ref[0, pl.ds(start, A_WIN), pair * LANES:(pair + 1) * LANES]
        return _dot(k, jnp.concatenate([q_even, q_odd], axis=1))

    n_pairs = A_HEADS // 2
    pending = [scores(0)]
    outs = []
    for pair in range(n_pairs):
        s = pending.pop(0)
        if pair + 1 < n_pairs:
            pending.append(scores(pair + 1))
        bias = jnp.concatenate(
            [jnp.concatenate([bias_ref[2 * pair + e, u0 + j] for j in range(A_WIN // LANES)], axis=0)
             for e in range(2)], axis=1)
        s = s + bias
        p = jnp.exp2(s - jnp.max(s, axis=0, keepdims=True)).astype(_BF)
        for e in range(2):
            hd = 2 * pair + e
            v = jnp.concatenate([vt_ref[0, hd, g0 + g] for g in range(A_WIN // A_Q_TILE)], axis=1)
            o = _dot(jnp.concatenate([v, ones_rows], axis=0),
                     p[:, e * A_Q_TILE:(e + 1) * A_Q_TILE])
            outs.append(o[:A_HEAD_DIM] / o[A_HEAD_DIM:A_HEAD_DIM + 1])
    o_ref[0] = jnp.concatenate(outs, axis=0).T.astype(_BF)


def _attn_b_kernel(qt_ref, k_ref, vt_ref, o_ref, m_ref, acc_ref):
    i = pl.program_id(1)
    t = TOK_TILE
    m_ref[...] = jnp.full(m_ref.shape, NEG_INF, _F32)
    acc_ref[...] = jnp.zeros(acc_ref.shape, _F32)
    ones_rows = jnp.ones((ACC_ROWS - B_V_DIM, t), _BF)

    def step(kt, masked):
        def scores(hd):
            k = k_ref[0, hd, pl.ds(pl.multiple_of(kt * t, t), t), :]
            return _dot(k, qt_ref[0, hd])

        pending = [scores(hd) for hd in range(QK_AHEAD)]
        for hd in range(B_HEADS):
            s = pending.pop(0)
            if hd + QK_AHEAD < B_HEADS:
                pending.append(scores(hd + QK_AHEAD))
            if masked:
                krow = lax.broadcasted_iota(jnp.int32, (t, t), 0) // CHUNK
                qcol = lax.broadcasted_iota(jnp.int32, (t, t), 1) // CHUNK
                s = jnp.where(krow <= qcol, s, NEG_INF)
            m = m_ref[hd:hd + 1, :]
            m_new = jnp.maximum(m, jnp.max(s, axis=0, keepdims=True))
            alpha = jnp.exp2(m - m_new)
            p = jnp.exp2(s - m_new).astype(_BF)
            m_ref[hd:hd + 1, :] = m_new
            v_aug = jnp.concatenate([vt_ref[0, hd, kt], ones_rows], axis=0)
            rows = slice(hd * ACC_ROWS, (hd + 1) * ACC_ROWS)
            acc_ref[rows, :] = alpha * acc_ref[rows, :] + _dot(v_aug, p)

    def full_step(kt, carry):
        step(kt, masked=False)
        return carry

    lax.fori_loop(0, i, full_step, 0)
    step(i, masked=True)
    outs = []
    for hd in range(B_HEADS):
        r0 = hd * ACC_ROWS
        outs.append(acc_ref[r0:r0 + B_V_DIM, :] / acc_ref[r0 + B_V_DIM:r0 + B_V_DIM + 1, :])
    o_ref[0] = jnp.concatenate(outs, axis=0).T.astype(_BF)


def _post_kernel(x1_ref, oa_ref, ob_ref, wout_ref, g2_ref, wg_ref, wu_ref, wd_ref,
                 gf_ref, y_ref, act_ref):
    x2 = (x1_ref[...] + _dot(oa_ref[...], wout_ref[0:A_WIDTH, :])
          + _dot(ob_ref[...], wout_ref[A_WIDTH:A_WIDTH + B_WIDTH, :]))
    h = _rms(x2, g2_ref[...]).astype(_BF)
    x3 = x2 + 0.5 * _swiglu(h, wg_ref, wu_ref, wd_ref, act_ref)
    y_ref[...] = _rms(x3, gf_ref[...])


def _const_spec(shape):
    n = len(shape)
    return pl.BlockSpec(shape, lambda *_: (0,) * n, pipeline_mode=pl.Buffered(1))


def kernel(x, ffn1_norm, ffn1_w_gate, ffn1_w_up, ffn1_w_down, mix_norm, w_in, a_q_norm, a_k_norm, a_rel_bias, b_q_lat_norm, b_w_uq, b_kv_lat_norm, b_w_ukv, b_q_nope_norm, b_q_rope_norm, b_k_nope_norm, b_k_rope_norm, w_out, ffn2_norm, ffn2_w_gate, ffn2_w_up, ffn2_w_down, final_norm):
    b, s, d = x.shape
    assert d == D_MODEL and s % TOK_TILE == 0 and s >= A_WIN and ffn1_norm.shape[0] == 1
    tm = TOK_TILE
    nt = s // tm
    half = B_ROPE // 2

    def row(v):
        return v.astype(_F32)[None, :]

    w_in_l = w_in[0]
    o_va, o_cq = 2 * A_WIDTH, 3 * A_WIDTH
    o_ckv = o_cq + B_Q_LORA
    o_kr = o_ckv + B_KV_LORA
    w_fm = jnp.concatenate([w_in_l[:, 0:A_WIDTH], w_in_l[:, o_va:o_cq], w_in_l[:, o_cq:o_ckv]],
                           axis=1).T.astype(_BF)
    kr_cols = jnp.zeros((d, LANES), _F32).at[:, B_NOPE:B_QK_DIM].set(w_in_l[:, o_kr:o_kr + B_ROPE])
    w_tm = jnp.concatenate([w_in_l[:, A_WIDTH:o_va], w_in_l[:, o_ckv:o_kr], kr_cols],
                           axis=1).astype(_BF)

    gaq_tab = jnp.broadcast_to((a_q_norm[0] * (A_HEAD_DIM ** -0.5 * LOG2E))[:, None],
                               (A_HEAD_DIM, tm)).astype(_F32)
    gak = row(jnp.tile(a_k_norm[0], A_HEADS))
    blk = jnp.arange(MXU_DIM)
    seg64 = (blk[:, None] // A_HEAD_DIM == blk[None, :] // A_HEAD_DIM).astype(_BF)
    seg128 = (blk[:, None] // HEAD_PAD == blk[None, :] // HEAD_PAD).astype(_BF)

    w_uq = (b_w_uq[0] * b_q_lat_norm[0][:, None]).reshape(B_Q_LORA, B_HEADS, B_QK_DIM)
    w_uq = jnp.pad(w_uq, ((0, 0), (0, 0), (0, HEAD_PAD - B_QK_DIM)))
    wq_t = w_uq.reshape(B_Q_LORA, B_HEADS * HEAD_PAD).T.astype(_BF)
    scale_b = (B_QK_DIM ** -0.5) * LOG2E
    gq_col = jnp.concatenate([b_q_nope_norm[0] * scale_b, b_q_rope_norm[0] * scale_b,
                              jnp.zeros((HEAD_PAD - B_QK_DIM,), _F32)])
    gq_tab = jnp.broadcast_to(gq_col[:, None], (HEAD_PAD, tm)).astype(_F32)

    inv = 1.0 / (ROPE_THETA ** (jnp.arange(0, B_ROPE, 2, dtype=_F32) / B_ROPE))
    ang = jnp.arange(s, dtype=_F32)[:, None] * inv[None, :]
    cos, sin = jnp.cos(ang), jnp.sin(ang)
    cos_t, sin_t = cos.T, sin.T
    zpad = jnp.zeros((s, B_NOPE), _F32)
    zend = jnp.zeros((s, LANES - B_QK_DIM), _F32)
    ck = jnp.concatenate([zpad, cos, cos, zend], axis=1)
    sk = jnp.concatenate([zpad, -sin, sin, zend], axis=1)

    w_ukv = b_w_ukv[0].reshape(B_KV_LORA, B_HEADS, B_NOPE + B_V_DIM)
    wkn = jnp.pad(w_ukv[..., :B_NOPE], ((0, 0), (0, 0), (0, HEAD_PAD - B_NOPE)))
    wkn = wkn.reshape(B_KV_LORA, B_HEADS * HEAD_PAD).astype(_BF)
    wv_t = w_ukv[..., B_NOPE:].reshape(B_KV_LORA, B_WIDTH).T.astype(_BF)
    gkn = row(jnp.tile(jnp.concatenate([b_k_nope_norm[0], jnp.zeros((HEAD_PAD - B_NOPE,), _F32)]),
                       B_HEADS))
    gkr = row(jnp.concatenate([jnp.zeros((B_NOPE,), _F32), b_k_rope_norm[0],
                               jnp.zeros((LANES - B_QK_DIM,), _F32)]))

    bias_tiles = _rel_bias_tiles(a_rel_bias[0])

    tok = lambda bi, j: (bi, j, 0)
    pre_in_specs = [
        pl.BlockSpec((None, tm, d), tok),
        _const_spec((1, d)), _const_spec((d, D_FF)), _const_spec((d, D_FF)), _const_spec((D_FF, d)),
        _const_spec((1, d)), _const_spec((FM_ROWS, d)), _const_spec((d, TM_COLS)),
        _const_spec((A_HEAD_DIM, tm)), _const_spec((1, A_WIDTH)), _const_spec((MXU_DIM, MXU_DIM)),
        _const_spec((B_HEADS * HEAD_PAD, B_Q_LORA)), _const_spec((HEAD_PAD, tm)),
        pl.BlockSpec((half, tm), lambda bi, j: (0, j)), pl.BlockSpec((half, tm), lambda bi, j: (0, j)),
        _const_spec((1, B_KV_LORA)), _const_spec((B_KV_LORA, B_HEADS * HEAD_PAD)),
        _const_spec((B_WIDTH, B_KV_LORA)), _const_spec((1, B_HEADS * HEAD_PAD)),
        _const_spec((MXU_DIM, MXU_DIM)),
        _const_spec((1, LANES)),
        pl.BlockSpec((tm, LANES), lambda bi, j: (j, 0)), pl.BlockSpec((tm, LANES), lambda bi, j: (j, 0)),
    ]
    pre_out_shapes = [
        jax.ShapeDtypeStruct((b, s, d), _F32),
        jax.ShapeDtypeStruct((b, A_WIDTH, s), _BF),
        jax.ShapeDtypeStruct((b, s, A_WIDTH), _BF),
        jax.ShapeDtypeStruct((b, A_HEADS, s // A_Q_TILE, A_HEAD_DIM, A_Q_TILE), _BF),
        jax.ShapeDtypeStruct((b, B_HEADS, HEAD_PAD, s), _BF),
        jax.ShapeDtypeStruct((b, B_HEADS, s, HEAD_PAD), _BF),
        jax.ShapeDtypeStruct((b, B_HEADS, nt, B_V_DIM, tm), _BF),
    ]
    pre_out_specs = [
        pl.BlockSpec((None, tm, d), tok),
        pl.BlockSpec((1, A_WIDTH, tm), lambda bi, j: (bi, 0, j)),
        pl.BlockSpec((1, tm, A_WIDTH), tok),
        pl.BlockSpec((1, A_HEADS, tm // A_Q_TILE, A_HEAD_DIM, A_Q_TILE),
                     lambda bi, j: (bi, 0, j, 0, 0)),
        pl.BlockSpec((1, B_HEADS, HEAD_PAD, tm), lambda bi, j: (bi, 0, 0, j)),
        pl.BlockSpec((1, B_HEADS, tm, HEAD_PAD), lambda bi, j: (bi, 0, j, 0)),
        pl.BlockSpec((1, B_HEADS, 1, B_V_DIM, tm), lambda bi, j: (bi, 0, j, 0, 0)),
    ]
    x1, qat, ka, vat, qbt, kb, vbt = pl.pallas_call(
        _pre_kernel,
        grid=(b, nt),
        in_specs=pre_in_specs,
        out_specs=pre_out_specs,
        out_shape=pre_out_shapes,
        scratch_shapes=[pltpu.VMEM((tm, D_FF), _BF)],
        compiler_params=pltpu.CompilerParams(
            dimension_semantics=("arbitrary", "arbitrary"), vmem_limit_bytes=VMEM_LIMIT),
        name="pre",
    )(x, row(ffn1_norm[0]), ffn1_w_gate[0].astype(_BF), ffn1_w_up[0].astype(_BF),
      ffn1_w_down[0].astype(_BF), row(mix_norm[0]), w_fm, w_tm,
      gaq_tab, gak, seg64,
      wq_t, gq_tab, cos_t, sin_t,
      row(b_kv_lat_norm[0]), wkn, wv_t, gkn, seg128,
      gkr, ck, sk)

    na = s // A_Q_TILE
    out_a = pl.pallas_call(
        _attn_a_kernel,
        grid=(b, na),
        in_specs=[
            pl.BlockSpec((1, A_WIDTH, A_Q_TILE), lambda bi, j: (bi, 0, j)),
            pl.BlockSpec((1, s, A_WIDTH), lambda bi, j: (bi, 0, 0), pipeline_mode=pl.Buffered(1)),
            pl.BlockSpec((1, A_HEADS, na, A_HEAD_DIM, A_Q_TILE), lambda bi, j: (bi, 0, 0, 0, 0),
                         pipeline_mode=pl.Buffered(1)),
            _const_spec((A_HEADS, A_BIAS_TILES, LANES, A_Q_TILE)),
        ],
        out_specs=pl.BlockSpec((1, A_Q_TILE, A_WIDTH), tok),
        out_shape=jax.ShapeDtypeStruct((b, s, A_WIDTH), _BF),
        compiler_params=pltpu.CompilerParams(
            dimension_semantics=("arbitrary", "arbitrary"), vmem_limit_bytes=VMEM_LIMIT),
        name="attn_a",
    )(qat, ka, vat, bias_tiles)

    out_b = pl.pallas_call(
        _attn_b_kernel,
        grid=(b, nt),
        in_specs=[
            pl.BlockSpec((1, B_HEADS, HEAD_PAD, tm), lambda bi, j: (bi, 0, 0, j)),
            pl.BlockSpec((1, B_HEADS, s, HEAD_PAD), lambda bi, j: (bi, 0, 0, 0),
                         pipeline_mode=pl.Buffered(1)),
            pl.BlockSpec((1, B_HEADS, nt, B_V_DIM, tm), lambda bi, j: (bi, 0, 0, 0, 0),
                         pipeline_mode=pl.Buffered(1)),
        ],
        out_specs=pl.BlockSpec((1, tm, B_WIDTH), tok),
        out_shape=jax.ShapeDtypeStruct((b, s, B_WIDTH), _BF),
        scratch_shapes=[pltpu.VMEM((B_HEADS, tm), _F32),
                        pltpu.VMEM((B_HEADS * ACC_ROWS, tm), _F32)],
        compiler_params=pltpu.CompilerParams(
            dimension_semantics=("arbitrary", "arbitrary"), vmem_limit_bytes=VMEM_LIMIT),
        name="attn_b",
    )(qbt, kb, vbt)

    n = b * s
    flat = lambda j: (j, 0)
    y = pl.pallas_call(
        _post_kernel,
        grid=(n // tm,),
        in_specs=[
            pl.BlockSpec((tm, d), flat),
            pl.BlockSpec((tm, A_WIDTH), flat),
            pl.BlockSpec((tm, B_WIDTH), flat),
            _const_spec((A_WIDTH + B_WIDTH, d)),
            _const_spec((1, d)), _const_spec((d, D_FF)), _const_spec((d, D_FF)), _const_spec((D_FF, d)),
            _const_spec((1, d)),
        ],
        out_specs=pl.BlockSpec((tm, d), flat),
        out_shape=jax.ShapeDtypeStruct((n, d), _F32),
        scratch_shapes=[pltpu.VMEM((tm, D_FF), _BF)],
        compiler_params=pltpu.CompilerParams(
            dimension_semantics=("arbitrary",), vmem_limit_bytes=VMEM_LIMIT),
        name="post",
    )(x1.reshape(n, d), out_a.reshape(n, A_WIDTH), out_b.reshape(n, B_WIDTH),
      w_out[0].astype(_BF), row(ffn2_norm[0]), ffn2_w_gate[0].astype(_BF),
      ffn2_w_up[0].astype(_BF), ffn2_w_down[0].astype(_BF), row(final_norm[0]))
    return y.reshape(b, s, d)
```

```python
import jax
import jax.numpy as jnp
from jax import lax
from jax.experimental import pallas as pl
from jax.experimental.pallas import tpu as pltpu

D_MODEL = 1024
D_FF = 2816
CHUNK = 64
A_HEADS = 8
A_HEAD_DIM = 64
A_LEFT_CHUNKS = 8
A_MAX_REL = 128
A_WIDTH = A_HEADS * A_HEAD_DIM
B_HEADS = 8
B_Q_LORA = 256
B_KV_LORA = 128
B_NOPE = 64
B_ROPE = 32
B_V_DIM = 64
B_QK_DIM = B_NOPE + B_ROPE
B_WIDTH = B_HEADS * B_V_DIM
ROPE_THETA = 10000.0
EPS = 1e-6
NEG_INF = -1e30
LOG2E = 1.4426950408889634
BOUND_SLACK = 1.0 + 2.0 ** -5
SAFE_DENOM = 2.0 ** -80

LANES = 128
BF16_SUBLANES = 16
MXU_DIM = 256
TOK_TILE = 512
FF_CHUNK = 256
QK_AHEAD = 2

A_Q_TILE = 256
A_LEFT = A_LEFT_CHUNKS * CHUNK
A_WIN = A_Q_TILE + A_LEFT
A_BIAS_ROWS = A_WIN + A_LEFT
A_BIAS_TILES = A_BIAS_ROWS // LANES
A_TAB_ZERO = A_BIAS_ROWS - A_LEFT + LANES
A_TAB_LEN = A_TAB_ZERO + A_LEFT + A_Q_TILE

HEAD_PAD = 128
ACC_ROWS = B_V_DIM + BF16_SUBLANES
FM_ROWS = 2 * A_WIDTH + B_Q_LORA
TM_COLS = A_WIDTH + B_KV_LORA + LANES
VMEM_LIMIT = 60 * 1024 * 1024

_BF = jnp.bfloat16
_F32 = jnp.float32


def _dot(a, b):
    return jnp.dot(a, b, preferred_element_type=_F32)


def _rms(x, g):
    ms = jnp.mean(x * x, axis=-1, keepdims=True)
    return x * lax.rsqrt(ms + EPS) * g


def _col_rms(x_t):
    return x_t * lax.rsqrt(jnp.mean(x_t * x_t, axis=0, keepdims=True) + EPS)


def _swiglu(h, wg_ref, wu_ref, wd_ref, act_ref):
    for c in range(D_FF // FF_CHUNK):
        sl = slice(c * FF_CHUNK, (c + 1) * FF_CHUNK)
        g = _dot(h, wg_ref[:, sl])
        u = _dot(h, wu_ref[:, sl])
        act_ref[:, sl] = (jax.nn.silu(g) * u).astype(_BF)
    return _dot(act_ref[...], wd_ref[...])


def _seg_sumsq(x, seg_ref):
    sq = x * x
    hi = sq.astype(_BF)
    lo = (sq - hi.astype(_F32)).astype(_BF)
    seg = seg_ref[...]
    parts = []
    for j in range(x.shape[1] // MXU_DIM):
        sl = slice(j * MXU_DIM, (j + 1) * MXU_DIM)
        parts.append(_dot(hi[:, sl], seg) + _dot(lo[:, sl], seg))
    return jnp.concatenate(parts, axis=1)


def _pre_kernel(x_ref, g1_ref, wg_ref, wu_ref, wd_ref, gmix_ref, wfm_ref, wtm_ref,
                gaq_ref, gak_ref, seg64_ref,
                wqt_ref, gqtab_ref, qpad_ref, cost_ref, sint_ref,
                gckv_ref, wkn_ref, wvt_ref, gkn_ref, seg128_ref,
                gkr_ref, kone_ref, ck_ref, sk_ref,
                x1_ref, qat_ref, ka_ref, vat_ref, qbt_ref, kb_ref, vbt_ref,
                act_ref):
    x = x_ref[...]
    h = _rms(x, g1_ref[...]).astype(_BF)
    x1 = x + 0.5 * _swiglu(h, wg_ref, wu_ref, wd_ref, act_ref)
    x1_ref[...] = x1
    h2f = _rms(x1, gmix_ref[...])
    h2 = h2f.astype(_BF)
    tm = h2.shape[0]

    fm = _dot(wfm_ref[...], h2f.T.astype(_BF))
    gaq = gaq_ref[...]
    for hd in range(A_HEADS):
        rows = slice(hd * A_HEAD_DIM, (hd + 1) * A_HEAD_DIM)
        qat_ref[0, rows, :] = (_col_rms(fm[rows]) * gaq).astype(_BF)
        for g in range(tm // A_Q_TILE):
            vat_ref[0, hd, g] = fm[A_WIDTH + hd * A_HEAD_DIM:A_WIDTH + (hd + 1) * A_HEAD_DIM,
                                   g * A_Q_TILE:(g + 1) * A_Q_TILE].astype(_BF)

    q_t = _dot(wqt_ref[...], _col_rms(fm[2 * A_WIDTH:FM_ROWS]).astype(_BF))
    cos_t = cost_ref[...]
    sin_t = sint_ref[...]
    gq = gqtab_ref[...]
    half = B_ROPE // 2
    for hd in range(B_HEADS):
        r0 = hd * HEAD_PAD
        nope = _col_rms(q_t[r0:r0 + B_NOPE]) * gq[0:B_NOPE]
        rope = _col_rms(q_t[r0 + B_NOPE:r0 + B_QK_DIM]) * gq[B_NOPE:B_QK_DIM]
        r1 = rope[0:half]
        r2 = rope[half:B_ROPE]
        blk = jnp.concatenate(
            [nope, r1 * cos_t - r2 * sin_t, r1 * sin_t + r2 * cos_t, qpad_ref[...]], axis=0)
        qbt_ref[0, hd] = blk.astype(_BF)

    tmj = _dot(h2, wtm_ref[...])
    ka = tmj[:, 0:A_WIDTH]
    ka_ref[0] = (ka * lax.rsqrt(_seg_sumsq(ka, seg64_ref) * (1.0 / A_HEAD_DIM) + EPS)
                 * gak_ref[...]).astype(_BF)

    ckvn = _rms(tmj[:, A_WIDTH:A_WIDTH + B_KV_LORA], gckv_ref[...])
    v_t = _dot(wvt_ref[...], ckvn.T.astype(_BF))
    for hd in range(B_HEADS):
        vbt_ref[0, hd, 0] = v_t[hd * B_V_DIM:(hd + 1) * B_V_DIM].astype(_BF)
    kn = _dot(ckvn.astype(_BF), wkn_ref[...])
    kn = kn * lax.rsqrt(_seg_sumsq(kn, seg128_ref) * (1.0 / B_NOPE) + EPS) * gkn_ref[...]
    kr = tmj[:, A_WIDTH + B_KV_LORA:TM_COLS]
    y = kr * lax.rsqrt(jnp.sum(kr * kr, axis=-1, keepdims=True) * (1.0 / B_ROPE) + EPS) * gkr_ref[...]
    lane = lax.broadcasted_iota(jnp.int32, y.shape, 1)
    swapped = jnp.where(lane < B_NOPE + half,
                        pltpu.roll(y, LANES - half, axis=1), pltpu.roll(y, half, axis=1))
    kpe = y * ck_ref[...] + swapped * sk_ref[...] + kone_ref[...]
    for hd in range(B_HEADS):
        kb_ref[0, hd] = (kn[:, hd * HEAD_PAD:(hd + 1) * HEAD_PAD] + kpe).astype(_BF)


def _bias_kernel(tab_ref, o_ref):
    u = lax.broadcasted_iota(jnp.int32, (LANES, A_Q_TILE), 0)
    qc = lax.broadcasted_iota(jnp.int32, (LANES, A_Q_TILE), 1) // CHUNK
    for tile in range(A_BIAS_TILES):
        a = A_TAB_ZERO + A_LEFT - LANES * tile
        w = jnp.concatenate([tab_ref[0, :, a:a + A_Q_TILE], tab_ref[0, :, a - A_Q_TILE:a]], axis=1)
        rolled = pltpu.roll(jnp.broadcast_to(w, (LANES, 2 * A_Q_TILE)), 0, 1,
                            stride=1, stride_axis=0)
        kc = (LANES * tile + u) // CHUNK - A_LEFT_CHUNKS
        valid = (kc <= qc) & (kc >= qc - A_LEFT_CHUNKS)
        o_ref[0, tile] = jnp.where(valid, rolled[:, :A_Q_TILE], NEG_INF)


def _rel_bias_tiles(rel_bias, qk_bound):
    pad_lo = A_TAB_ZERO - A_MAX_REL
    pad_hi = A_TAB_LEN - pad_lo - (2 * A_MAX_REL + 1)
    rb = rel_bias.astype(_F32) * LOG2E
    rb = rb - (qk_bound + jnp.max(rb, axis=1, keepdims=True))
    table = jnp.concatenate([jnp.broadcast_to(rb[:, :1], (A_HEADS, pad_lo)), rb,
                             jnp.broadcast_to(rb[:, -1:], (A_HEADS, pad_hi))], axis=1)
    return pl.pallas_call(
        _bias_kernel,
        grid=(A_HEADS,),
        in_specs=[pl.BlockSpec((1, 1, A_TAB_LEN), lambda h: (h, 0, 0))],
        out_specs=pl.BlockSpec((1, A_BIAS_TILES, LANES, A_Q_TILE), lambda h: (h, 0, 0, 0)),
        out_shape=jax.ShapeDtypeStruct((A_HEADS, A_BIAS_TILES, LANES, A_Q_TILE), _F32),
        name="rel_bias",
    )(table.reshape(A_HEADS, 1, A_TAB_LEN))


def _attn_a_kernel(qt_ref, k_ref, vt_ref, bias_ref, o_ref):
    i = pl.program_id(1)
    left_tiles = A_LEFT // A_Q_TILE
    g0 = jnp.maximum(i - left_tiles, 0)
    start = pl.multiple_of(g0 * A_Q_TILE, A_Q_TILE)
    u0 = (left_tiles - jnp.minimum(i, left_tiles)) * (A_Q_TILE // LANES)
    zeros = jnp.zeros((A_HEAD_DIM, A_Q_TILE), _BF)
    ones_rows = jnp.ones((ACC_ROWS - A_HEAD_DIM, A_WIN), _BF)

    def scores(pair):
        q2 = qt_ref[0, pair * LANES:(pair + 1) * LANES, :]
        q_even = jnp.concatenate([q2[:A_HEAD_DIM], zeros], axis=0)
        q_odd = jnp.concatenate([zeros, q2[A_HEAD_DIM:]], axis=0)
        k = k_ref[0, pl.ds(start, A_WIN), pair * LANES:(pair + 1) * LANES]
        return _dot(k, jnp.concatenate([q_even, q_odd], axis=1))

    def attend(subtract_max):
        n_pairs = A_HEADS // 2
        pending = [scores(0)]
        outs, denoms = [], []
        for pair in range(n_pairs):
            s = pending.pop(0)
            if pair + 1 < n_pairs:
                pending.append(scores(pair + 1))
            bias = jnp.concatenate(
                [jnp.concatenate([bias_ref[2 * pair + e, u0 + j] for j in range(A_WIN // LANES)],
                                 axis=0) for e in range(2)], axis=1)
            s = s + bias
            if subtract_max:
                s = s - jnp.max(s, axis=0, keepdims=True)
            p = jnp.exp2(s).astype(_BF)
            for e in range(2):
                hd = 2 * pair + e
                v = jnp.concatenate([vt_ref[0, hd, g0 + g] for g in range(A_WIN // A_Q_TILE)],
                                    axis=1)
                o = _dot(jnp.concatenate([v, ones_rows], axis=0),
                         p[:, e * A_Q_TILE:(e + 1) * A_Q_TILE])
                denoms.append(o[A_HEAD_DIM:A_HEAD_DIM + 1])
                outs.append(o[:A_HEAD_DIM] / denoms[-1])
        o_ref[0] = jnp.concatenate(outs, axis=0).T.astype(_BF)
        return jnp.min(jnp.concatenate(denoms, axis=0))

    smallest = attend(subtract_max=False)

    @pl.when(jnp.logical_not(smallest > SAFE_DENOM))
    def _():
        attend(subtract_max=True)


def _attn_b_kernel(qt_ref, k_ref, vt_ref, o_ref, m_ref, acc_ref):
    i = pl.program_id(1)
    t = TOK_TILE
    ones_rows = jnp.ones((ACC_ROWS - B_V_DIM, t), _BF)

    def v_aug(kt, hd):
        return jnp.concatenate([vt_ref[0, hd, kt], ones_rows], axis=0)

    def acc_rows(hd):
        return slice(hd * ACC_ROWS, (hd + 1) * ACC_ROWS)

    def diag_mask(s):
        krow = lax.broadcasted_iota(jnp.int32, (t, t), 0) // CHUNK
        qcol = lax.broadcasted_iota(jnp.int32, (t, t), 1) // CHUNK
        return jnp.where(krow <= qcol, s, NEG_INF)

    def for_heads(kt, consume):
        def scores(hd):
            k = k_ref[0, hd, pl.ds(pl.multiple_of(kt * t, t), t), :]
            return _dot(k, qt_ref[0, hd])

        pending = [scores(hd) for hd in range(QK_AHEAD)]
        for hd in range(B_HEADS):
            s = pending.pop(0)
            if hd + QK_AHEAD < B_HEADS:
                pending.append(scores(hd + QK_AHEAD))
            consume(hd, s)

    def bounded_diag(hd, s):
        acc_ref[acc_rows(hd), :] = _dot(v_aug(i, hd), jnp.exp2(diag_mask(s)).astype(_BF))

    for_heads(i, bounded_diag)
    denoms = jnp.concatenate(
        [acc_ref[hd * ACC_ROWS + B_V_DIM:hd * ACC_ROWS + B_V_DIM + 1, :] for hd in range(B_HEADS)],
        axis=0)
    safe = jnp.min(denoms) > SAFE_DENOM

    @pl.when(safe)
    def _():
        def bounded_full(kt, carry):
            def consume(hd, s):
                acc_ref[acc_rows(hd), :] += _dot(v_aug(kt, hd), jnp.exp2(s).astype(_BF))

            for_heads(kt, consume)
            return carry

        lax.fori_loop(0, i, bounded_full, 0)

    @pl.when(jnp.logical_not(safe))
    def _():
        m_ref[...] = jnp.full(m_ref.shape, NEG_INF, _F32)
        acc_ref[...] = jnp.zeros(acc_ref.shape, _F32)

        def online_step(kt, masked):
            def consume(hd, s):
                if masked:
                    s = diag_mask(s)
                m = m_ref[hd:hd + 1, :]
                m_new = jnp.maximum(m, jnp.max(s, axis=0, keepdims=True))
                alpha = jnp.exp2(m - m_new)
                p = jnp.exp2(s - m_new).astype(_BF)
                m_ref[hd:hd + 1, :] = m_new
                acc_ref[acc_rows(hd), :] = alpha * acc_ref[acc_rows(hd), :] + _dot(v_aug(kt, hd), p)

            for_heads(kt, consume)

        def online_full(kt, carry):
            online_step(kt, masked=False)
            return carry

        lax.fori_loop(0, i, online_full, 0)
        online_step(i, masked=True)

    outs = []
    for hd in range(B_HEADS):
        r0 = hd * ACC_ROWS
        outs.append(acc_ref[r0:r0 + B_V_DIM, :] / acc_ref[r0 + B_V_DIM:r0 + B_V_DIM + 1, :])
    o_ref[0] = jnp.concatenate(outs, axis=0).T.astype(_BF)


def _post_kernel(x1_ref, oa_ref, ob_ref, wout_ref, g2_ref, wg_ref, wu_ref, wd_ref,
                 gf_ref, y_ref, act_ref):
    x2 = (x1_ref[...] + _dot(oa_ref[...], wout_ref[0:A_WIDTH, :])
          + _dot(ob_ref[...], wout_ref[A_WIDTH:A_WIDTH + B_WIDTH, :]))
    h = _rms(x2, g2_ref[...]).astype(_BF)
    x3 = x2 + 0.5 * _swiglu(h, wg_ref, wu_ref, wd_ref, act_ref)
    y_ref[...] = _rms(x3, gf_ref[...])


def _const_spec(shape):
    n = len(shape)
    return pl.BlockSpec(shape, lambda *_: (0,) * n, pipeline_mode=pl.Buffered(1))


def kernel(x, ffn1_norm, ffn1_w_gate, ffn1_w_up, ffn1_w_down, mix_norm, w_in, a_q_norm, a_k_norm, a_rel_bias, b_q_lat_norm, b_w_uq, b_kv_lat_norm, b_w_ukv, b_q_nope_norm, b_q_rope_norm, b_k_nope_norm, b_k_rope_norm, w_out, ffn2_norm, ffn2_w_gate, ffn2_w_up, ffn2_w_down, final_norm):
    b, s, d = x.shape
    assert d == D_MODEL and s % TOK_TILE == 0 and s >= A_WIN and ffn1_norm.shape[0] == 1
    tm = TOK_TILE
    nt = s // tm
    half = B_ROPE // 2

    def row(v):
        return v.astype(_F32)[None, :]

    w_in_l = w_in[0]
    o_va, o_cq = 2 * A_WIDTH, 3 * A_WIDTH
    o_ckv = o_cq + B_Q_LORA
    o_kr = o_ckv + B_KV_LORA
    w_fm = jnp.concatenate([w_in_l[:, 0:A_WIDTH], w_in_l[:, o_va:o_cq], w_in_l[:, o_cq:o_ckv]],
                           axis=1).T.astype(_BF)
    kr_cols = jnp.zeros((d, LANES), _F32).at[:, B_NOPE:B_QK_DIM].set(w_in_l[:, o_kr:o_kr + B_ROPE])
    w_tm = jnp.concatenate([w_in_l[:, A_WIDTH:o_va], w_in_l[:, o_ckv:o_kr], kr_cols],
                           axis=1).astype(_BF)

    gaq_tab = jnp.broadcast_to((a_q_norm[0] * (A_HEAD_DIM ** -0.5 * LOG2E))[:, None],
                               (A_HEAD_DIM, tm)).astype(_F32)
    gak = row(jnp.tile(a_k_norm[0], A_HEADS))
    blk = jnp.arange(MXU_DIM)
    seg64 = (blk[:, None] // A_HEAD_DIM == blk[None, :] // A_HEAD_DIM).astype(_BF)
    seg128 = (blk[:, None] // HEAD_PAD == blk[None, :] // HEAD_PAD).astype(_BF)

    w_uq = (b_w_uq[0] * b_q_lat_norm[0][:, None]).reshape(B_Q_LORA, B_HEADS, B_QK_DIM)
    w_uq = jnp.pad(w_uq, ((0, 0), (0, 0), (0, HEAD_PAD - B_QK_DIM)))
    wq_t = w_uq.reshape(B_Q_LORA, B_HEADS * HEAD_PAD).T.astype(_BF)
    scale_b = (B_QK_DIM ** -0.5) * LOG2E
    gq_col = jnp.concatenate([b_q_nope_norm[0] * scale_b, b_q_rope_norm[0] * scale_b,
                              jnp.zeros((HEAD_PAD - B_QK_DIM,), _F32)])
    gq_tab = jnp.broadcast_to(gq_col[:, None], (HEAD_PAD, tm)).astype(_F32)
    q_norm2 = B_NOPE * jnp.max(b_q_nope_norm[0] ** 2) + B_ROPE * jnp.max(b_q_rope_norm[0] ** 2)
    k_norm2 = B_NOPE * jnp.max(b_k_nope_norm[0] ** 2) + B_ROPE * jnp.max(b_k_rope_norm[0] ** 2)
    bound_b = (scale_b * jnp.sqrt(q_norm2 * k_norm2) * BOUND_SLACK).astype(_BF).astype(_F32)
    qpad_tab = jnp.zeros((HEAD_PAD - B_QK_DIM, tm), _F32).at[0, :].set(-bound_b)
    kone = jnp.zeros((1, LANES), _F32).at[0, B_QK_DIM].set(1.0)

    inv = 1.0 / (ROPE_THETA ** (jnp.arange(0, B_ROPE, 2, dtype=_F32) / B_ROPE))
    ang = jnp.arange(s, dtype=_F32)[:, None] * inv[None, :]
    cos, sin = jnp.cos(ang), jnp.sin(ang)
    cos_t, sin_t = cos.T, sin.T
    zpad = jnp.zeros((s, B_NOPE), _F32)
    zend = jnp.zeros((s, LANES - B_QK_DIM), _F32)
    ck = jnp.concatenate([zpad, cos, cos, zend], axis=1)
    sk = jnp.concatenate([zpad, -sin, sin, zend], axis=1)

    w_ukv = b_w_ukv[0].reshape(B_KV_LORA, B_HEADS, B_NOPE + B_V_DIM)
    wkn = jnp.pad(w_ukv[..., :B_NOPE], ((0, 0), (0, 0), (0, HEAD_PAD - B_NOPE)))
    wkn = wkn.reshape(B_KV_LORA, B_HEADS * HEAD_PAD).astype(_BF)
    wv_t = w_ukv[..., B_NOPE:].reshape(B_KV_LORA, B_WIDTH).T.astype(_BF)
    gkn = row(jnp.tile(jnp.concatenate([b_k_nope_norm[0], jnp.zeros((HEAD_PAD - B_NOPE,), _F32)]),
                       B_HEADS))
    gkr = row(jnp.concatenate([jnp.zeros((B_NOPE,), _F32), b_k_rope_norm[0],
                               jnp.zeros((LANES - B_QK_DIM,), _F32)]))

    bound_a = ((A_HEAD_DIM ** -0.5 * LOG2E) * A_HEAD_DIM * jnp.max(jnp.abs(a_q_norm[0]))
               * jnp.max(jnp.abs(a_k_norm[0])) * BOUND_SLACK)
    bias_tiles = _rel_bias_tiles(a_rel_bias[0], bound_a)

    tok = lambda bi, j: (bi, j, 0)
    pre_in_specs = [
        pl.BlockSpec((None, tm, d), tok),
        _const_spec((1, d)), _const_spec((d, D_FF)), _const_spec((d, D_FF)), _const_spec((D_FF, d)),
        _const_spec((1, d)), _const_spec((FM_ROWS, d)), _const_spec((d, TM_COLS)),
        _const_spec((A_HEAD_DIM, tm)), _const_spec((1, A_WIDTH)), _const_spec((MXU_DIM, MXU_DIM)),
        _const_spec((B_HEADS * HEAD_PAD, B_Q_LORA)), _const_spec((HEAD_PAD, tm)),
        _const_spec((HEAD_PAD - B_QK_DIM, tm)),
        pl.BlockSpec((half, tm), lambda bi, j: (0, j)), pl.BlockSpec((half, tm), lambda bi, j: (0, j)),
        _const_spec((1, B_KV_LORA)), _const_spec((B_KV_LORA, B_HEADS * HEAD_PAD)),
        _const_spec((B_WIDTH, B_KV_LORA)), _const_spec((1, B_HEADS * HEAD_PAD)),
        _const_spec((MXU_DIM, MXU_DIM)),
        _const_spec((1, LANES)), _const_spec((1, LANES)),
        pl.BlockSpec((tm, LANES), lambda bi, j: (j, 0)), pl.BlockSpec((tm, LANES), lambda bi, j: (j, 0)),
    ]
    pre_out_shapes = [
        jax.ShapeDtypeStruct((b, s, d), _F32),
        jax.ShapeDtypeStruct((b, A_WIDTH, s), _BF),
        jax.ShapeDtypeStruct((b, s, A_WIDTH), _BF),
        jax.ShapeDtypeStruct((b, A_HEADS, s // A_Q_TILE, A_HEAD_DIM, A_Q_TILE), _BF),
        jax.ShapeDtypeStruct((b, B_HEADS, HEAD_PAD, s), _BF),
        jax.ShapeDtypeStruct((b, B_HEADS, s, HEAD_PAD), _BF),
        jax.ShapeDtypeStruct((b, B_HEADS, nt, B_V_DIM, tm), _BF),
    ]
    pre_out_specs = [
        pl.BlockSpec((None, tm, d), tok),
        pl.BlockSpec((1, A_WIDTH, tm), lambda bi, j: (bi, 0, j)),
        pl.BlockSpec((1, tm, A_WIDTH), tok),
        pl.BlockSpec((1, A_HEADS, tm // A_Q_TILE, A_HEAD_DIM, A_Q_TILE),
                     lambda bi, j: (bi, 0, j, 0, 0)),
        pl.BlockSpec((1, B_HEADS, HEAD_PAD, tm), lambda bi, j: (bi, 0, 0, j)),
        pl.BlockSpec((1, B_HEADS, tm, HEAD_PAD), lambda bi, j: (bi, 0, j, 0)),
        pl.BlockSpec((1, B_HEADS, 1, B_V_DIM, tm), lambda bi, j: (bi, 0, j, 0, 0)),
    ]
    x1, qat, ka, vat, qbt, kb, vbt = pl.pallas_call(
        _pre_kernel,
        grid=(b, nt),
        in_specs=pre_in_specs,
        out_specs=pre_out_specs,
        out_shape=pre_out_shapes,
        scratch_shapes=[pltpu.VMEM((tm, D_FF), _BF)],
        compiler_params=pltpu.CompilerParams(
            dimension_semantics=("arbitrary", "arbitrary"), vmem_limit_bytes=VMEM_LIMIT),
        name="pre",
    )(x, row(ffn1_norm[0]), ffn1_w_gate[0].astype(_BF), ffn1_w_up[0].astype(_BF),
      ffn1_w_down[0].astype(_BF), row(mix_norm[0]), w_fm, w_tm,
      gaq_tab, gak, seg64,
      wq_t, gq_tab, qpad_tab, cos_t, sin_t,
      row(b_kv_lat_norm[0]), wkn, wv_t, gkn, seg128,
      gkr, kone, ck, sk)

    na = s // A_Q_TILE
    out_a = pl.pallas_call(
        _attn_a_kernel,
        grid=(b, na),
        in_specs=[
            pl.BlockSpec((1, A_WIDTH, A_Q_TILE), lambda bi, j: (bi, 0, j)),
            pl.BlockSpec((1, s, A_WIDTH), lambda bi, j: (bi, 0, 0), pipeline_mode=pl.Buffered(1)),
            pl.BlockSpec((1, A_HEADS, na, A_HEAD_DIM, A_Q_TILE), lambda bi, j: (bi, 0, 0, 0, 0),
                         pipeline_mode=pl.Buffered(1)),
            _const_spec((A_HEADS, A_BIAS_TILES, LANES, A_Q_TILE)),
        ],
        out_specs=pl.BlockSpec((1, A_Q_TILE, A_WIDTH), tok),
        out_shape=jax.ShapeDtypeStruct((b, s, A_WIDTH), _BF),
        compiler_params=pltpu.CompilerParams(
            dimension_semantics=("arbitrary", "arbitrary"), vmem_limit_bytes=VMEM_LIMIT),
        name="attn_a",
    )(qat, ka, vat, bias_tiles)

    out_b = pl.pallas_call(
        _attn_b_kernel,
        grid=(b, nt),
        in_specs=[
            pl.BlockSpec((1, B_HEADS, HEAD_PAD, tm), lambda bi, j: (bi, 0, 0, j)),
            pl.BlockSpec((1, B_HEADS, s, HEAD_PAD), lambda bi, j: (bi, 0, 0, 0),
                         pipeline_mode=pl.Buffered(1)),
            pl.BlockSpec((1, B_HEADS, nt, B_V_DIM, tm), lambda bi, j: (bi, 0, 0, 0, 0),
                         pipeline_mode=pl.Buffered(1)),
        ],
        out_specs=pl.BlockSpec((1, tm, B_WIDTH), tok),
        out_shape=jax.ShapeDtypeStruct((b, s, B_WIDTH), _BF),
        scratch_shapes=[pltpu.VMEM((B_HEADS, tm), _F32),
                        pltpu.VMEM((B_HEADS * ACC_ROWS, tm), _F32)],
        compiler_params=pltpu.CompilerParams(
            dimension_semantics=("arbitrary", "arbitrary"), vmem_limit_bytes=VMEM_LIMIT),
        name="attn_b",
    )(qbt, kb, vbt)

    n = b * s
    flat = lambda j: (j, 0)
    y = pl.pallas_call(
        _post_kernel,
        grid=(n // tm,),
        in_specs=[
            pl.BlockSpec((tm, d), flat),
            pl.BlockSpec((tm, A_WIDTH), flat),
            pl.BlockSpec((tm, B_WIDTH), flat),
            _const_spec((A_WIDTH + B_WIDTH, d)),
            _const_spec((1, d)), _const_spec((d, D_FF)), _const_spec((d, D_FF)), _const_spec((D_FF, d)),
            _const_spec((1, d)),
        ],
        out_specs=pl.BlockSpec((tm, d), flat),
        out_shape=jax.ShapeDtypeStruct((n, d), _F32),
        scratch_shapes=[pltpu.VMEM((tm, D_FF), _BF)],
        compiler_params=pltpu.CompilerParams(
            dimension_semantics=("arbitrary",), vmem_limit_bytes=VMEM_LIMIT),
        name="post",
    )(x1.reshape(n, d), out_a.reshape(n, A_WIDTH), out_b.reshape(n, B_WIDTH),
      w_out[0].astype(_BF), row(ffn2_norm[0]), ffn2_w_gate[0].astype(_BF),
      ffn2_w_up[0].astype(_BF), ffn2_w_down[0].astype(_BF), row(final_norm[0]))
    return y.reshape(b, s, d)
```

```python
import jax
import jax.numpy as jnp
from jax import lax
from jax.experimental import pallas as pl
from jax.experimental.pallas import tpu as pltpu

D_MODEL = 1024
D_FF = 2816
CHUNK = 64
A_HEADS = 8
A_HEAD_DIM = 64
A_LEFT_CHUNKS = 8
A_MAX_REL = 128
A_WIDTH = A_HEADS * A_HEAD_DIM
B_HEADS = 8
B_Q_LORA = 256
B_KV_LORA = 128
B_NOPE = 64
B_ROPE = 32
B_V_DIM = 64
B_QK_DIM = B_NOPE + B_ROPE
B_WIDTH = B_HEADS * B_V_DIM
ROPE_THETA = 10000.0
EPS = 1e-6
NEG_INF = -1e30
LOG2E = 1.4426950408889634
BOUND_SLACK = 1.0 + 2.0 ** -5
SAFE_DENOM = 2.0 ** -80

LANES = 128
BF16_SUBLANES = 16
MXU_DIM = 256
TOK_TILE = 512
FF_CHUNK = 256
QK_AHEAD = 2

A_Q_TILE = 256
A_LEFT = A_LEFT_CHUNKS * CHUNK
A_WIN = A_Q_TILE + A_LEFT
A_KEY_SPLITS = ((0, 384), (384, 768))
A_BIAS_ROWS = A_WIN + A_LEFT
A_BIAS_TILES = A_BIAS_ROWS // LANES
A_TAB_ZERO = A_BIAS_ROWS - A_LEFT + LANES
A_TAB_LEN = A_TAB_ZERO + A_LEFT + A_Q_TILE

HEAD_PAD = 128
ACC_ROWS = B_V_DIM + BF16_SUBLANES
FM_ROWS = 3 * A_WIDTH + B_Q_LORA
TM_COLS = B_KV_LORA + LANES
VMEM_LIMIT = 60 * 1024 * 1024

_BF = jnp.bfloat16
_F32 = jnp.float32


def _dot(a, b):
    return jnp.dot(a, b, preferred_element_type=_F32)


def _rms(x, g):
    ms = jnp.mean(x * x, axis=-1, keepdims=True)
    return x * lax.rsqrt(ms + EPS) * g


def _col_rms(x_t):
    return x_t * lax.rsqrt(jnp.mean(x_t * x_t, axis=0, keepdims=True) + EPS)


def _swiglu(h, wg_ref, wu_ref, wd_ref, act_ref):
    for c in range(D_FF // FF_CHUNK):
        sl = slice(c * FF_CHUNK, (c + 1) * FF_CHUNK)
        g = _dot(h, wg_ref[:, sl])
        u = _dot(h, wu_ref[:, sl])
        act_ref[:, sl] = (jax.nn.silu(g) * u).astype(_BF)
    return _dot(act_ref[...], wd_ref[...])


def _pre_kernel(x_ref, g1_ref, wg_ref, wu_ref, wd_ref, gmix_ref, wfm_ref, wtm_ref,
                gaq_ref, gak_ref,
                wqt_ref, gqtab_ref, qpad_ref, cost_ref, sint_ref,
                gckv_ref, wvk_ref, gkn_ref,
                gkr_ref, kone_ref, ck_ref, sk_ref,
                x1_ref, qat_ref, ka_ref, vat_ref, qbt_ref, kb_ref, vbt_ref,
                act_ref):
    x = x_ref[...]
    h = _rms(x, g1_ref[...]).astype(_BF)
    x1 = x + 0.5 * _swiglu(h, wg_ref, wu_ref, wd_ref, act_ref)
    x1_ref[...] = x1
    h2f = _rms(x1, gmix_ref[...])
    h2 = h2f.astype(_BF)
    tm = h2.shape[0]

    fm = _dot(wfm_ref[...], h2f.T.astype(_BF))
    gaq = gaq_ref[...]
    gak = gak_ref[...]
    ka_t = []
    for hd in range(A_HEADS):
        rows = slice(hd * A_HEAD_DIM, (hd + 1) * A_HEAD_DIM)
        qat_ref[0, rows, :] = (_col_rms(fm[rows]) * gaq).astype(_BF)
        for g in range(tm // A_Q_TILE):
            vat_ref[0, hd, g] = fm[A_WIDTH + hd * A_HEAD_DIM:A_WIDTH + (hd + 1) * A_HEAD_DIM,
                                   g * A_Q_TILE:(g + 1) * A_Q_TILE].astype(_BF)
        ka_t.append(_col_rms(fm[2 * A_WIDTH + hd * A_HEAD_DIM:
                                2 * A_WIDTH + (hd + 1) * A_HEAD_DIM]) * gak)
    ka_ref[0] = jnp.concatenate(ka_t, axis=0).T.astype(_BF)

    q_t = _dot(wqt_ref[...], _col_rms(fm[3 * A_WIDTH:FM_ROWS]).astype(_BF))
    cos_t = cost_ref[...]
    sin_t = sint_ref[...]
    gq = gqtab_ref[...]
    half = B_ROPE // 2
    for hd in range(B_HEADS):
        r0 = hd * HEAD_PAD
        nope = _col_rms(q_t[r0:r0 + B_NOPE]) * gq[0:B_NOPE]
        rope = _col_rms(q_t[r0 + B_NOPE:r0 + B_QK_DIM]) * gq[B_NOPE:B_QK_DIM]
        r1 = rope[0:half]
        r2 = rope[half:B_ROPE]
        blk = jnp.concatenate(
            [nope, r1 * cos_t - r2 * sin_t, r1 * sin_t + r2 * cos_t, qpad_ref[...]], axis=0)
        qbt_ref[0, hd] = blk.astype(_BF)

    tmj = _dot(h2, wtm_ref[...])

    ckvn_t = _rms(tmj[:, 0:B_KV_LORA], gckv_ref[...]).T.astype(_BF)
    vk_t = _dot(wvk_ref[...], ckvn_t)
    gkn = gkn_ref[...]
    kn_t = []
    for hd in range(B_HEADS):
        vbt_ref[0, hd, 0] = vk_t[hd * B_V_DIM:(hd + 1) * B_V_DIM].astype(_BF)
        kn_t.append(_col_rms(vk_t[B_WIDTH + hd * B_NOPE:B_WIDTH + (hd + 1) * B_NOPE]) * gkn)
    kn = jnp.concatenate(kn_t, axis=0).T
    kr = tmj[:, B_KV_LORA:TM_COLS]
    y = kr * lax.rsqrt(jnp.sum(kr * kr, axis=-1, keepdims=True) * (1.0 / B_ROPE) + EPS) * gkr_ref[...]
    lane = lax.broadcasted_iota(jnp.int32, y.shape, 1)
    swapped = jnp.where(lane < B_NOPE + half,
                        pltpu.roll(y, LANES - half, axis=1), pltpu.roll(y, half, axis=1))
    kpe = y * ck_ref[...] + swapped * sk_ref[...] + kone_ref[...]
    for hd in range(B_HEADS):
        pair_tile = kn[:, (hd // 2) * LANES:(hd // 2 + 1) * LANES]
        if hd % 2 == 1:
            pair_tile = pltpu.roll(pair_tile, B_NOPE, axis=1)
        kb_ref[0, hd] = jnp.where(lane < B_NOPE, pair_tile, kpe).astype(_BF)


def _bias_kernel(tab_ref, o_ref):
    u = lax.broadcasted_iota(jnp.int32, (LANES, A_Q_TILE), 0)
    qc = lax.broadcasted_iota(jnp.int32, (LANES, A_Q_TILE), 1) // CHUNK
    for tile in range(A_BIAS_TILES):
        a = A_TAB_ZERO + A_LEFT - LANES * tile
        w = jnp.concatenate([tab_ref[0, :, a:a + A_Q_TILE], tab_ref[0, :, a - A_Q_TILE:a]], axis=1)
        rolled = pltpu.roll(jnp.broadcast_to(w, (LANES, 2 * A_Q_TILE)), 0, 1,
                            stride=1, stride_axis=0)
        kc = (LANES * tile + u) // CHUNK - A_LEFT_CHUNKS
        valid = (kc <= qc) & (kc >= qc - A_LEFT_CHUNKS)
        o_ref[0, tile] = jnp.where(valid, rolled[:, :A_Q_TILE], NEG_INF)


def _rel_bias_tiles(rel_bias, qk_bound):
    pad_lo = A_TAB_ZERO - A_MAX_REL
    pad_hi = A_TAB_LEN - pad_lo - (2 * A_MAX_REL + 1)
    rb = rel_bias.astype(_F32) * LOG2E
    rb = rb - (qk_bound + jnp.max(rb, axis=1, keepdims=True))
    table = jnp.concatenate([jnp.broadcast_to(rb[:, :1], (A_HEADS, pad_lo)), rb,
                             jnp.broadcast_to(rb[:, -1:], (A_HEADS, pad_hi))], axis=1)
    return pl.pallas_call(
        _bias_kernel,
        grid=(A_HEADS,),
        in_specs=[pl.BlockSpec((1, 1, A_TAB_LEN), lambda h: (h, 0, 0))],
        out_specs=pl.BlockSpec((1, A_BIAS_TILES, LANES, A_Q_TILE), lambda h: (h, 0, 0, 0)),
        out_shape=jax.ShapeDtypeStruct((A_HEADS, A_BIAS_TILES, LANES, A_Q_TILE), _F32),
        name="rel_bias",
    )(table.reshape(A_HEADS, 1, A_TAB_LEN))


def _attn_a_kernel(qt_ref, k_ref, vt_ref, bias_ref, o_ref):
    i = pl.program_id(1)
    left_tiles = A_LEFT // A_Q_TILE
    g0 = jnp.maximum(i - left_tiles, 0)
    start = pl.multiple_of(g0 * A_Q_TILE, A_Q_TILE)
    u0 = (left_tiles - jnp.minimum(i, left_tiles)) * (A_Q_TILE // LANES)
    zeros = jnp.zeros((A_HEAD_DIM, A_Q_TILE), _BF)
    ones_rows = jnp.ones((ACC_ROWS - A_HEAD_DIM, A_WIN), _BF)

    def scores(pair):
        q2 = qt_ref[0, pair * LANES:(pair + 1) * LANES, :]
        q_even = jnp.concatenate([q2[:A_HEAD_DIM], zeros], axis=0)
        q_odd = jnp.concatenate([zeros, q2[A_HEAD_DIM:]], axis=0)
        qq = jnp.concatenate([q_even, q_odd], axis=1)
        parts = []
        for r0, r1 in A_KEY_SPLITS:
            k = k_ref[0, pl.ds(pl.multiple_of(start + r0, LANES), r1 - r0),
                      pair * LANES:(pair + 1) * LANES]
            parts.append(_dot(k, qq))
        return jnp.concatenate(parts, axis=0)

    def attend(subtract_max):
        n_pairs = A_HEADS // 2
        pending = [scores(0)]
        outs, denoms = [], []
        for pair in range(n_pairs):
            s = pending.pop(0)
            if pair + 1 < n_pairs:
                pending.append(scores(pair + 1))
            bias = jnp.concatenate(
                [jnp.concatenate([bias_ref[2 * pair + e, u0 + j] for j in range(A_WIN // LANES)],
                                 axis=0) for e in range(2)], axis=1)
            s = s + bias
            if subtract_max:
                s = s - jnp.max(s, axis=0, keepdims=True)
            p = jnp.exp2(s).astype(_BF)
            for e in range(2):
                hd = 2 * pair + e
                v = jnp.concatenate([vt_ref[0, hd, g0 + g] for g in range(A_WIN // A_Q_TILE)],
                                    axis=1)
                o = _dot(jnp.concatenate([v, ones_rows], axis=0),
                         p[:, e * A_Q_TILE:(e + 1) * A_Q_TILE])
                denoms.append(o[A_HEAD_DIM:A_HEAD_DIM + 1])
                outs.append(o[:A_HEAD_DIM] / denoms[-1])
        o_ref[0] = jnp.concatenate(outs, axis=0).T.astype(_BF)
        return jnp.min(jnp.concatenate(denoms, axis=0))

    smallest = attend(subtract_max=False)

    @pl.when(jnp.logical_not(smallest > SAFE_DENOM))
    def _():
        attend(subtract_max=True)


def _attn_b_kernel(qt_ref, k_ref, vt_ref, o_ref, m_ref, l_ref, acc_ref):
    i = pl.program_id(1)
    t = TOK_TILE

    def acc_rows(hd):
        return slice(hd * B_V_DIM, (hd + 1) * B_V_DIM)

    def diag_mask(s):
        krow = lax.broadcasted_iota(jnp.int32, (t, t), 0) // CHUNK
        qcol = lax.broadcasted_iota(jnp.int32, (t, t), 1) // CHUNK
        return jnp.where(krow <= qcol, s, NEG_INF)

    def sweep(kts, probs, commit):
        units = [(hd, kt) for hd in range(B_HEADS) for kt in kts]

        def scores(hd, kt):
            k = k_ref[0, hd, pl.ds(pl.multiple_of(kt * t, t), t), :]
            return _dot(k, qt_ref[0, hd])

        pending = [scores(*u) for u in units[:QK_AHEAD]]
        issued = QK_AHEAD
        for hd in range(B_HEADS):
            pv = ps = None
            for kt in kts:
                s = pending.pop(0)
                if issued < len(units):
                    pending.append(scores(*units[issued]))
                    issued += 1
                p = probs(hd, kt, s)
                pv_kt = _dot(vt_ref[0, hd, kt], p.astype(_BF))
                ps_kt = jnp.sum(p, axis=0, keepdims=True)
                pv = pv_kt if pv is None else pv + pv_kt
                ps = ps_kt if ps is None else ps + ps_kt
            commit(hd, pv, ps)

    def assign(hd, pv, ps):
        acc_ref[acc_rows(hd), :] = pv
        l_ref[hd:hd + 1, :] = ps

    def add(hd, pv, ps):
        acc_ref[acc_rows(hd), :] += pv
        l_ref[hd:hd + 1, :] += ps

    sweep([i], lambda hd, kt, s: jnp.exp2(diag_mask(s)), assign)
    safe = jnp.min(l_ref[...]) > SAFE_DENOM

    @pl.when(safe)
    def _():
        def full_tile(kt, carry):
            sweep([kt], lambda hd, kt, s: jnp.exp2(s), add)
            return carry

        lax.fori_loop(0, i, full_tile, 0)

    @pl.when(jnp.logical_not(safe))
    def _():
        m_ref[...] = jnp.full(m_ref.shape, NEG_INF, _F32)
        l_ref[...] = jnp.zeros(l_ref.shape, _F32)
        acc_ref[...] = jnp.zeros(acc_ref.shape, _F32)

        def online_step(kt, masked):
            alphas = {}

            def probs(hd, kt, s):
                if masked:
                    s = diag_mask(s)
                m = m_ref[hd:hd + 1, :]
                m_new = jnp.maximum(m, jnp.max(s, axis=0, keepdims=True))
                m_ref[hd:hd + 1, :] = m_new
                alphas[hd] = jnp.exp2(m - m_new)
                return jnp.exp2(s - m_new)

            def rescale_add(hd, pv, ps):
                acc_ref[acc_rows(hd), :] = alphas[hd] * acc_ref[acc_rows(hd), :] + pv
                l_ref[hd:hd + 1, :] = alphas[hd] * l_ref[hd:hd + 1, :] + ps

            sweep([kt], probs, rescale_add)

        def online_full(kt, carry):
            online_step(kt, masked=False)
            return carry

        lax.fori_loop(0, i, online_full, 0)
        online_step(i, masked=True)

    outs = [acc_ref[acc_rows(hd), :] / l_ref[hd:hd + 1, :] for hd in range(B_HEADS)]
    o_ref[0] = jnp.concatenate(outs, axis=0).T.astype(_BF)


def _post_kernel(x1_ref, oa_ref, ob_ref, wout_ref, g2_ref, wg_ref, wu_ref, wd_ref,
                 gf_ref, y_ref, act_ref):
    x2 = (x1_ref[...] + _dot(oa_ref[...], wout_ref[0:A_WIDTH, :])
          + _dot(ob_ref[...], wout_ref[A_WIDTH:A_WIDTH + B_WIDTH, :]))
    h = _rms(x2, g2_ref[...]).astype(_BF)
    x3 = x2 + 0.5 * _swiglu(h, wg_ref, wu_ref, wd_ref, act_ref)
    y_ref[...] = _rms(x3, gf_ref[...])


def _const_spec(shape):
    n = len(shape)
    return pl.BlockSpec(shape, lambda *_: (0,) * n, pipeline_mode=pl.Buffered(1))


def kernel(x, ffn1_norm, ffn1_w_gate, ffn1_w_up, ffn1_w_down, mix_norm, w_in, a_q_norm, a_k_norm, a_rel_bias, b_q_lat_norm, b_w_uq, b_kv_lat_norm, b_w_ukv, b_q_nope_norm, b_q_rope_norm, b_k_nope_norm, b_k_rope_norm, w_out, ffn2_norm, ffn2_w_gate, ffn2_w_up, ffn2_w_down, final_norm):
    b, s, d = x.shape
    assert d == D_MODEL and s % TOK_TILE == 0 and s >= A_WIN and ffn1_norm.shape[0] == 1
    tm = TOK_TILE
    nt = s // tm
    half = B_ROPE // 2

    def row(v):
        return v.astype(_F32)[None, :]

    w_in_l = w_in[0]
    o_va, o_cq = 2 * A_WIDTH, 3 * A_WIDTH
    o_ckv = o_cq + B_Q_LORA
    o_kr = o_ckv + B_KV_LORA
    w_fm = jnp.concatenate([w_in_l[:, 0:A_WIDTH], w_in_l[:, o_va:o_cq], w_in_l[:, A_WIDTH:o_va],
                            w_in_l[:, o_cq:o_ckv]], axis=1).T.astype(_BF)
    kr_cols = jnp.zeros((d, LANES), _F32).at[:, B_NOPE:B_QK_DIM].set(w_in_l[:, o_kr:o_kr + B_ROPE])
    w_tm = jnp.concatenate([w_in_l[:, o_ckv:o_kr], kr_cols], axis=1).astype(_BF)

    def col_tab(v):
        return jnp.broadcast_to(v.astype(_F32)[:, None], (v.shape[0], tm))

    gaq_tab = col_tab(a_q_norm[0] * (A_HEAD_DIM ** -0.5 * LOG2E))
    gak_tab = col_tab(a_k_norm[0])

    w_uq = (b_w_uq[0] * b_q_lat_norm[0][:, None]).reshape(B_Q_LORA, B_HEADS, B_QK_DIM)
    w_uq = jnp.pad(w_uq, ((0, 0), (0, 0), (0, HEAD_PAD - B_QK_DIM)))
    wq_t = w_uq.reshape(B_Q_LORA, B_HEADS * HEAD_PAD).T.astype(_BF)
    scale_b = (B_QK_DIM ** -0.5) * LOG2E
    gq_col = jnp.concatenate([b_q_nope_norm[0] * scale_b, b_q_rope_norm[0] * scale_b,
                              jnp.zeros((HEAD_PAD - B_QK_DIM,), _F32)])
    gq_tab = jnp.broadcast_to(gq_col[:, None], (HEAD_PAD, tm)).astype(_F32)
    q_norm2 = B_NOPE * jnp.max(b_q_nope_norm[0] ** 2) + B_ROPE * jnp.max(b_q_rope_norm[0] ** 2)
    k_norm2 = B_NOPE * jnp.max(b_k_nope_norm[0] ** 2) + B_ROPE * jnp.max(b_k_rope_norm[0] ** 2)
    bound_b = (scale_b * jnp.sqrt(q_norm2 * k_norm2) * BOUND_SLACK).astype(_BF).astype(_F32)
    qpad_tab = jnp.zeros((HEAD_PAD - B_QK_DIM, tm), _F32).at[0, :].set(-bound_b)
    kone = jnp.zeros((1, LANES), _F32).at[0, B_QK_DIM].set(1.0)

    inv = 1.0 / (ROPE_THETA ** (jnp.arange(0, B_ROPE, 2, dtype=_F32) / B_ROPE))
    ang = jnp.arange(s, dtype=_F32)[:, None] * inv[None, :]
    cos, sin = jnp.cos(ang), jnp.sin(ang)
    cos_t, sin_t = cos.T, sin.T
    zpad = jnp.zeros((s, B_NOPE), _F32)
    zend = jnp.zeros((s, LANES - B_QK_DIM), _F32)
    ck = jnp.concatenate([zpad, cos, cos, zend], axis=1)
    sk = jnp.concatenate([zpad, -sin, sin, zend], axis=1)

    w_ukv = b_w_ukv[0].reshape(B_KV_LORA, B_HEADS, B_NOPE + B_V_DIM)
    wvk_t = jnp.concatenate([w_ukv[..., B_NOPE:].reshape(B_KV_LORA, B_WIDTH),
                             w_ukv[..., :B_NOPE].reshape(B_KV_LORA, B_HEADS * B_NOPE)],
                            axis=1).T.astype(_BF)
    gkn_tab = col_tab(b_k_nope_norm[0])
    gkr = row(jnp.concatenate([jnp.zeros((B_NOPE,), _F32), b_k_rope_norm[0],
                               jnp.zeros((LANES - B_QK_DIM,), _F32)]))

    bound_a = ((A_HEAD_DIM ** -0.5 * LOG2E) * A_HEAD_DIM * jnp.max(jnp.abs(a_q_norm[0]))
               * jnp.max(jnp.abs(a_k_norm[0])) * BOUND_SLACK)
    bias_tiles = _rel_bias_tiles(a_rel_bias[0], bound_a)

    tok = lambda bi, j: (bi, j, 0)
    pre_in_specs = [
        pl.BlockSpec((None, tm, d), tok),
        _const_spec((1, d)), _const_spec((d, D_FF)), _const_spec((d, D_FF)), _const_spec((D_FF, d)),
        _const_spec((1, d)), _const_spec((FM_ROWS, d)), _const_spec((d, TM_COLS)),
        _const_spec((A_HEAD_DIM, tm)), _const_spec((A_HEAD_DIM, tm)),
        _const_spec((B_HEADS * HEAD_PAD, B_Q_LORA)), _const_spec((HEAD_PAD, tm)),
        _const_spec((HEAD_PAD - B_QK_DIM, tm)),
        pl.BlockSpec((half, tm), lambda bi, j: (0, j)), pl.BlockSpec((half, tm), lambda bi, j: (0, j)),
        _const_spec((1, B_KV_LORA)), _const_spec((B_WIDTH + B_HEADS * B_NOPE, B_KV_LORA)),
        _const_spec((B_NOPE, tm)),
        _const_spec((1, LANES)), _const_spec((1, LANES)),
        pl.BlockSpec((tm, LANES), lambda bi, j: (j, 0)), pl.BlockSpec((tm, LANES), lambda bi, j: (j, 0)),
    ]
    pre_out_shapes = [
        jax.ShapeDtypeStruct((b, s, d), _F32),
        jax.ShapeDtypeStruct((b, A_WIDTH, s), _BF),
        jax.ShapeDtypeStruct((b, s, A_WIDTH), _BF),
        jax.ShapeDtypeStruct((b, A_HEADS, s // A_Q_TILE, A_HEAD_DIM, A_Q_TILE), _BF),
        jax.ShapeDtypeStruct((b, B_HEADS, HEAD_PAD, s), _BF),
        jax.ShapeDtypeStruct((b, B_HEADS, s, HEAD_PAD), _BF),
        jax.ShapeDtypeStruct((b, B_HEADS, nt, B_V_DIM, tm), _BF),
    ]
    pre_out_specs = [
        pl.BlockSpec((None, tm, d), tok),
        pl.BlockSpec((1, A_WIDTH, tm), lambda bi, j: (bi, 0, j)),
        pl.BlockSpec((1, tm, A_WIDTH), tok),
        pl.BlockSpec((1, A_HEADS, tm // A_Q_TILE, A_HEAD_DIM, A_Q_TILE),
                     lambda bi, j: (bi, 0, j, 0, 0)),
        pl.BlockSpec((1, B_HEADS, HEAD_PAD, tm), lambda bi, j: (bi, 0, 0, j)),
        pl.BlockSpec((1, B_HEADS, tm, HEAD_PAD), lambda bi, j: (bi, 0, j, 0)),
        pl.BlockSpec((1, B_HEADS, 1, B_V_DIM, tm), lambda bi, j: (bi, 0, j, 0, 0)),
    ]
    x1, qat, ka, vat, qbt, kb, vbt = pl.pallas_call(
        _pre_kernel,
        grid=(b, nt),
        in_specs=pre_in_specs,
        out_specs=pre_out_specs,
        out_shape=pre_out_shapes,
        scratch_shapes=[pltpu.VMEM((tm, D_FF), _BF)],
        compiler_params=pltpu.CompilerParams(
            dimension_semantics=("arbitrary", "arbitrary"), vmem_limit_bytes=VMEM_LIMIT),
        name="pre",
    )(x, row(ffn1_norm[0]), ffn1_w_gate[0].astype(_BF), ffn1_w_up[0].astype(_BF),
      ffn1_w_down[0].astype(_BF), row(mix_norm[0]), w_fm, w_tm,
      gaq_tab, gak_tab,
      wq_t, gq_tab, qpad_tab, cos_t, sin_t,
      row(b_kv_lat_norm[0]), wvk_t, gkn_tab,
      gkr, kone, ck, sk)

    na = s // A_Q_TILE
    out_a = pl.pallas_call(
        _attn_a_kernel,
        grid=(b, na),
        in_specs=[
            pl.BlockSpec((1, A_WIDTH, A_Q_TILE), lambda bi, j: (bi, 0, j)),
            pl.BlockSpec((1, s, A_WIDTH), lambda bi, j: (bi, 0, 0), pipeline_mode=pl.Buffered(1)),
            pl.BlockSpec((1, A_HEADS, na, A_HEAD_DIM, A_Q_TILE), lambda bi, j: (bi, 0, 0, 0, 0),
                         pipeline_mode=pl.Buffered(1)),
            _const_spec((A_HEADS, A_BIAS_TILES, LANES, A_Q_TILE)),
        ],
        out_specs=pl.BlockSpec((1, A_Q_TILE, A_WIDTH), tok),
        out_shape=jax.ShapeDtypeStruct((b, s, A_WIDTH), _BF),
        compiler_params=pltpu.CompilerParams(
            dimension_semantics=("arbitrary", "arbitrary"), vmem_limit_bytes=VMEM_LIMIT),
        name="attn_a",
    )(qat, ka, vat, bias_tiles)

    out_b = pl.pallas_call(
        _attn_b_kernel,
        grid=(b, nt),
        in_specs=[
            pl.BlockSpec((1, B_HEADS, HEAD_PAD, tm), lambda bi, j: (bi, 0, 0, j)),
            pl.BlockSpec((1, B_HEADS, s, HEAD_PAD), lambda bi, j: (bi, 0, 0, 0),
                         pipeline_mode=pl.Buffered(1)),
            pl.BlockSpec((1, B_HEADS, nt, B_V_DIM, tm), lambda bi, j: (bi, 0, 0, 0, 0),
                         pipeline_mode=pl.Buffered(1)),
        ],
        out_specs=pl.BlockSpec((1, tm, B_WIDTH), tok),
        out_shape=jax.ShapeDtypeStruct((b, s, B_WIDTH), _BF),
        scratch_shapes=[pltpu.VMEM((B_HEADS, tm), _F32), pltpu.VMEM((B_HEADS, tm), _F32),
                        pltpu.VMEM((B_WIDTH, tm), _F32)],
        compiler_params=pltpu.CompilerParams(
            dimension_semantics=("arbitrary", "arbitrary"), vmem_limit_bytes=VMEM_LIMIT),
        name="attn_b",
    )(qbt, kb, vbt)

    n = b * s
    flat = lambda j: (j, 0)
    y = pl.pallas_call(
        _post_kernel,
        grid=(n // tm,),
        in_specs=[
            pl.BlockSpec((tm, d), flat),
            pl.BlockSpec((tm, A_WIDTH), flat),
            pl.BlockSpec((tm, B_WIDTH), flat),
            _const_spec((A_WIDTH + B_WIDTH, d)),
            _const_spec((1, d)), _const_spec((d, D_FF)), _const_spec((d, D_FF)), _const_spec((D_FF, d)),
            _const_spec((1, d)),
        ],
        out_specs=pl.BlockSpec((tm, d), flat),
        out_shape=jax.ShapeDtypeStruct((n, d), _F32),
        scratch_shapes=[pltpu.VMEM((tm, D_FF), _BF)],
        compiler_params=pltpu.CompilerParams(
            dimension_semantics=("arbitrary",), vmem_limit_bytes=VMEM_LIMIT),
        name="post",
    )(x1.reshape(n, d), out_a.reshape(n, A_WIDTH), out_b.reshape(n, B_WIDTH),
      w_out[0].astype(_BF), row(ffn2_norm[0]), ffn2_w_gate[0].astype(_BF),
      ffn2_w_up[0].astype(_BF), ffn2_w_down[0].astype(_BF), row(final_norm[0]))
    return y.reshape(b, s, d)
```

```python
import jax
import jax.numpy as jnp
from jax import lax
from jax.experimental import pallas as pl
from jax.experimental.pallas import tpu as pltpu

D_MODEL = 1024
D_FF = 2816
CHUNK = 64
A_HEADS = 8
A_HEAD_DIM = 64
A_LEFT_CHUNKS = 8
A_MAX_REL = 128
A_WIDTH = A_HEADS * A_HEAD_DIM
B_HEADS = 8
B_Q_LORA = 256
B_KV_LORA = 128
B_NOPE = 64
B_ROPE = 32
B_V_DIM = 64
B_QK_DIM = B_NOPE + B_ROPE
B_WIDTH = B_HEADS * B_V_DIM
ROPE_THETA = 10000.0
EPS = 1e-6
NEG_INF = -1e30
LOG2E = 1.4426950408889634
BOUND_SLACK = 1.0 + 2.0 ** -5
SAFE_DENOM = 2.0 ** -80

LANES = 128
BF16_SUBLANES = 16
MXU_DIM = 256
TOK_TILE = 512
FF_CHUNK = 256
ROW_SPLITS = 2
QK_AHEAD = 2

A_Q_TILE = 256
A_LEFT = A_LEFT_CHUNKS * CHUNK
A_WIN = A_Q_TILE + A_LEFT
A_KEY_SPLITS = ((0, 384), (384, 768))
A_BIAS_ROWS = A_WIN + A_LEFT
A_BIAS_TILES = A_BIAS_ROWS // LANES
A_TAB_ZERO = A_BIAS_ROWS - A_LEFT + LANES
A_TAB_LEN = A_TAB_ZERO + A_LEFT + A_Q_TILE

HEAD_PAD = 128
ACC_ROWS = B_V_DIM + BF16_SUBLANES
FM_ROWS = 3 * A_WIDTH + B_Q_LORA
TM_COLS = B_KV_LORA + LANES
VMEM_LIMIT = 60 * 1024 * 1024

_BF = jnp.bfloat16
_F32 = jnp.float32


def _dot(a, b):
    return jnp.dot(a, b, preferred_element_type=_F32)


def _rms(x, g):
    ms = jnp.mean(x * x, axis=-1, keepdims=True)
    return x * lax.rsqrt(ms + EPS) * g


def _col_rms(x_t):
    return x_t * lax.rsqrt(jnp.mean(x_t * x_t, axis=0, keepdims=True) + EPS)


def _swiglu(h, wg_ref, wu_ref, wd_ref, act_ref):
    for c in range(D_FF // FF_CHUNK):
        sl = slice(c * FF_CHUNK, (c + 1) * FF_CHUNK)
        g = _dot(h, wg_ref[:, sl])
        u = _dot(h, wu_ref[:, sl])
        act_ref[:, sl] = (jax.nn.silu(g) * u).astype(_BF)
    return [_dot(act_ref[rows, :], wd_ref[...]) for rows in _row_blocks(h.shape[0])]


def _row_blocks(n):
    step = n // ROW_SPLITS
    return [slice(r, r + step) for r in range(0, n, step)]


def _pre_kernel(x_ref, g1_ref, wg_ref, wu_ref, wd_ref, gmix_ref, wfm_ref, wtm_ref,
                gaq_ref, gak_ref,
                wqt_ref, gqtab_ref, qpad_ref, cost_ref, sint_ref,
                gckv_ref, wvk_ref, gkn_ref,
                gkr_ref, kone_ref, ck_ref, sk_ref,
                x1_ref, qat_ref, ka_ref, vat_ref, qbt_ref, kb_ref, vbt_ref,
                act_ref):
    x = x_ref[...]
    tm = x.shape[0]
    h = _rms(x, g1_ref[...]).astype(_BF)
    ffn = _swiglu(h, wg_ref, wu_ref, wd_ref, act_ref)
    h2_blocks = []
    for rows, f in zip(_row_blocks(tm), ffn):
        x1 = x[rows] + 0.5 * f
        x1_ref[rows, :] = x1
        h2_blocks.append(_rms(x1, gmix_ref[...]))
    h2f = jnp.concatenate(h2_blocks, axis=0)
    h2 = h2f.astype(_BF)

    tmj = _dot(h2, wtm_ref[...])
    h2t = h2f.T.astype(_BF)
    ckvn_t = _rms(tmj[:, 0:B_KV_LORA], gckv_ref[...]).T.astype(_BF)
    cq_t = _dot(wfm_ref[3 * A_WIDTH:FM_ROWS, :], h2t)
    vk_t = _dot(wvk_ref[...], ckvn_t)
    q_t = _dot(wqt_ref[...], _col_rms(cq_t).astype(_BF))
    ka_fm = _dot(wfm_ref[2 * A_WIDTH:3 * A_WIDTH, :], h2t)
    qa_fm = _dot(wfm_ref[0:A_WIDTH, :], h2t)
    va_fm = _dot(wfm_ref[A_WIDTH:2 * A_WIDTH, :], h2t)

    gaq = gaq_ref[...]
    gak = gak_ref[...]
    ka_t = []
    for hd in range(A_HEADS):
        rows = slice(hd * A_HEAD_DIM, (hd + 1) * A_HEAD_DIM)
        qat_ref[0, rows, :] = (_col_rms(qa_fm[rows]) * gaq).astype(_BF)
        for g in range(tm // A_Q_TILE):
            vat_ref[0, hd, g] = va_fm[rows, g * A_Q_TILE:(g + 1) * A_Q_TILE].astype(_BF)
        ka_t.append(_col_rms(ka_fm[rows]) * gak)
    ka_ref[0] = jnp.concatenate(ka_t, axis=0).T.astype(_BF)

    cos_t = cost_ref[...]
    sin_t = sint_ref[...]
    gq = gqtab_ref[...]
    half = B_ROPE // 2
    for hd in range(B_HEADS):
        r0 = hd * HEAD_PAD
        nope = _col_rms(q_t[r0:r0 + B_NOPE]) * gq[0:B_NOPE]
        rope = _col_rms(q_t[r0 + B_NOPE:r0 + B_QK_DIM]) * gq[B_NOPE:B_QK_DIM]
        r1 = rope[0:half]
        r2 = rope[half:B_ROPE]
        blk = jnp.concatenate(
            [nope, r1 * cos_t - r2 * sin_t, r1 * sin_t + r2 * cos_t, qpad_ref[...]], axis=0)
        qbt_ref[0, hd] = blk.astype(_BF)

    gkn = gkn_ref[...]
    kn_t = []
    for hd in range(B_HEADS):
        vbt_ref[0, hd, 0] = vk_t[hd * B_V_DIM:(hd + 1) * B_V_DIM].astype(_BF)
        kn_t.append(_col_rms(vk_t[B_WIDTH + hd * B_NOPE:B_WIDTH + (hd + 1) * B_NOPE]) * gkn)
    kn = jnp.concatenate(kn_t, axis=0).T
    kr = tmj[:, B_KV_LORA:TM_COLS]
    y = kr * lax.rsqrt(jnp.sum(kr * kr, axis=-1, keepdims=True) * (1.0 / B_ROPE) + EPS) * gkr_ref[...]
    lane = lax.broadcasted_iota(jnp.int32, y.shape, 1)
    swapped = jnp.where(lane < B_NOPE + half,
                        pltpu.roll(y, LANES - half, axis=1), pltpu.roll(y, half, axis=1))
    kpe = y * ck_ref[...] + swapped * sk_ref[...] + kone_ref[...]
    for hd in range(B_HEADS):
        pair_tile = kn[:, (hd // 2) * LANES:(hd // 2 + 1) * LANES]
        if hd % 2 == 1:
            pair_tile = pltpu.roll(pair_tile, B_NOPE, axis=1)
        kb_ref[0, hd] = jnp.where(lane < B_NOPE, pair_tile, kpe).astype(_BF)


def _bias_kernel(tab_ref, o_ref):
    u = lax.broadcasted_iota(jnp.int32, (LANES, A_Q_TILE), 0)
    qc = lax.broadcasted_iota(jnp.int32, (LANES, A_Q_TILE), 1) // CHUNK
    for tile in range(A_BIAS_TILES):
        a = A_TAB_ZERO + A_LEFT - LANES * tile
        w = jnp.concatenate([tab_ref[0, :, a:a + A_Q_TILE], tab_ref[0, :, a - A_Q_TILE:a]], axis=1)
        rolled = pltpu.roll(jnp.broadcast_to(w, (LANES, 2 * A_Q_TILE)), 0, 1,
                            stride=1, stride_axis=0)
        kc = (LANES * tile + u) // CHUNK - A_LEFT_CHUNKS
        valid = (kc <= qc) & (kc >= qc - A_LEFT_CHUNKS)
        o_ref[0, tile] = jnp.where(valid, rolled[:, :A_Q_TILE], NEG_INF)


def _rel_bias_tiles(rel_bias, qk_bound):
    pad_lo = A_TAB_ZERO - A_MAX_REL
    pad_hi = A_TAB_LEN - pad_lo - (2 * A_MAX_REL + 1)
    rb = rel_bias.astype(_F32) * LOG2E
    rb = rb - (qk_bound + jnp.max(rb, axis=1, keepdims=True))
    table = jnp.concatenate([jnp.broadcast_to(rb[:, :1], (A_HEADS, pad_lo)), rb,
                             jnp.broadcast_to(rb[:, -1:], (A_HEADS, pad_hi))], axis=1)
    return pl.pallas_call(
        _bias_kernel,
        grid=(A_HEADS,),
        in_specs=[pl.BlockSpec((1, 1, A_TAB_LEN), lambda h: (h, 0, 0))],
        out_specs=pl.BlockSpec((1, A_BIAS_TILES, LANES, A_Q_TILE), lambda h: (h, 0, 0, 0)),
        out_shape=jax.ShapeDtypeStruct((A_HEADS, A_BIAS_TILES, LANES, A_Q_TILE), _F32),
        name="rel_bias",
    )(table.reshape(A_HEADS, 1, A_TAB_LEN))


def _attn_a_kernel(qt_ref, k_ref, vt_ref, bias_ref, o_ref):
    i = pl.program_id(1)
    left_tiles = A_LEFT // A_Q_TILE
    g0 = jnp.maximum(i - left_tiles, 0)
    start = pl.multiple_of(g0 * A_Q_TILE, A_Q_TILE)
    u0 = (left_tiles - jnp.minimum(i, left_tiles)) * (A_Q_TILE // LANES)
    zeros = jnp.zeros((A_HEAD_DIM, A_Q_TILE), _BF)
    ones_rows = jnp.ones((ACC_ROWS - A_HEAD_DIM, A_WIN), _BF)

    def scores(pair):
        q2 = qt_ref[0, pair * LANES:(pair + 1) * LANES, :]
        q_even = jnp.concatenate([q2[:A_HEAD_DIM], zeros], axis=0)
        q_odd = jnp.concatenate([zeros, q2[A_HEAD_DIM:]], axis=0)
        qq = jnp.concatenate([q_even, q_odd], axis=1)
        parts = []
        for r0, r1 in A_KEY_SPLITS:
            k = k_ref[0, pl.ds(pl.multiple_of(start + r0, LANES), r1 - r0),
                      pair * LANES:(pair + 1) * LANES]
            parts.append(_dot(k, qq))
        return jnp.concatenate(parts, axis=0)

    def attend(subtract_max):
        n_pairs = A_HEADS // 2
        pending = [scores(0)]
        outs, denoms = [], []
        for pair in range(n_pairs):
            s = pending.pop(0)
            if pair + 1 < n_pairs:
                pending.append(scores(pair + 1))
            bias = jnp.concatenate(
                [jnp.concatenate([bias_ref[2 * pair + e, u0 + j] for j in range(A_WIN // LANES)],
                                 axis=0) for e in range(2)], axis=1)
            s = s + bias
            if subtract_max:
                s = s - jnp.max(s, axis=0, keepdims=True)
            p = jnp.exp2(s).astype(_BF)
            for e in range(2):
                hd = 2 * pair + e
                v = jnp.concatenate([vt_ref[0, hd, g0 + g] for g in range(A_WIN // A_Q_TILE)],
                                    axis=1)
                o = _dot(jnp.concatenate([v, ones_rows], axis=0),
                         p[:, e * A_Q_TILE:(e + 1) * A_Q_TILE])
                denoms.append(o[A_HEAD_DIM:A_HEAD_DIM + 1])
                outs.append(o[:A_HEAD_DIM] / denoms[-1])
        o_ref[0] = jnp.concatenate(outs, axis=0).T.astype(_BF)
        return jnp.min(jnp.concatenate(denoms, axis=0))

    smallest = attend(subtract_max=False)

    @pl.when(jnp.logical_not(smallest > SAFE_DENOM))
    def _():
        attend(subtract_max=True)


def _attn_b_kernel(qt_ref, k_ref, vt_ref, o_ref, m_ref, l_ref, acc_ref):
    i = pl.program_id(1)
    t = TOK_TILE

    def acc_rows(hd):
        return slice(hd * B_V_DIM, (hd + 1) * B_V_DIM)

    def diag_mask(s):
        krow = lax.broadcasted_iota(jnp.int32, (t, t), 0) // CHUNK
        qcol = lax.broadcasted_iota(jnp.int32, (t, t), 1) // CHUNK
        return jnp.where(krow <= qcol, s, NEG_INF)

    def sweep(kts, probs, commit):
        units = [(hd, kt) for hd in range(B_HEADS) for kt in kts]

        def scores(hd, kt):
            k = k_ref[0, hd, pl.ds(pl.multiple_of(kt * t, t), t), :]
            return _dot(k, qt_ref[0, hd])

        pending = [scores(*u) for u in units[:QK_AHEAD]]
        issued = QK_AHEAD
        for hd in range(B_HEADS):
            pv = ps = None
            for kt in kts:
                s = pending.pop(0)
                if issued < len(units):
                    pending.append(scores(*units[issued]))
                    issued += 1
                p = probs(hd, kt, s)
                pv_kt = _dot(vt_ref[0, hd, kt], p.astype(_BF))
                ps_kt = jnp.sum(p, axis=0, keepdims=True)
                pv = pv_kt if pv is None else pv + pv_kt
                ps = ps_kt if ps is None else ps + ps_kt
            commit(hd, pv, ps)

    def assign(hd, pv, ps):
        acc_ref[acc_rows(hd), :] = pv
        l_ref[hd:hd + 1, :] = ps

    def add(hd, pv, ps):
        acc_ref[acc_rows(hd), :] += pv
        l_ref[hd:hd + 1, :] += ps

    sweep([i], lambda hd, kt, s: jnp.exp2(diag_mask(s)), assign)
    safe = jnp.min(l_ref[...]) > SAFE_DENOM

    @pl.when(safe)
    def _():
        def full_tile(kt, carry):
            sweep([kt], lambda hd, kt, s: jnp.exp2(s), add)
            return carry

        lax.fori_loop(0, i, full_tile, 0)

    @pl.when(jnp.logical_not(safe))
    def _():
        m_ref[...] = jnp.full(m_ref.shape, NEG_INF, _F32)
        l_ref[...] = jnp.zeros(l_ref.shape, _F32)
        acc_ref[...] = jnp.zeros(acc_ref.shape, _F32)

        def online_step(kt, masked):
            alphas = {}

            def probs(hd, kt, s):
                if masked:
                    s = diag_mask(s)
                m = m_ref[hd:hd + 1, :]
                m_new = jnp.maximum(m, jnp.max(s, axis=0, keepdims=True))
                m_ref[hd:hd + 1, :] = m_new
                alphas[hd] = jnp.exp2(m - m_new)
                return jnp.exp2(s - m_new)

            def rescale_add(hd, pv, ps):
                acc_ref[acc_rows(hd), :] = alphas[hd] * acc_ref[acc_rows(hd), :] + pv
                l_ref[hd:hd + 1, :] = alphas[hd] * l_ref[hd:hd + 1, :] + ps

            sweep([kt], probs, rescale_add)

        def online_full(kt, carry):
            online_step(kt, masked=False)
            return carry

        lax.fori_loop(0, i, online_full, 0)
        online_step(i, masked=True)

    outs = [acc_ref[acc_rows(hd), :] / l_ref[hd:hd + 1, :] for hd in range(B_HEADS)]
    o_ref[0] = jnp.concatenate(outs, axis=0).T.astype(_BF)


def _post_kernel(x1_ref, oa_ref, ob_ref, wout_ref, g2_ref, wg_ref, wu_ref, wd_ref,
                 gf_ref, y_ref, act_ref):
    blocks = _row_blocks(x1_ref.shape[0])
    x2 = [x1_ref[rows, :] + _dot(oa_ref[rows, :], wout_ref[0:A_WIDTH, :])
          + _dot(ob_ref[rows, :], wout_ref[A_WIDTH:A_WIDTH + B_WIDTH, :]) for rows in blocks]
    h = jnp.concatenate([_rms(v, g2_ref[...]) for v in x2], axis=0).astype(_BF)
    ffn = _swiglu(h, wg_ref, wu_ref, wd_ref, act_ref)
    for rows, v, f in zip(blocks, x2, ffn):
        y_ref[rows, :] = _rms(v + 0.5 * f, gf_ref[...])


def _const_spec(shape):
    n = len(shape)
    return pl.BlockSpec(shape, lambda *_: (0,) * n, pipeline_mode=pl.Buffered(1))


def kernel(x, ffn1_norm, ffn1_w_gate, ffn1_w_up, ffn1_w_down, mix_norm, w_in, a_q_norm, a_k_norm, a_rel_bias, b_q_lat_norm, b_w_uq, b_kv_lat_norm, b_w_ukv, b_q_nope_norm, b_q_rope_norm, b_k_nope_norm, b_k_rope_norm, w_out, ffn2_norm, ffn2_w_gate, ffn2_w_up, ffn2_w_down, final_norm):
    b, s, d = x.shape
    assert d == D_MODEL and s % TOK_TILE == 0 and s >= A_WIN and ffn1_norm.shape[0] == 1
    tm = TOK_TILE
    nt = s // tm
    half = B_ROPE // 2

    def row(v):
        return v.astype(_F32)[None, :]

    w_in_l = w_in[0]
    o_va, o_cq = 2 * A_WIDTH, 3 * A_WIDTH
    o_ckv = o_cq + B_Q_LORA
    o_kr = o_ckv + B_KV_LORA
    w_fm = jnp.concatenate([w_in_l[:, 0:A_WIDTH], w_in_l[:, o_va:o_cq], w_in_l[:, A_WIDTH:o_va],
                            w_in_l[:, o_cq:o_ckv]], axis=1).T.astype(_BF)
    kr_cols = jnp.zeros((d, LANES), _F32).at[:, B_NOPE:B_QK_DIM].set(w_in_l[:, o_kr:o_kr + B_ROPE])
    w_tm = jnp.concatenate([w_in_l[:, o_ckv:o_kr], kr_cols], axis=1).astype(_BF)

    def col_tab(v):
        return jnp.broadcast_to(v.astype(_F32)[:, None], (v.shape[0], tm))

    gaq_tab = col_tab(a_q_norm[0] * (A_HEAD_DIM ** -0.5 * LOG2E))
    gak_tab = col_tab(a_k_norm[0])

    w_uq = (b_w_uq[0] * b_q_lat_norm[0][:, None]).reshape(B_Q_LORA, B_HEADS, B_QK_DIM)
    w_uq = jnp.pad(w_uq, ((0, 0), (0, 0), (0, HEAD_PAD - B_QK_DIM)))
    wq_t = w_uq.reshape(B_Q_LORA, B_HEADS * HEAD_PAD).T.astype(_BF)
    scale_b = (B_QK_DIM ** -0.5) * LOG2E
    gq_col = jnp.concatenate([b_q_nope_norm[0] * scale_b, b_q_rope_norm[0] * scale_b,
                              jnp.zeros((HEAD_PAD - B_QK_DIM,), _F32)])
    gq_tab = jnp.broadcast_to(gq_col[:, None], (HEAD_PAD, tm)).astype(_F32)
    q_norm2 = B_NOPE * jnp.max(b_q_nope_norm[0] ** 2) + B_ROPE * jnp.max(b_q_rope_norm[0] ** 2)
    k_norm2 = B_NOPE * jnp.max(b_k_nope_norm[0] ** 2) + B_ROPE * jnp.max(b_k_rope_norm[0] ** 2)
    bound_b = (scale_b * jnp.sqrt(q_norm2 * k_norm2) * BOUND_SLACK).astype(_BF).astype(_F32)
    qpad_tab = jnp.zeros((HEAD_PAD - B_QK_DIM, tm), _F32).at[0, :].set(-bound_b)
    kone = jnp.zeros((1, LANES), _F32).at[0, B_QK_DIM].set(1.0)

    inv = 1.0 / (ROPE_THETA ** (jnp.arange(0, B_ROPE, 2, dtype=_F32) / B_ROPE))
    ang = jnp.arange(s, dtype=_F32)[:, None] * inv[None, :]
    cos, sin = jnp.cos(ang), jnp.sin(ang)
    cos_t, sin_t = cos.T, sin.T
    zpad = jnp.zeros((s, B_NOPE), _F32)
    zend = jnp.zeros((s, LANES - B_QK_DIM), _F32)
    ck = jnp.concatenate([zpad, cos, cos, zend], axis=1)
    sk = jnp.concatenate([zpad, -sin, sin, zend], axis=1)

    w_ukv = b_w_ukv[0].reshape(B_KV_LORA, B_HEADS, B_NOPE + B_V_DIM)
    wvk_t = jnp.concatenate([w_ukv[..., B_NOPE:].reshape(B_KV_LORA, B_WIDTH),
                             w_ukv[..., :B_NOPE].reshape(B_KV_LORA, B_HEADS * B_NOPE)],
                            axis=1).T.astype(_BF)
    gkn_tab = col_tab(b_k_nope_norm[0])
    gkr = row(jnp.concatenate([jnp.zeros((B_NOPE,), _F32), b_k_rope_norm[0],
                               jnp.zeros((LANES - B_QK_DIM,), _F32)]))

    bound_a = ((A_HEAD_DIM ** -0.5 * LOG2E) * A_HEAD_DIM * jnp.max(jnp.abs(a_q_norm[0]))
               * jnp.max(jnp.abs(a_k_norm[0])) * BOUND_SLACK)
    bias_tiles = _rel_bias_tiles(a_rel_bias[0], bound_a)

    tok = lambda bi, j: (bi, j, 0)
    pre_in_specs = [
        pl.BlockSpec((None, tm, d), tok),
        _const_spec((1, d)), _const_spec((d, D_FF)), _const_spec((d, D_FF)), _const_spec((D_FF, d)),
        _const_spec((1, d)), _const_spec((FM_ROWS, d)), _const_spec((d, TM_COLS)),
        _const_spec((A_HEAD_DIM, tm)), _const_spec((A_HEAD_DIM, tm)),
        _const_spec((B_HEADS * HEAD_PAD, B_Q_LORA)), _const_spec((HEAD_PAD, tm)),
        _const_spec((HEAD_PAD - B_QK_DIM, tm)),
        pl.BlockSpec((half, tm), lambda bi, j: (0, j)), pl.BlockSpec((half, tm), lambda bi, j: (0, j)),
        _const_spec((1, B_KV_LORA)), _const_spec((B_WIDTH + B_HEADS * B_NOPE, B_KV_LORA)),
        _const_spec((B_NOPE, tm)),
        _const_spec((1, LANES)), _const_spec((1, LANES)),
        pl.BlockSpec((tm, LANES), lambda bi, j: (j, 0)), pl.BlockSpec((tm, LANES), lambda bi, j: (j, 0)),
    ]
    pre_out_shapes = [
        jax.ShapeDtypeStruct((b, s, d), _F32),
        jax.ShapeDtypeStruct((b, A_WIDTH, s), _BF),
        jax.ShapeDtypeStruct((b, s, A_WIDTH), _BF),
        jax.ShapeDtypeStruct((b, A_HEADS, s // A_Q_TILE, A_HEAD_DIM, A_Q_TILE), _BF),
        jax.ShapeDtypeStruct((b, B_HEADS, HEAD_PAD, s), _BF),
        jax.ShapeDtypeStruct((b, B_HEADS, s, HEAD_PAD), _BF),
        jax.ShapeDtypeStruct((b, B_HEADS, nt, B_V_DIM, tm), _BF),
    ]
    pre_out_specs = [
        pl.BlockSpec((None, tm, d), tok),
        pl.BlockSpec((1, A_WIDTH, tm), lambda bi, j: (bi, 0, j)),
        pl.BlockSpec((1, tm, A_WIDTH), tok),
        pl.BlockSpec((1, A_HEADS, tm // A_Q_TILE, A_HEAD_DIM, A_Q_TILE),
                     lambda bi, j: (bi, 0, j, 0, 0)),
        pl.BlockSpec((1, B_HEADS, HEAD_PAD, tm), lambda bi, j: (bi, 0, 0, j)),
        pl.BlockSpec((1, B_HEADS, tm, HEAD_PAD), lambda bi, j: (bi, 0, j, 0)),
        pl.BlockSpec((1, B_HEADS, 1, B_V_DIM, tm), lambda bi, j: (bi, 0, j, 0, 0)),
    ]
    x1, qat, ka, vat, qbt, kb, vbt = pl.pallas_call(
        _pre_kernel,
        grid=(b, nt),
        in_specs=pre_in_specs,
        out_specs=pre_out_specs,
        out_shape=pre_out_shapes,
        scratch_shapes=[pltpu.VMEM((tm, D_FF), _BF)],
        compiler_params=pltpu.CompilerParams(
            dimension_semantics=("arbitrary", "arbitrary"), vmem_limit_bytes=VMEM_LIMIT),
        name="pre",
    )(x, row(ffn1_norm[0]), ffn1_w_gate[0].astype(_BF), ffn1_w_up[0].astype(_BF),
      ffn1_w_down[0].astype(_BF), row(mix_norm[0]), w_fm, w_tm,
      gaq_tab, gak_tab,
      wq_t, gq_tab, qpad_tab, cos_t, sin_t,
      row(b_kv_lat_norm[0]), wvk_t, gkn_tab,
      gkr, kone, ck, sk)

    na = s // A_Q_TILE
    out_a = pl.pallas_call(
        _attn_a_kernel,
        grid=(b, na),
        in_specs=[
            pl.BlockSpec((1, A_WIDTH, A_Q_TILE), lambda bi, j: (bi, 0, j)),
            pl.BlockSpec((1, s, A_WIDTH), lambda bi, j: (bi, 0, 0)),
            pl.BlockSpec((1, A_HEADS, na, A_HEAD_DIM, A_Q_TILE), lambda bi, j: (bi, 0, 0, 0, 0)),
            _const_spec((A_HEADS, A_BIAS_TILES, LANES, A_Q_TILE)),
        ],
        out_specs=pl.BlockSpec((1, A_Q_TILE, A_WIDTH), tok),
        out_shape=jax.ShapeDtypeStruct((b, s, A_WIDTH), _BF),
        compiler_params=pltpu.CompilerParams(
            dimension_semantics=("arbitrary", "arbitrary"), vmem_limit_bytes=VMEM_LIMIT),
        name="attn_a",
    )(qat, ka, vat, bias_tiles)

    out_b = pl.pallas_call(
        _attn_b_kernel,
        grid=(b, nt),
        in_specs=[
            pl.BlockSpec((1, B_HEADS, HEAD_PAD, tm), lambda bi, j: (bi, 0, 0, j)),
            pl.BlockSpec((1, B_HEADS, s, HEAD_PAD), lambda bi, j: (bi, 0, 0, 0)),
            pl.BlockSpec((1, B_HEADS, nt, B_V_DIM, tm), lambda bi, j: (bi, 0, 0, 0, 0)),
        ],
        out_specs=pl.BlockSpec((1, tm, B_WIDTH), tok),
        out_shape=jax.ShapeDtypeStruct((b, s, B_WIDTH), _BF),
        scratch_shapes=[pltpu.VMEM((B_HEADS, tm), _F32), pltpu.VMEM((B_HEADS, tm), _F32),
                        pltpu.VMEM((B_WIDTH, tm), _F32)],
        compiler_params=pltpu.CompilerParams(
            dimension_semantics=("arbitrary", "arbitrary"), vmem_limit_bytes=VMEM_LIMIT),
        name="attn_b",
    )(qbt, kb, vbt)

    n = b * s
    flat = lambda j: (j, 0)
    y = pl.pallas_call(
        _post_kernel,
        grid=(n // tm,),
        in_specs=[
            pl.BlockSpec((tm, d), flat),
            pl.BlockSpec((tm, A_WIDTH), flat),
            pl.BlockSpec((tm, B_WIDTH), flat),
            _const_spec((A_WIDTH + B_WIDTH, d)),
            _const_spec((1, d)), _const_spec((d, D_FF)), _const_spec((d, D_FF)), _const_spec((D_FF, d)),
            _const_spec((1, d)),
        ],
        out_specs=pl.BlockSpec((tm, d), flat),
        out_shape=jax.ShapeDtypeStruct((n, d), _F32),
        scratch_shapes=[pltpu.VMEM((tm, D_FF), _BF)],
        compiler_params=pltpu.CompilerParams(
            dimension_semantics=("arbitrary",), vmem_limit_bytes=VMEM_LIMIT),
        name="post",
    )(x1.reshape(n, d), out_a.reshape(n, A_WIDTH), out_b.reshape(n, B_WIDTH),
      w_out[0].astype(_BF), row(ffn2_norm[0]), ffn2_w_gate[0].astype(_BF),
      ffn2_w_up[0].astype(_BF), ffn2_w_down[0].astype(_BF), row(final_norm[0]))
    return y.reshape(b, s, d)
```

```python
import jax
import jax.numpy as jnp
import numpy as np
from jax import lax
from jax.experimental import pallas as pl
from jax.experimental.pallas import tpu as pltpu

D_MODEL = 1024
D_FF = 2816
CHUNK = 64
A_HEADS = 8
A_HEAD_DIM = 64
A_LEFT_CHUNKS = 8
A_MAX_REL = 128
A_WIDTH = A_HEADS * A_HEAD_DIM
B_HEADS = 8
B_Q_LORA = 256
B_KV_LORA = 128
B_NOPE = 64
B_ROPE = 32
B_V_DIM = 64
B_QK_DIM = B_NOPE + B_ROPE
B_WIDTH = B_HEADS * B_V_DIM
ROPE_THETA = 10000.0
EPS = 1e-6
NEG_INF = -1e30
LOG2E = 1.4426950408889634
BOUND_SLACK = 1.0 + 2.0 ** -5
SAFE_DENOM = 2.0 ** -80

LANES = 128
BF16_SUBLANES = 16
MXU_DIM = 256
TOK_TILE = 512
FF_CHUNK = 256
ROW_SPLITS = 2
QK_AHEAD = 2
P_SLOTS = 4

A_Q_TILE = 256
A_TILES_PER_STEP = 2
A_LEFT = A_LEFT_CHUNKS * CHUNK
A_WIN = A_Q_TILE + A_LEFT
A_KEY_SPLITS = ((0, 384), (384, 768))
A_BIAS_ROWS = A_WIN + A_LEFT
A_BIAS_TILES = A_BIAS_ROWS // LANES
A_TAB_ZERO = A_BIAS_ROWS - A_LEFT + LANES
A_TAB_LEN = A_TAB_ZERO + A_LEFT + A_Q_TILE

HEAD_PAD = 128
ACC_ROWS = B_V_DIM + BF16_SUBLANES
FM_ROWS = 3 * A_WIDTH + B_Q_LORA
TM_COLS = B_KV_LORA + LANES
VMEM_LIMIT = 60 * 1024 * 1024

_BF = jnp.bfloat16
_F32 = jnp.float32


def _dot(a, b):
    return jnp.dot(a, b, preferred_element_type=_F32)


def _rms(x, g):
    ms = jnp.mean(x * x, axis=-1, keepdims=True)
    return x * lax.rsqrt(ms + EPS) * g


def _col_rms(x_t):
    return x_t * lax.rsqrt(jnp.mean(x_t * x_t, axis=0, keepdims=True) + EPS)


def _swiglu(h, wg_ref, wu_ref, wd_ref, act_ref):
    for c in range(D_FF // FF_CHUNK):
        sl = slice(c * FF_CHUNK, (c + 1) * FF_CHUNK)
        g = _dot(h, wg_ref[:, sl])
        u = _dot(h, wu_ref[:, sl])
        act_ref[:, sl] = (jax.nn.silu(g) * u).astype(_BF)
    return [_dot(act_ref[rows, :], wd_ref[...]) for rows in _row_blocks(h.shape[0])]


def _row_blocks(n):
    step = n // ROW_SPLITS
    return [slice(r, r + step) for r in range(0, n, step)]


def _pre_kernel(x_ref, g1_ref, wg_ref, wu_ref, wd_ref, gmix_ref, wfm_ref, wtm_ref,
                gaq_ref, gak_ref,
                wqt_ref, gqtab_ref, qpad_ref, cost_ref, sint_ref,
                gckv_ref, wvk_ref, gkn_ref,
                gkr_ref, kone_ref, ck_ref, sk_ref,
                x1_ref, qat_ref, ka_ref, vat_ref, qbt_ref, kb_ref, vbt_ref,
                act_ref):
    x = x_ref[...]
    tm = x.shape[0]
    h = _rms(x, g1_ref[...]).astype(_BF)
    ffn = _swiglu(h, wg_ref, wu_ref, wd_ref, act_ref)
    h2_blocks = []
    for rows, f in zip(_row_blocks(tm), ffn):
        x1 = x[rows] + 0.5 * f
        x1_ref[rows, :] = x1
        h2_blocks.append(_rms(x1, gmix_ref[...]))
    h2f = jnp.concatenate(h2_blocks, axis=0)
    h2 = h2f.astype(_BF)

    tmj = _dot(h2, wtm_ref[...])
    h2t = h2f.T.astype(_BF)
    ckvn_t = _rms(tmj[:, 0:B_KV_LORA], gckv_ref[...]).T.astype(_BF)
    cq_t = _dot(wfm_ref[3 * A_WIDTH:FM_ROWS, :], h2t)
    vk_t = _dot(wvk_ref[...], ckvn_t)
    q_t = _dot(wqt_ref[...], _col_rms(cq_t).astype(_BF))
    ka_fm = _dot(wfm_ref[2 * A_WIDTH:3 * A_WIDTH, :], h2t)
    qa_fm = _dot(wfm_ref[0:A_WIDTH, :], h2t)
    va_fm = _dot(wfm_ref[A_WIDTH:2 * A_WIDTH, :], h2t)

    gaq = gaq_ref[...]
    gak = gak_ref[...]
    ka_t = []
    for hd in range(A_HEADS):
        rows = slice(hd * A_HEAD_DIM, (hd + 1) * A_HEAD_DIM)
        qa_hd = (_col_rms(qa_fm[rows]) * gaq).astype(_BF)
        for g in range(tm // A_Q_TILE):
            cols = slice(g * A_Q_TILE, (g + 1) * A_Q_TILE)
            qat_ref[0, g, rows, :] = qa_hd[:, cols]
            vat_ref[0, hd, g] = va_fm[rows, cols].astype(_BF)
        ka_t.append(_col_rms(ka_fm[rows]) * gak)
    ka_ref[0] = jnp.concatenate(ka_t, axis=0).T.astype(_BF)

    cos_t = cost_ref[...]
    sin_t = sint_ref[...]
    gq = gqtab_ref[...]
    half = B_ROPE // 2
    for hd in range(B_HEADS):
        r0 = hd * HEAD_PAD
        nope = _col_rms(q_t[r0:r0 + B_NOPE]) * gq[0:B_NOPE]
        rope = _col_rms(q_t[r0 + B_NOPE:r0 + B_QK_DIM]) * gq[B_NOPE:B_QK_DIM]
        r1 = rope[0:half]
        r2 = rope[half:B_ROPE]
        blk = jnp.concatenate(
            [nope, r1 * cos_t - r2 * sin_t, r1 * sin_t + r2 * cos_t, qpad_ref[...]], axis=0)
        qbt_ref[0, hd] = blk.astype(_BF)

    gkn = gkn_ref[...]
    kn_t = []
    for hd in range(B_HEADS):
        vbt_ref[0, hd, 0] = vk_t[hd * B_V_DIM:(hd + 1) * B_V_DIM].astype(_BF)
        kn_t.append(_col_rms(vk_t[B_WIDTH + hd * B_NOPE:B_WIDTH + (hd + 1) * B_NOPE]) * gkn)
    kn = jnp.concatenate(kn_t, axis=0).T
    kr = tmj[:, B_KV_LORA:TM_COLS]
    y = kr * lax.rsqrt(jnp.sum(kr * kr, axis=-1, keepdims=True) * (1.0 / B_ROPE) + EPS) * gkr_ref[...]
    lane = lax.broadcasted_iota(jnp.int32, y.shape, 1)
    swapped = jnp.where(lane < B_NOPE + half,
                        pltpu.roll(y, LANES - half, axis=1), pltpu.roll(y, half, axis=1))
    kpe = y * ck_ref[...] + swapped * sk_ref[...] + kone_ref[...]
    for hd in range(B_HEADS):
        pair_tile = kn[:, (hd // 2) * LANES:(hd // 2 + 1) * LANES]
        if hd % 2 == 1:
            pair_tile = pltpu.roll(pair_tile, B_NOPE, axis=1)
        kb_ref[0, hd] = jnp.where(lane < B_NOPE, pair_tile, kpe).astype(_BF)


def _bias_kernel(tab_ref, o_ref):
    u = lax.broadcasted_iota(jnp.int32, (LANES, A_Q_TILE), 0)
    qc = lax.broadcasted_iota(jnp.int32, (LANES, A_Q_TILE), 1) // CHUNK
    for tile in range(A_BIAS_TILES):
        a = A_TAB_ZERO + A_LEFT - LANES * tile
        w = jnp.concatenate([tab_ref[0, :, a:a + A_Q_TILE], tab_ref[0, :, a - A_Q_TILE:a]], axis=1)
        rolled = pltpu.roll(jnp.broadcast_to(w, (LANES, 2 * A_Q_TILE)), 0, 1,
                            stride=1, stride_axis=0)
        kc = (LANES * tile + u) // CHUNK - A_LEFT_CHUNKS
        valid = (kc <= qc) & (kc >= qc - A_LEFT_CHUNKS)
        o_ref[0, tile] = jnp.where(valid, rolled[:, :A_Q_TILE], NEG_INF)


def _rel_bias_tiles(rel_bias, qk_bound):
    pad_lo = A_TAB_ZERO - A_MAX_REL
    pad_hi = A_TAB_LEN - pad_lo - (2 * A_MAX_REL + 1)
    rb = rel_bias.astype(_F32) * LOG2E
    rb = rb - (qk_bound + jnp.max(rb, axis=1, keepdims=True))
    table = jnp.concatenate([jnp.broadcast_to(rb[:, :1], (A_HEADS, pad_lo)), rb,
                             jnp.broadcast_to(rb[:, -1:], (A_HEADS, pad_hi))], axis=1)
    return pl.pallas_call(
        _bias_kernel,
        grid=(A_HEADS,),
        in_specs=[pl.BlockSpec((1, 1, A_TAB_LEN), lambda h: (h, 0, 0))],
        out_specs=pl.BlockSpec((1, A_BIAS_TILES, LANES, A_Q_TILE), lambda h: (h, 0, 0, 0)),
        out_shape=jax.ShapeDtypeStruct((A_HEADS, A_BIAS_TILES, LANES, A_Q_TILE), _F32),
        name="rel_bias",
    )(table.reshape(A_HEADS, 1, A_TAB_LEN))


def _attn_a_kernel(qt_ref, k_ref, vt_ref, bias_ref, o_ref):
    def one_tile(sub, carry):
        _attn_a_tile(pl.program_id(1) * A_TILES_PER_STEP + sub, sub,
                     qt_ref, k_ref, vt_ref, bias_ref, o_ref)
        return carry

    lax.fori_loop(0, A_TILES_PER_STEP, one_tile, 0)


def _attn_a_tile(i, sub, qt_ref, k_ref, vt_ref, bias_ref, o_ref):
    left_tiles = A_LEFT // A_Q_TILE
    g0 = jnp.maximum(i - left_tiles, 0)
    start = pl.multiple_of(g0 * A_Q_TILE, A_Q_TILE)
    u0 = (left_tiles - jnp.minimum(i, left_tiles)) * (A_Q_TILE // LANES)
    zeros = jnp.zeros((A_HEAD_DIM, A_Q_TILE), _BF)
    ones_rows = jnp.ones((ACC_ROWS - A_HEAD_DIM, A_WIN), _BF)

    def scores(pair):
        q2 = qt_ref[0, sub, pair * LANES:(pair + 1) * LANES, :]
        q_even = jnp.concatenate([q2[:A_HEAD_DIM], zeros], axis=0)
        q_odd = jnp.concatenate([zeros, q2[A_HEAD_DIM:]], axis=0)
        qq = jnp.concatenate([q_even, q_odd], axis=1)
        parts = []
        for r0, r1 in A_KEY_SPLITS:
            k = k_ref[0, pl.ds(pl.multiple_of(start + r0, LANES), r1 - r0),
                      pair * LANES:(pair + 1) * LANES]
            parts.append(_dot(k, qq))
        return jnp.concatenate(parts, axis=0)

    def attend(subtract_max):
        n_pairs = A_HEADS // 2
        pending = [scores(0)]
        outs, denoms = [], []
        for pair in range(n_pairs):
            s = pending.pop(0)
            if pair + 1 < n_pairs:
                pending.append(scores(pair + 1))
            bias = jnp.concatenate(
                [jnp.concatenate([bias_ref[2 * pair + e, u0 + j] for j in range(A_WIN // LANES)],
                                 axis=0) for e in range(2)], axis=1)
            s = s + bias
            if subtract_max:
                s = s - jnp.max(s, axis=0, keepdims=True)
            p = jnp.exp2(s).astype(_BF)
            for e in range(2):
                hd = 2 * pair + e
                v = jnp.concatenate([vt_ref[0, hd, g0 + g] for g in range(A_WIN // A_Q_TILE)],
                                    axis=1)
                o = _dot(jnp.concatenate([v, ones_rows], axis=0),
                         p[:, e * A_Q_TILE:(e + 1) * A_Q_TILE])
                denoms.append(o[A_HEAD_DIM:A_HEAD_DIM + 1])
                outs.append(o[:A_HEAD_DIM] / denoms[-1])
        o_ref[0, pl.ds(pl.multiple_of(sub * A_Q_TILE, A_Q_TILE), A_Q_TILE), :] = (
            jnp.concatenate(outs, axis=0).T.astype(_BF))
        return jnp.min(jnp.concatenate(denoms, axis=0))

    smallest = attend(subtract_max=False)

    @pl.when(jnp.logical_not(smallest > SAFE_DENOM))
    def _():
        attend(subtract_max=True)


def _attn_b_kernel(qt_ref, k_ref, vt_ref, o_ref, m_ref, l_ref, acc_ref, p_ref):
    i = pl.program_id(1)
    t = TOK_TILE

    def acc_rows(hd):
        return slice(hd * B_V_DIM, (hd + 1) * B_V_DIM)

    def diag_mask(s):
        krow = lax.broadcasted_iota(jnp.int32, (t, t), 0) // CHUNK
        qcol = lax.broadcasted_iota(jnp.int32, (t, t), 1) // CHUNK
        return jnp.where(krow <= qcol, s, NEG_INF)

    def sweep(kt, probs, commit):
        def scores(hd):
            k = k_ref[0, hd, pl.ds(pl.multiple_of(kt * t, t), t), :]
            return _dot(k, qt_ref[0, hd])

        pending = [scores(hd) for hd in range(QK_AHEAD)]
        for hd in range(B_HEADS):
            s = pending.pop(0)
            if hd + QK_AHEAD < B_HEADS:
                pending.append(scores(hd + QK_AHEAD))
            p = probs(hd, s)
            slot = hd % P_SLOTS
            p_ref[slot] = p.astype(_BF)
            commit(hd, _dot(vt_ref[0, hd, kt], p_ref[slot]), jnp.sum(p, axis=0, keepdims=True))

    def assign(hd, pv, ps):
        acc_ref[acc_rows(hd), :] = pv
        l_ref[hd:hd + 1, :] = ps

    def add(hd, pv, ps):
        acc_ref[acc_rows(hd), :] += pv
        l_ref[hd:hd + 1, :] += ps

    sweep(i, lambda hd, s: jnp.exp2(diag_mask(s)), assign)
    safe = jnp.min(l_ref[...]) > SAFE_DENOM

    @pl.when(safe)
    def _():
        def full_tile(kt, carry):
            sweep(kt, lambda hd, s: jnp.exp2(s), add)
            return carry

        lax.fori_loop(0, i, full_tile, 0)

    @pl.when(jnp.logical_not(safe))
    def _():
        m_ref[...] = jnp.full(m_ref.shape, NEG_INF, _F32)
        l_ref[...] = jnp.zeros(l_ref.shape, _F32)
        acc_ref[...] = jnp.zeros(acc_ref.shape, _F32)

        def online_step(kt, masked):
            alphas = {}

            def probs(hd, s):
                if masked:
                    s = diag_mask(s)
                m = m_ref[hd:hd + 1, :]
                m_new = jnp.maximum(m, jnp.max(s, axis=0, keepdims=True))
                m_ref[hd:hd + 1, :] = m_new
                alphas[hd] = jnp.exp2(m - m_new)
                return jnp.exp2(s - m_new)

            def rescale_add(hd, pv, ps):
                acc_ref[acc_rows(hd), :] = alphas[hd] * acc_ref[acc_rows(hd), :] + pv
                l_ref[hd:hd + 1, :] = alphas[hd] * l_ref[hd:hd + 1, :] + ps

            sweep(kt, probs, rescale_add)

        def online_full(kt, carry):
            online_step(kt, masked=False)
            return carry

        lax.fori_loop(0, i, online_full, 0)
        online_step(i, masked=True)

    outs = [acc_ref[acc_rows(hd), :] / l_ref[hd:hd + 1, :] for hd in range(B_HEADS)]
    o_ref[0] = jnp.concatenate(outs, axis=0).T.astype(_BF)


def _post_kernel(x1_ref, oa_ref, ob_ref, wout_ref, g2_ref, wg_ref, wu_ref, wd_ref,
                 gf_ref, y_ref, act_ref):
    blocks = _row_blocks(x1_ref.shape[0])
    x2 = [x1_ref[rows, :] + _dot(oa_ref[rows, :], wout_ref[0:A_WIDTH, :])
          + _dot(ob_ref[rows, :], wout_ref[A_WIDTH:A_WIDTH + B_WIDTH, :]) for rows in blocks]
    h = jnp.concatenate([_rms(v, g2_ref[...]) for v in x2], axis=0).astype(_BF)
    ffn = _swiglu(h, wg_ref, wu_ref, wd_ref, act_ref)
    for rows, v, f in zip(blocks, x2, ffn):
        y_ref[rows, :] = _rms(v + 0.5 * f, gf_ref[...])


def _rope_tables(s):
    half = B_ROPE // 2
    inv = (1.0 / (ROPE_THETA ** (np.arange(0, B_ROPE, 2, dtype=np.float32) / B_ROPE))).astype(np.float32)
    ang = np.arange(s, dtype=np.float32)[:, None] * inv[None, :]
    cos, sin = np.cos(ang).astype(np.float32), np.sin(ang).astype(np.float32)
    ck = np.zeros((s, LANES), np.float32)
    sk = np.zeros((s, LANES), np.float32)
    ck[:, B_NOPE:B_NOPE + half] = cos
    ck[:, B_NOPE + half:B_QK_DIM] = cos
    sk[:, B_NOPE:B_NOPE + half] = -sin
    sk[:, B_NOPE + half:B_QK_DIM] = sin
    return (jnp.asarray(np.ascontiguousarray(cos.T)), jnp.asarray(np.ascontiguousarray(sin.T)),
            jnp.asarray(ck), jnp.asarray(sk))


def _const_spec(shape):
    n = len(shape)
    return pl.BlockSpec(shape, lambda *_: (0,) * n, pipeline_mode=pl.Buffered(1))


def kernel(x, ffn1_norm, ffn1_w_gate, ffn1_w_up, ffn1_w_down, mix_norm, w_in, a_q_norm, a_k_norm, a_rel_bias, b_q_lat_norm, b_w_uq, b_kv_lat_norm, b_w_ukv, b_q_nope_norm, b_q_rope_norm, b_k_nope_norm, b_k_rope_norm, w_out, ffn2_norm, ffn2_w_gate, ffn2_w_up, ffn2_w_down, final_norm):
    b, s, d = x.shape
    assert d == D_MODEL and s % TOK_TILE == 0 and s >= A_WIN and ffn1_norm.shape[0] == 1
    tm = TOK_TILE
    nt = s // tm
    half = B_ROPE // 2

    def row(v):
        return v.astype(_F32)[None, :]

    w_in_l = w_in[0]
    o_va, o_cq = 2 * A_WIDTH, 3 * A_WIDTH
    o_ckv = o_cq + B_Q_LORA
    o_kr = o_ckv + B_KV_LORA
    w_fm = jnp.concatenate([w_in_l[:, 0:A_WIDTH], w_in_l[:, o_va:o_cq], w_in_l[:, A_WIDTH:o_va],
                            w_in_l[:, o_cq:o_ckv]], axis=1).T.astype(_BF)
    kr_cols = jnp.zeros((d, LANES), _F32).at[:, B_NOPE:B_QK_DIM].set(w_in_l[:, o_kr:o_kr + B_ROPE])
    w_tm = jnp.concatenate([w_in_l[:, o_ckv:o_kr], kr_cols], axis=1).astype(_BF)

    def col_tab(v):
        return jnp.broadcast_to(v.astype(_F32)[:, None], (v.shape[0], tm))

    gaq_tab = col_tab(a_q_norm[0] * (A_HEAD_DIM ** -0.5 * LOG2E))
    gak_tab = col_tab(a_k_norm[0])

    w_uq = (b_w_uq[0] * b_q_lat_norm[0][:, None]).reshape(B_Q_LORA, B_HEADS, B_QK_DIM)
    w_uq = jnp.pad(w_uq, ((0, 0), (0, 0), (0, HEAD_PAD - B_QK_DIM)))
    wq_t = w_uq.reshape(B_Q_LORA, B_HEADS * HEAD_PAD).T.astype(_BF)
    scale_b = (B_QK_DIM ** -0.5) * LOG2E
    gq_col = jnp.concatenate([b_q_nope_norm[0] * scale_b, b_q_rope_norm[0] * scale_b,
                              jnp.zeros((HEAD_PAD - B_QK_DIM,), _F32)])
    gq_tab = jnp.broadcast_to(gq_col[:, None], (HEAD_PAD, tm)).astype(_F32)
    q_norm2 = B_NOPE * jnp.max(b_q_nope_norm[0] ** 2) + B_ROPE * jnp.max(b_q_rope_norm[0] ** 2)
    k_norm2 = B_NOPE * jnp.max(b_k_nope_norm[0] ** 2) + B_ROPE * jnp.max(b_k_rope_norm[0] ** 2)
    bound_b = (scale_b * jnp.sqrt(q_norm2 * k_norm2) * BOUND_SLACK).astype(_BF).astype(_F32)
    qpad_tab = jnp.zeros((HEAD_PAD - B_QK_DIM, tm), _F32).at[0, :].set(-bound_b)
    kone = jnp.zeros((1, LANES), _F32).at[0, B_QK_DIM].set(1.0)

    cos_t, sin_t, ck, sk = _rope_tables(s)

    w_ukv = b_w_ukv[0].reshape(B_KV_LORA, B_HEADS, B_NOPE + B_V_DIM)
    wvk_t = jnp.concatenate([w_ukv[..., B_NOPE:].reshape(B_KV_LORA, B_WIDTH),
                             w_ukv[..., :B_NOPE].reshape(B_KV_LORA, B_HEADS * B_NOPE)],
                            axis=1).T.astype(_BF)
    gkn_tab = col_tab(b_k_nope_norm[0])
    gkr = row(jnp.concatenate([jnp.zeros((B_NOPE,), _F32), b_k_rope_norm[0],
                               jnp.zeros((LANES - B_QK_DIM,), _F32)]))

    bound_a = ((A_HEAD_DIM ** -0.5 * LOG2E) * A_HEAD_DIM * jnp.max(jnp.abs(a_q_norm[0]))
               * jnp.max(jnp.abs(a_k_norm[0])) * BOUND_SLACK)
    bias_tiles = _rel_bias_tiles(a_rel_bias[0], bound_a)

    tok = lambda bi, j: (bi, j, 0)
    pre_in_specs = [
        pl.BlockSpec((None, tm, d), tok),
        _const_spec((1, d)), _const_spec((d, D_FF)), _const_spec((d, D_FF)), _const_spec((D_FF, d)),
        _const_spec((1, d)), _const_spec((FM_ROWS, d)), _const_spec((d, TM_COLS)),
        _const_spec((A_HEAD_DIM, tm)), _const_spec((A_HEAD_DIM, tm)),
        _const_spec((B_HEADS * HEAD_PAD, B_Q_LORA)), _const_spec((HEAD_PAD, tm)),
        _const_spec((HEAD_PAD - B_QK_DIM, tm)),
        pl.BlockSpec((half, tm), lambda bi, j: (0, j)), pl.BlockSpec((half, tm), lambda bi, j: (0, j)),
        _const_spec((1, B_KV_LORA)), _const_spec((B_WIDTH + B_HEADS * B_NOPE, B_KV_LORA)),
        _const_spec((B_NOPE, tm)),
        _const_spec((1, LANES)), _const_spec((1, LANES)),
        pl.BlockSpec((tm, LANES), lambda bi, j: (j, 0)), pl.BlockSpec((tm, LANES), lambda bi, j: (j, 0)),
    ]
    pre_out_shapes = [
        jax.ShapeDtypeStruct((b, s, d), _F32),
        jax.ShapeDtypeStruct((b, s // A_Q_TILE, A_WIDTH, A_Q_TILE), _BF),
        jax.ShapeDtypeStruct((b, s, A_WIDTH), _BF),
        jax.ShapeDtypeStruct((b, A_HEADS, s // A_Q_TILE, A_HEAD_DIM, A_Q_TILE), _BF),
        jax.ShapeDtypeStruct((b, B_HEADS, HEAD_PAD, s), _BF),
        jax.ShapeDtypeStruct((b, B_HEADS, s, HEAD_PAD), _BF),
        jax.ShapeDtypeStruct((b, B_HEADS, nt, B_V_DIM, tm), _BF),
    ]
    pre_out_specs = [
        pl.BlockSpec((None, tm, d), tok),
        pl.BlockSpec((1, tm // A_Q_TILE, A_WIDTH, A_Q_TILE), lambda bi, j: (bi, j, 0, 0)),
        pl.BlockSpec((1, tm, A_WIDTH), tok),
        pl.BlockSpec((1, A_HEADS, tm // A_Q_TILE, A_HEAD_DIM, A_Q_TILE),
                     lambda bi, j: (bi, 0, j, 0, 0)),
        pl.BlockSpec((1, B_HEADS, HEAD_PAD, tm), lambda bi, j: (bi, 0, 0, j)),
        pl.BlockSpec((1, B_HEADS, tm, HEAD_PAD), lambda bi, j: (bi, 0, j, 0)),
        pl.BlockSpec((1, B_HEADS, 1, B_V_DIM, tm), lambda bi, j: (bi, 0, j, 0, 0)),
    ]
    x1, qat, ka, vat, qbt, kb, vbt = pl.pallas_call(
        _pre_kernel,
        grid=(b, nt),
        in_specs=pre_in_specs,
        out_specs=pre_out_specs,
        out_shape=pre_out_shapes,
        scratch_shapes=[pltpu.VMEM((tm, D_FF), _BF)],
        compiler_params=pltpu.CompilerParams(
            dimension_semantics=("arbitrary", "arbitrary"), vmem_limit_bytes=VMEM_LIMIT),
        name="pre",
    )(x, row(ffn1_norm[0]), ffn1_w_gate[0].astype(_BF), ffn1_w_up[0].astype(_BF),
      ffn1_w_down[0].astype(_BF), row(mix_norm[0]), w_fm, w_tm,
      gaq_tab, gak_tab,
      wq_t, gq_tab, qpad_tab, cos_t, sin_t,
      row(b_kv_lat_norm[0]), wvk_t, gkn_tab,
      gkr, kone, ck, sk)

    na = s // A_Q_TILE
    out_a = pl.pallas_call(
        _attn_a_kernel,
        grid=(b, na // A_TILES_PER_STEP),
        in_specs=[
            pl.BlockSpec((1, A_TILES_PER_STEP, A_WIDTH, A_Q_TILE), lambda bi, j: (bi, j, 0, 0)),
            pl.BlockSpec((1, s, A_WIDTH), lambda bi, j: (bi, 0, 0)),
            pl.BlockSpec((1, A_HEADS, na, A_HEAD_DIM, A_Q_TILE), lambda bi, j: (bi, 0, 0, 0, 0)),
            _const_spec((A_HEADS, A_BIAS_TILES, LANES, A_Q_TILE)),
        ],
        out_specs=pl.BlockSpec((1, A_TILES_PER_STEP * A_Q_TILE, A_WIDTH), tok),
        out_shape=jax.ShapeDtypeStruct((b, s, A_WIDTH), _BF),
        compiler_params=pltpu.CompilerParams(
            dimension_semantics=("arbitrary", "arbitrary"), vmem_limit_bytes=VMEM_LIMIT),
        name="attn_a",
    )(qat, ka, vat, bias_tiles)

    out_b = pl.pallas_call(
        _attn_b_kernel,
        grid=(b, nt),
        in_specs=[
            pl.BlockSpec((1, B_HEADS, HEAD_PAD, tm), lambda bi, j: (bi, 0, 0, j)),
            pl.BlockSpec((1, B_HEADS, s, HEAD_PAD), lambda bi, j: (bi, 0, 0, 0)),
            pl.BlockSpec((1, B_HEADS, nt, B_V_DIM, tm), lambda bi, j: (bi, 0, 0, 0, 0)),
        ],
        out_specs=pl.BlockSpec((1, tm, B_WIDTH), tok),
        out_shape=jax.ShapeDtypeStruct((b, s, B_WIDTH), _BF),
        scratch_shapes=[pltpu.VMEM((B_HEADS, tm), _F32), pltpu.VMEM((B_HEADS, tm), _F32),
                        pltpu.VMEM((B_WIDTH, tm), _F32), pltpu.VMEM((P_SLOTS, tm, tm), _BF)],
        compiler_params=pltpu.CompilerParams(
            dimension_semantics=("arbitrary", "arbitrary"), vmem_limit_bytes=VMEM_LIMIT),
        name="attn_b",
    )(qbt, kb, vbt)

    n = b * s
    flat = lambda j: (j, 0)
    y = pl.pallas_call(
        _post_kernel,
        grid=(n // tm,),
        in_specs=[
            pl.BlockSpec((tm, d), flat),
            pl.BlockSpec((tm, A_WIDTH), flat),
            pl.BlockSpec((tm, B_WIDTH), flat),
            _const_spec((A_WIDTH + B_WIDTH, d)),
            _const_spec((1, d)), _const_spec((d, D_FF)), _const_spec((d, D_FF)), _const_spec((D_FF, d)),
            _const_spec((1, d)),
        ],
        out_specs=pl.BlockSpec((tm, d), flat),
        out_shape=jax.ShapeDtypeStruct((n, d), _F32),
        scratch_shapes=[pltpu.VMEM((tm, D_FF), _BF)],
        compiler_params=pltpu.CompilerParams(
            dimension_semantics=("arbitrary",), vmem_limit_bytes=VMEM_LIMIT),
        name="post",
    )(x1.reshape(n, d), out_a.reshape(n, A_WIDTH), out_b.reshape(n, B_WIDTH),
      w_out[0].astype(_BF), row(ffn2_norm[0]), ffn2_w_gate[0].astype(_BF),
      ffn2_w_up[0].astype(_BF), ffn2_w_down[0].astype(_BF), row(final_norm[0]))
    return y.reshape(b, s, d)
```

```python
import jax
import jax.numpy as jnp
import numpy as np
from jax import lax
from jax.experimental import pallas as pl
from jax.experimental.pallas import tpu as pltpu

D_MODEL = 1024
D_FF = 2816
CHUNK = 64
A_HEADS = 8
A_HEAD_DIM = 64
A_LEFT_CHUNKS = 8
A_MAX_REL = 128
A_WIDTH = A_HEADS * A_HEAD_DIM
B_HEADS = 8
B_Q_LORA = 256
B_KV_LORA = 128
B_NOPE = 64
B_ROPE = 32
B_V_DIM = 64
B_QK_DIM = B_NOPE + B_ROPE
B_WIDTH = B_HEADS * B_V_DIM
ROPE_THETA = 10000.0
EPS = 1e-6
NEG_INF = -1e30
LOG2E = 1.4426950408889634
BOUND_SLACK = 1.0 + 2.0 ** -5
SAFE_DENOM = 2.0 ** -80

LANES = 128
BF16_SUBLANES = 16
TOK_TILE = 512
POST_TILE = 1024
FF_CHUNK = 256
QK_AHEAD = 2

A_Q_TILE = 256
A_TILES_PER_STEP = 2
A_LEFT = A_LEFT_CHUNKS * CHUNK
A_WIN = A_Q_TILE + A_LEFT
A_KEY_SPLITS = ((0, 384), (384, 768))
A_BIAS_ROWS = A_WIN + A_LEFT
A_BIAS_TILES = A_BIAS_ROWS // LANES
A_TAB_ZERO = A_BIAS_ROWS - A_LEFT + LANES
A_TAB_LEN = A_TAB_ZERO + A_LEFT + A_Q_TILE

HEAD_PAD = 128
ACC_ROWS = A_HEAD_DIM + BF16_SUBLANES
FM_ROWS = 3 * A_WIDTH + B_Q_LORA
TM_COLS = B_KV_LORA + LANES
VMEM_LIMIT = 60 * 1024 * 1024

_BF = jnp.bfloat16
_F32 = jnp.float32


def _dot(a, b):
    return jnp.dot(a, b, preferred_element_type=_F32)


def _rms(x, g):
    ms = jnp.mean(x * x, axis=-1, keepdims=True)
    return x * lax.rsqrt(ms + EPS) * g


def _col_rms(x_t):
    return x_t * lax.rsqrt(jnp.mean(x_t * x_t, axis=0, keepdims=True) + EPS)


def _swiglu(h, wg_ref, wu_ref, wd_ref, act_ref):
    for c in range(D_FF // FF_CHUNK):
        sl = slice(c * FF_CHUNK, (c + 1) * FF_CHUNK)
        g = _dot(h, wg_ref[:, sl])
        u = _dot(h, wu_ref[:, sl])
        act_ref[:, sl] = (jax.nn.silu(g) * u).astype(_BF)
    return _dot(act_ref[...], wd_ref[...])


def _pre_kernel(x_ref, g1_ref, wg_ref, wu_ref, wd_ref, gmix_ref, wfm_ref, wtm_ref,
                gaq_ref, gak_ref,
                wqt_ref, gqtab_ref, qpad_ref, cost_ref, sint_ref,
                gckv_ref, wvk_ref, gkn_ref,
                gkr_ref, kone_ref, ck_ref, sk_ref,
                x1_ref, qat_ref, ka_ref, vat_ref, qbt_ref, kb_ref, vbt_ref,
                act_ref):
    x = x_ref[...]
    tm = x.shape[0]
    h = _rms(x, g1_ref[...]).astype(_BF)
    x1 = x + 0.5 * _swiglu(h, wg_ref, wu_ref, wd_ref, act_ref)
    x1_ref[...] = x1
    h2f = _rms(x1, gmix_ref[...])
    h2 = h2f.astype(_BF)

    tmj = _dot(h2, wtm_ref[...])
    h2t = h2f.T.astype(_BF)
    ckvn_t = _rms(tmj[:, 0:B_KV_LORA], gckv_ref[...]).T.astype(_BF)
    cq_t = _dot(wfm_ref[3 * A_WIDTH:FM_ROWS, :], h2t)
    vk_t = _dot(wvk_ref[...], ckvn_t)
    q_t = _dot(wqt_ref[...], _col_rms(cq_t).astype(_BF))
    ka_fm = _dot(wfm_ref[2 * A_WIDTH:3 * A_WIDTH, :], h2t)
    qa_fm = _dot(wfm_ref[0:A_WIDTH, :], h2t)
    va_fm = _dot(wfm_ref[A_WIDTH:2 * A_WIDTH, :], h2t)

    gaq = gaq_ref[...]
    gak = gak_ref[...]
    ka_t = []
    for hd in range(A_HEADS):
        rows = slice(hd * A_HEAD_DIM, (hd + 1) * A_HEAD_DIM)
        qa_hd = (_col_rms(qa_fm[rows]) * gaq).astype(_BF)
        for g in range(tm // A_Q_TILE):
            cols = slice(g * A_Q_TILE, (g + 1) * A_Q_TILE)
            qat_ref[0, g, rows, :] = qa_hd[:, cols]
            vat_ref[0, hd, g] = va_fm[rows, cols].astype(_BF)
        ka_t.append(_col_rms(ka_fm[rows]) * gak)
    ka_ref[0] = jnp.concatenate(ka_t, axis=0).T.astype(_BF)

    cos_t = cost_ref[...]
    sin_t = sint_ref[...]
    gq = gqtab_ref[...]
    half = B_ROPE // 2
    for hd in range(B_HEADS):
        r0 = hd * HEAD_PAD
        nope = _col_rms(q_t[r0:r0 + B_NOPE]) * gq[0:B_NOPE]
        rope = _col_rms(q_t[r0 + B_NOPE:r0 + B_QK_DIM]) * gq[B_NOPE:B_QK_DIM]
        r1 = rope[0:half]
        r2 = rope[half:B_ROPE]
        blk = jnp.concatenate(
            [nope, r1 * cos_t - r2 * sin_t, r1 * sin_t + r2 * cos_t, qpad_ref[...]], axis=0)
        qbt_ref[0, hd] = blk.astype(_BF)

    gkn = gkn_ref[...]
    kn_t = []
    for hd in range(B_HEADS):
        vbt_ref[0, hd, 0] = vk_t[hd * B_V_DIM:(hd + 1) * B_V_DIM].astype(_BF)
        kn_t.append(_col_rms(vk_t[B_WIDTH + hd * B_NOPE:B_WIDTH + (hd + 1) * B_NOPE]) * gkn)
    kn = jnp.concatenate(kn_t, axis=0).T
    kr = tmj[:, B_KV_LORA:TM_COLS]
    y = kr * lax.rsqrt(jnp.sum(kr * kr, axis=-1, keepdims=True) * (1.0 / B_ROPE) + EPS) * gkr_ref[...]
    lane = lax.broadcasted_iota(jnp.int32, y.shape, 1)
    swapped = jnp.where(lane < B_NOPE + half,
                        pltpu.roll(y, LANES - half, axis=1), pltpu.roll(y, half, axis=1))
    kpe = y * ck_ref[...] + swapped * sk_ref[...] + kone_ref[...]
    for hd in range(B_HEADS):
        pair_tile = kn[:, (hd // 2) * LANES:(hd // 2 + 1) * LANES]
        if hd % 2 == 1:
            pair_tile = pltpu.roll(pair_tile, B_NOPE, axis=1)
        kb_ref[0, hd] = jnp.where(lane < B_NOPE, pair_tile, kpe).astype(_BF)


def _bias_kernel(tab_ref, o_ref):
    u = lax.broadcasted_iota(jnp.int32, (LANES, A_Q_TILE), 0)
    qc = lax.broadcasted_iota(jnp.int32, (LANES, A_Q_TILE), 1) // CHUNK
    for tile in range(A_BIAS_TILES):
        a = A_TAB_ZERO + A_LEFT - LANES * tile
        w = jnp.concatenate([tab_ref[0, :, a:a + A_Q_TILE], tab_ref[0, :, a - A_Q_TILE:a]], axis=1)
        rolled = pltpu.roll(jnp.broadcast_to(w, (LANES, 2 * A_Q_TILE)), 0, 1,
                            stride=1, stride_axis=0)
        kc = (LANES * tile + u) // CHUNK - A_LEFT_CHUNKS
        valid = (kc <= qc) & (kc >= qc - A_LEFT_CHUNKS)
        o_ref[0, tile] = jnp.where(valid, rolled[:, :A_Q_TILE], NEG_INF)


def _rel_bias_tiles(rel_bias, qk_bound):
    pad_lo = A_TAB_ZERO - A_MAX_REL
    pad_hi = A_TAB_LEN - pad_lo - (2 * A_MAX_REL + 1)
    rb = rel_bias.astype(_F32) * LOG2E
    rb = rb - (qk_bound + jnp.max(rb, axis=1, keepdims=True))
    table = jnp.concatenate([jnp.broadcast_to(rb[:, :1], (A_HEADS, pad_lo)), rb,
                             jnp.broadcast_to(rb[:, -1:], (A_HEADS, pad_hi))], axis=1)
    return pl.pallas_call(
        _bias_kernel,
        grid=(A_HEADS,),
        in_specs=[pl.BlockSpec((1, 1, A_TAB_LEN), lambda h: (h, 0, 0))],
        out_specs=pl.BlockSpec((1, A_BIAS_TILES, LANES, A_Q_TILE), lambda h: (h, 0, 0, 0)),
        out_shape=jax.ShapeDtypeStruct((A_HEADS, A_BIAS_TILES, LANES, A_Q_TILE), _F32),
        name="rel_bias",
    )(table.reshape(A_HEADS, 1, A_TAB_LEN))


def _attn_a_kernel(qt_ref, k_ref, vt_ref, bias_ref, o_ref):
    def one_tile(sub, carry):
        _attn_a_tile(pl.program_id(1) * A_TILES_PER_STEP + sub, sub,
                     qt_ref, k_ref, vt_ref, bias_ref, o_ref)
        return carry

    lax.fori_loop(0, A_TILES_PER_STEP, one_tile, 0)


def _attn_a_tile(i, sub, qt_ref, k_ref, vt_ref, bias_ref, o_ref):
    left_tiles = A_LEFT // A_Q_TILE
    g0 = jnp.maximum(i - left_tiles, 0)
    start = pl.multiple_of(g0 * A_Q_TILE, A_Q_TILE)
    u0 = (left_tiles - jnp.minimum(i, left_tiles)) * (A_Q_TILE // LANES)
    zeros = jnp.zeros((A_HEAD_DIM, A_Q_TILE), _BF)
    ones_rows = jnp.ones((ACC_ROWS - A_HEAD_DIM, A_WIN), _BF)

    def scores(pair):
        q2 = qt_ref[0, sub, pair * LANES:(pair + 1) * LANES, :]
        q_even = jnp.concatenate([q2[:A_HEAD_DIM], zeros], axis=0)
        q_odd = jnp.concatenate([zeros, q2[A_HEAD_DIM:]], axis=0)
        qq = jnp.concatenate([q_even, q_odd], axis=1)
        parts = []
        for r0, r1 in A_KEY_SPLITS:
            k = k_ref[0, pl.ds(pl.multiple_of(start + r0, LANES), r1 - r0),
                      pair * LANES:(pair + 1) * LANES]
            parts.append(_dot(k, qq))
        return jnp.concatenate(parts, axis=0)

    def attend(subtract_max):
        n_pairs = A_HEADS // 2
        pending = [scores(0)]
        outs, denoms = [], []
        for pair in range(n_pairs):
            s = pending.pop(0)
            if pair + 1 < n_pairs:
                pending.append(scores(pair + 1))
            bias = jnp.concatenate(
                [jnp.concatenate([bias_ref[2 * pair + e, u0 + j] for j in range(A_WIN // LANES)],
                                 axis=0) for e in range(2)], axis=1)
            s = s + bias
            if subtract_max:
                s = s - jnp.max(s, axis=0, keepdims=True)
            p = jnp.exp2(s).astype(_BF)
            for e in range(2):
                hd = 2 * pair + e
                v = jnp.concatenate([vt_ref[0, hd, g0 + g] for g in range(A_WIN // A_Q_TILE)],
                                    axis=1)
                o = _dot(jnp.concatenate([v, ones_rows], axis=0),
                         p[:, e * A_Q_TILE:(e + 1) * A_Q_TILE])
                denoms.append(o[A_HEAD_DIM:A_HEAD_DIM + 1])
                outs.append(o[:A_HEAD_DIM] / denoms[-1])
        o_ref[0, pl.ds(pl.multiple_of(sub * A_Q_TILE, A_Q_TILE), A_Q_TILE), :] = (
            jnp.concatenate(outs, axis=0).T.astype(_BF))
        return jnp.min(jnp.concatenate(denoms, axis=0))

    smallest = attend(subtract_max=False)

    @pl.when(jnp.logical_not(smallest > SAFE_DENOM))
    def _():
        attend(subtract_max=True)


def _attn_b_kernel(qt_ref, k_ref, vt_ref, o_ref, m_ref, l_ref, acc_ref):
    i = pl.program_id(1)
    t = TOK_TILE

    def acc_rows(hd):
        return slice(hd * B_V_DIM, (hd + 1) * B_V_DIM)

    def diag_mask(s):
        krow = lax.broadcasted_iota(jnp.int32, (t, t), 0) // CHUNK
        qcol = lax.broadcasted_iota(jnp.int32, (t, t), 1) // CHUNK
        return jnp.where(krow <= qcol, s, NEG_INF)

    def sweep(kt, probs, commit):
        def scores(hd):
            k = k_ref[0, hd, pl.ds(pl.multiple_of(kt * t, t), t), :]
            return _dot(k, qt_ref[0, hd])

        pending = [scores(hd) for hd in range(QK_AHEAD)]
        for hd in range(B_HEADS):
            s = pending.pop(0)
            if hd + QK_AHEAD < B_HEADS:
                pending.append(scores(hd + QK_AHEAD))
            p = probs(hd, s)
            commit(hd, _dot(vt_ref[0, hd, kt], p.astype(_BF)), jnp.sum(p, axis=0, keepdims=True))

    def assign(hd, pv, ps):
        acc_ref[acc_rows(hd), :] = pv
        l_ref[hd:hd + 1, :] = ps

    def add(hd, pv, ps):
        acc_ref[acc_rows(hd), :] += pv
        l_ref[hd:hd + 1, :] += ps

    sweep(i, lambda hd, s: jnp.exp2(diag_mask(s)), assign)
    safe = jnp.min(l_ref[...]) > SAFE_DENOM

    @pl.when(safe)
    def _():
        def full_tile(kt, carry):
            sweep(kt, lambda hd, s: jnp.exp2(s), add)
            return carry

        lax.fori_loop(0, i, full_tile, 0)

    @pl.when(jnp.logical_not(safe))
    def _():
        m_ref[...] = jnp.full(m_ref.shape, NEG_INF, _F32)
        l_ref[...] = jnp.zeros(l_ref.shape, _F32)
        acc_ref[...] = jnp.zeros(acc_ref.shape, _F32)

        def online_step(kt, masked):
            alphas = {}

            def probs(hd, s):
                if masked:
                    s = diag_mask(s)
                m = m_ref[hd:hd + 1, :]
                m_new = jnp.maximum(m, jnp.max(s, axis=0, keepdims=True))
                m_ref[hd:hd + 1, :] = m_new
                alphas[hd] = jnp.exp2(m - m_new)
                return jnp.exp2(s - m_new)

            def rescale_add(hd, pv, ps):
                acc_ref[acc_rows(hd), :] = alphas[hd] * acc_ref[acc_rows(hd), :] + pv
                l_ref[hd:hd + 1, :] = alphas[hd] * l_ref[hd:hd + 1, :] + ps

            sweep(kt, probs, rescale_add)

        def online_full(kt, carry):
            online_step(kt, masked=False)
            return carry

        lax.fori_loop(0, i, online_full, 0)
        online_step(i, masked=True)

    outs = [acc_ref[acc_rows(hd), :] / l_ref[hd:hd + 1, :] for hd in range(B_HEADS)]
    o_ref[0] = jnp.concatenate(outs, axis=0).T.astype(_BF)


def _post_kernel(x1_ref, oa_ref, ob_ref, wout_ref, g2_ref, wg_ref, wu_ref, wd_ref,
                 gf_ref, y_ref, act_ref):
    x2 = (x1_ref[...] + _dot(oa_ref[...], wout_ref[0:A_WIDTH, :])
          + _dot(ob_ref[...], wout_ref[A_WIDTH:A_WIDTH + B_WIDTH, :]))
    h = _rms(x2, g2_ref[...]).astype(_BF)
    x3 = x2 + 0.5 * _swiglu(h, wg_ref, wu_ref, wd_ref, act_ref)
    y_ref[...] = _rms(x3, gf_ref[...])


def _rope_tables(s):
    half = B_ROPE // 2
    inv = (1.0 / (ROPE_THETA ** (np.arange(0, B_ROPE, 2, dtype=np.float32) / B_ROPE))).astype(np.float32)
    ang = np.arange(s, dtype=np.float32)[:, None] * inv[None, :]
    cos, sin = np.cos(ang).astype(np.float32), np.sin(ang).astype(np.float32)
    ck = np.zeros((s, LANES), np.float32)
    sk = np.zeros((s, LANES), np.float32)
    ck[:, B_NOPE:B_NOPE + half] = cos
    ck[:, B_NOPE + half:B_QK_DIM] = cos
    sk[:, B_NOPE:B_NOPE + half] = -sin
    sk[:, B_NOPE + half:B_QK_DIM] = sin
    return (jnp.asarray(np.ascontiguousarray(cos.T)), jnp.asarray(np.ascontiguousarray(sin.T)),
            jnp.asarray(ck), jnp.asarray(sk))


def _const_spec(shape):
    n = len(shape)
    return pl.BlockSpec(shape, lambda *_: (0,) * n, pipeline_mode=pl.Buffered(1))


def kernel(x, ffn1_norm, ffn1_w_gate, ffn1_w_up, ffn1_w_down, mix_norm, w_in, a_q_norm, a_k_norm, a_rel_bias, b_q_lat_norm, b_w_uq, b_kv_lat_norm, b_w_ukv, b_q_nope_norm, b_q_rope_norm, b_k_nope_norm, b_k_rope_norm, w_out, ffn2_norm, ffn2_w_gate, ffn2_w_up, ffn2_w_down, final_norm):
    b, s, d = x.shape
    assert d == D_MODEL and s % TOK_TILE == 0 and s >= A_WIN and ffn1_norm.shape[0] == 1
    tm = TOK_TILE
    nt = s // tm
    half = B_ROPE // 2

    def row(v):
        return v.astype(_F32)[None, :]

    w_in_l = w_in[0]
    o_va, o_cq = 2 * A_WIDTH, 3 * A_WIDTH
    o_ckv = o_cq + B_Q_LORA
    o_kr = o_ckv + B_KV_LORA
    w_fm = jnp.concatenate([w_in_l[:, 0:A_WIDTH], w_in_l[:, o_va:o_cq], w_in_l[:, A_WIDTH:o_va],
                            w_in_l[:, o_cq:o_ckv]], axis=1).T.astype(_BF)
    kr_cols = jnp.zeros((d, LANES), _F32).at[:, B_NOPE:B_QK_DIM].set(w_in_l[:, o_kr:o_kr + B_ROPE])
    w_tm = jnp.concatenate([w_in_l[:, o_ckv:o_kr], kr_cols], axis=1).astype(_BF)

    def col_tab(v):
        return jnp.broadcast_to(v.astype(_F32)[:, None], (v.shape[0], tm))

    gaq_tab = col_tab(a_q_norm[0] * (A_HEAD_DIM ** -0.5 * LOG2E))
    gak_tab = col_tab(a_k_norm[0])

    w_uq = (b_w_uq[0] * b_q_lat_norm[0][:, None]).reshape(B_Q_LORA, B_HEADS, B_QK_DIM)
    w_uq = jnp.pad(w_uq, ((0, 0), (0, 0), (0, HEAD_PAD - B_QK_DIM)))
    wq_t = w_uq.reshape(B_Q_LORA, B_HEADS * HEAD_PAD).T.astype(_BF)
    scale_b = (B_QK_DIM ** -0.5) * LOG2E
    gq_col = jnp.concatenate([b_q_nope_norm[0] * scale_b, b_q_rope_norm[0] * scale_b,
                              jnp.zeros((HEAD_PAD - B_QK_DIM,), _F32)])
    gq_tab = jnp.broadcast_to(gq_col[:, None], (HEAD_PAD, tm)).astype(_F32)
    q_norm2 = B_NOPE * jnp.max(b_q_nope_norm[0] ** 2) + B_ROPE * jnp.max(b_q_rope_norm[0] ** 2)
    k_norm2 = B_NOPE * jnp.max(b_k_nope_norm[0] ** 2) + B_ROPE * jnp.max(b_k_rope_norm[0] ** 2)
    bound_b = (scale_b * jnp.sqrt(q_norm2 * k_norm2) * BOUND_SLACK).astype(_BF).astype(_F32)
    qpad_tab = jnp.zeros((HEAD_PAD - B_QK_DIM, tm), _F32).at[0, :].set(-bound_b)
    kone = jnp.zeros((1, LANES), _F32).at[0, B_QK_DIM].set(1.0)

    cos_t, sin_t, ck, sk = _rope_tables(s)

    w_ukv = b_w_ukv[0].reshape(B_KV_LORA, B_HEADS, B_NOPE + B_V_DIM)
    wvk_t = jnp.concatenate([w_ukv[..., B_NOPE:].reshape(B_KV_LORA, B_WIDTH),
                             w_ukv[..., :B_NOPE].reshape(B_KV_LORA, B_HEADS * B_NOPE)],
                            axis=1).T.astype(_BF)
    gkn_tab = col_tab(b_k_nope_norm[0])
    gkr = row(jnp.concatenate([jnp.zeros((B_NOPE,), _F32), b_k_rope_norm[0],
                               jnp.zeros((LANES - B_QK_DIM,), _F32)]))

    bound_a = ((A_HEAD_DIM ** -0.5 * LOG2E) * A_HEAD_DIM * jnp.max(jnp.abs(a_q_norm[0]))
               * jnp.max(jnp.abs(a_k_norm[0])) * BOUND_SLACK)
    bias_tiles = _rel_bias_tiles(a_rel_bias[0], bound_a)

    tok = lambda bi, j: (bi, j, 0)
    pre_in_specs = [
        pl.BlockSpec((None, tm, d), tok),
        _const_spec((1, d)), _const_spec((d, D_FF)), _const_spec((d, D_FF)), _const_spec((D_FF, d)),
        _const_spec((1, d)), _const_spec((FM_ROWS, d)), _const_spec((d, TM_COLS)),
        _const_spec((A_HEAD_DIM, tm)), _const_spec((A_HEAD_DIM, tm)),
        _const_spec((B_HEADS * HEAD_PAD, B_Q_LORA)), _const_spec((HEAD_PAD, tm)),
        _const_spec((HEAD_PAD - B_QK_DIM, tm)),
        pl.BlockSpec((half, tm), lambda bi, j: (0, j)), pl.BlockSpec((half, tm), lambda bi, j: (0, j)),
        _const_spec((1, B_KV_LORA)), _const_spec((B_WIDTH + B_HEADS * B_NOPE, B_KV_LORA)),
        _const_spec((B_NOPE, tm)),
        _const_spec((1, LANES)), _const_spec((1, LANES)),
        pl.BlockSpec((tm, LANES), lambda bi, j: (j, 0)), pl.BlockSpec((tm, LANES), lambda bi, j: (j, 0)),
    ]
    pre_out_shapes = [
        jax.ShapeDtypeStruct((b, s, d), _F32),
        jax.ShapeDtypeStruct((b, s // A_Q_TILE, A_WIDTH, A_Q_TILE), _BF),
        jax.ShapeDtypeStruct((b, s, A_WIDTH), _BF),
        jax.ShapeDtypeStruct((b, A_HEADS, s // A_Q_TILE, A_HEAD_DIM, A_Q_TILE), _BF),
        jax.ShapeDtypeStruct((b, B_HEADS, HEAD_PAD, s), _BF),
        jax.ShapeDtypeStruct((b, B_HEADS, s, HEAD_PAD), _BF),
        jax.ShapeDtypeStruct((b, B_HEADS, nt, B_V_DIM, tm), _BF),
    ]
    pre_out_specs = [
        pl.BlockSpec((None, tm, d), tok),
        pl.BlockSpec((1, tm // A_Q_TILE, A_WIDTH, A_Q_TILE), lambda bi, j: (bi, j, 0, 0)),
        pl.BlockSpec((1, tm, A_WIDTH), tok),
        pl.BlockSpec((1, A_HEADS, tm // A_Q_TILE, A_HEAD_DIM, A_Q_TILE),
                     lambda bi, j: (bi, 0, j, 0, 0)),
        pl.BlockSpec((1, B_HEADS, HEAD_PAD, tm), lambda bi, j: (bi, 0, 0, j)),
        pl.BlockSpec((1, B_HEADS, tm, HEAD_PAD), lambda bi, j: (bi, 0, j, 0)),
        pl.BlockSpec((1, B_HEADS, 1, B_V_DIM, tm), lambda bi, j: (bi, 0, j, 0, 0)),
    ]
    x1, qat, ka, vat, qbt, kb, vbt = pl.pallas_call(
        _pre_kernel,
        grid=(b, nt),
        in_specs=pre_in_specs,
        out_specs=pre_out_specs,
        out_shape=pre_out_shapes,
        scratch_shapes=[pltpu.VMEM((tm, D_FF), _BF)],
        compiler_params=pltpu.CompilerParams(
            dimension_semantics=("arbitrary", "arbitrary"), vmem_limit_bytes=VMEM_LIMIT),
        name="pre",
    )(x, row(ffn1_norm[0]), ffn1_w_gate[0].astype(_BF), ffn1_w_up[0].astype(_BF),
      ffn1_w_down[0].astype(_BF), row(mix_norm[0]), w_fm, w_tm,
      gaq_tab, gak_tab,
      wq_t, gq_tab, qpad_tab, cos_t, sin_t,
      row(b_kv_lat_norm[0]), wvk_t, gkn_tab,
      gkr, kone, ck, sk)

    na = s // A_Q_TILE
    out_a = pl.pallas_call(
        _attn_a_kernel,
        grid=(b, na // A_TILES_PER_STEP),
        in_specs=[
            pl.BlockSpec((1, A_TILES_PER_STEP, A_WIDTH, A_Q_TILE), lambda bi, j: (bi, j, 0, 0)),
            pl.BlockSpec((1, s, A_WIDTH), lambda bi, j: (bi, 0, 0)),
            pl.BlockSpec((1, A_HEADS, na, A_HEAD_DIM, A_Q_TILE), lambda bi, j: (bi, 0, 0, 0, 0)),
            _const_spec((A_HEADS, A_BIAS_TILES, LANES, A_Q_TILE)),
        ],
        out_specs=pl.BlockSpec((1, A_TILES_PER_STEP * A_Q_TILE, A_WIDTH), tok),
        out_shape=jax.ShapeDtypeStruct((b, s, A_WIDTH), _BF),
        compiler_params=pltpu.CompilerParams(
            dimension_semantics=("arbitrary", "arbitrary"), vmem_limit_bytes=VMEM_LIMIT),
        name="attn_a",
    )(qat, ka, vat, bias_tiles)

    out_b = pl.pallas_call(
        _attn_b_kernel,
        grid=(b, nt),
        in_specs=[
            pl.BlockSpec((1, B_HEADS, HEAD_PAD, tm), lambda bi, j: (bi, 0, 0, j)),
            pl.BlockSpec((1, B_HEADS, s, HEAD_PAD), lambda bi, j: (bi, 0, 0, 0)),
            pl.BlockSpec((1, B_HEADS, nt, B_V_DIM, tm), lambda bi, j: (bi, 0, 0, 0, 0)),
        ],
        out_specs=pl.BlockSpec((1, tm, B_WIDTH), tok),
        out_shape=jax.ShapeDtypeStruct((b, s, B_WIDTH), _BF),
        scratch_shapes=[pltpu.VMEM((B_HEADS, tm), _F32), pltpu.VMEM((B_HEADS, tm), _F32),
                        pltpu.VMEM((B_WIDTH, tm), _F32)],
        compiler_params=pltpu.CompilerParams(
            dimension_semantics=("arbitrary", "arbitrary"), vmem_limit_bytes=VMEM_LIMIT),
        name="attn_b",
    )(qbt, kb, vbt)

    n = b * s
    tp = POST_TILE
    flat = lambda j: (j, 0)
    y = pl.pallas_call(
        _post_kernel,
        grid=(n // tp,),
        in_specs=[
            pl.BlockSpec((tp, d), flat),
            pl.BlockSpec((tp, A_WIDTH), flat),
            pl.BlockSpec((tp, B_WIDTH), flat),
            _const_spec((A_WIDTH + B_WIDTH, d)),
            _const_spec((1, d)), _const_spec((d, D_FF)), _const_spec((d, D_FF)), _const_spec((D_FF, d)),
            _const_spec((1, d)),
        ],
        out_specs=pl.BlockSpec((tp, d), flat),
        out_shape=jax.ShapeDtypeStruct((n, d), _F32),
        scratch_shapes=[pltpu.VMEM((tp, D_FF), _BF)],
        compiler_params=pltpu.CompilerParams(
            dimension_semantics=("arbitrary",), vmem_limit_bytes=VMEM_LIMIT),
        name="post",
    )(x1.reshape(n, d), out_a.reshape(n, A_WIDTH), out_b.reshape(n, B_WIDTH),
      w_out[0].astype(_BF), row(ffn2_norm[0]), ffn2_w_gate[0].astype(_BF),
      ffn2_w_up[0].astype(_BF), ffn2_w_down[0].astype(_BF), row(final_norm[0]))
    return y.reshape(b, s, d)
```

```python
import jax
import jax.numpy as jnp
import numpy as np
from jax import lax
from jax.experimental import pallas as pl
from jax.experimental.pallas import tpu as pltpu

D_MODEL = 1024
D_FF = 2816
CHUNK = 64
A_HEADS = 8
A_HEAD_DIM = 64
A_LEFT_CHUNKS = 8
A_MAX_REL = 128
A_WIDTH = A_HEADS * A_HEAD_DIM
B_HEADS = 8
B_Q_LORA = 256
B_KV_LORA = 128
B_NOPE = 64
B_ROPE = 32
B_V_DIM = 64
B_QK_DIM = B_NOPE + B_ROPE
B_WIDTH = B_HEADS * B_V_DIM
ROPE_THETA = 10000.0
EPS = 1e-6
NEG_INF = -1e30
LOG2E = 1.4426950408889634
BOUND_SLACK = 1.0 + 2.0 ** -5
SAFE_DENOM = 2.0 ** -80

LANES = 128
BF16_SUBLANES = 16
TOK_TILE = 512
POST_TILE = 1024
FF_CHUNK = 256
QK_AHEAD = 2

A_Q_TILE = 256
A_TILES_PER_STEP = 2
A_LEFT = A_LEFT_CHUNKS * CHUNK
A_WIN = A_Q_TILE + A_LEFT
A_KEY_SPLITS = ((0, 384), (384, 768))
A_BIAS_ROWS = A_WIN + A_LEFT
A_BIAS_TILES = A_BIAS_ROWS // LANES
A_TAB_ZERO = A_BIAS_ROWS - A_LEFT + LANES
A_TAB_LEN = A_TAB_ZERO + A_LEFT + A_Q_TILE

HEAD_PAD = 128
ACC_ROWS = A_HEAD_DIM + BF16_SUBLANES
FM_ROWS = 3 * A_WIDTH + B_Q_LORA
TM_COLS = B_KV_LORA + LANES
VMEM_LIMIT = 60 * 1024 * 1024

_BF = jnp.bfloat16
_F32 = jnp.float32


def _dot(a, b):
    return jnp.dot(a, b, preferred_element_type=_F32)


def _rms(x, g):
    ms = jnp.mean(x * x, axis=-1, keepdims=True)
    return x * lax.rsqrt(ms + EPS) * g


def _col_rms(x_t):
    return x_t * lax.rsqrt(jnp.mean(x_t * x_t, axis=0, keepdims=True) + EPS)


def _norm_swiglu(x, gain, wg_ref, wu_ref, wd_ref, act_ref):
    h = (x * gain).astype(_BF)
    r = lax.rsqrt(jnp.mean(x * x, axis=-1, keepdims=True) + EPS)
    for c in range(D_FF // FF_CHUNK):
        sl = slice(c * FF_CHUNK, (c + 1) * FF_CHUNK)
        g = _dot(h, wg_ref[:, sl]) * r
        u = _dot(h, wu_ref[:, sl]) * r
        act_ref[:, sl] = (jax.nn.silu(g) * u).astype(_BF)
    return _dot(act_ref[...], wd_ref[...])


def _pre_kernel(x_ref, g1_ref, wg_ref, wu_ref, wd_ref, gmix_ref, wfm_ref, wtm_ref,
                gaq_ref, gak_ref,
                wqt_ref, gqtab_ref, qpad_ref, cost_ref, sint_ref,
                gckv_ref, wvk_ref, gkn_ref,
                gkr_ref, kone_ref, ck_ref, sk_ref,
                x1_ref, qat_ref, ka_ref, vat_ref, qbt_ref, kb_ref, vbt_ref,
                act_ref):
    x = x_ref[...]
    tm = x.shape[0]
    x1 = x + 0.5 * _norm_swiglu(x, g1_ref[...], wg_ref, wu_ref, wd_ref, act_ref)
    x1_ref[...] = x1
    h2f = _rms(x1, gmix_ref[...])
    h2 = h2f.astype(_BF)

    tmj = _dot(h2, wtm_ref[...])
    h2t = h2f.T.astype(_BF)
    ckvn_t = _rms(tmj[:, 0:B_KV_LORA], gckv_ref[...]).T.astype(_BF)
    cq_t = _dot(wfm_ref[3 * A_WIDTH:FM_ROWS, :], h2t)
    vk_t = _dot(wvk_ref[...], ckvn_t)
    q_t = _dot(wqt_ref[...], _col_rms(cq_t).astype(_BF))
    ka_fm = _dot(wfm_ref[2 * A_WIDTH:3 * A_WIDTH, :], h2t)
    qa_fm = _dot(wfm_ref[0:A_WIDTH, :], h2t)
    va_fm = _dot(wfm_ref[A_WIDTH:2 * A_WIDTH, :], h2t)

    gaq = gaq_ref[...]
    gak = gak_ref[...]
    ka_t = []
    for hd in range(A_HEADS):
        rows = slice(hd * A_HEAD_DIM, (hd + 1) * A_HEAD_DIM)
        qa_hd = (_col_rms(qa_fm[rows]) * gaq).astype(_BF)
        for g in range(tm // A_Q_TILE):
            cols = slice(g * A_Q_TILE, (g + 1) * A_Q_TILE)
            qat_ref[0, g, rows, :] = qa_hd[:, cols]
            vat_ref[0, hd, g] = va_fm[rows, cols].astype(_BF)
        ka_t.append(_col_rms(ka_fm[rows]) * gak)
    ka_ref[0] = jnp.concatenate(ka_t, axis=0).T.astype(_BF)

    cos_t = cost_ref[...]
    sin_t = sint_ref[...]
    gq = gqtab_ref[...]
    half = B_ROPE // 2
    for hd in range(B_HEADS):
        r0 = hd * HEAD_PAD
        nope = _col_rms(q_t[r0:r0 + B_NOPE]) * gq[0:B_NOPE]
        rope = _col_rms(q_t[r0 + B_NOPE:r0 + B_QK_DIM]) * gq[B_NOPE:B_QK_DIM]
        r1 = rope[0:half]
        r2 = rope[half:B_ROPE]
        blk = jnp.concatenate(
            [nope, r1 * cos_t - r2 * sin_t, r1 * sin_t + r2 * cos_t, qpad_ref[...]], axis=0)
        qbt_ref[0, hd] = blk.astype(_BF)

    gkn = gkn_ref[...]
    kn_t = []
    for hd in range(B_HEADS):
        vbt_ref[0, hd, 0] = vk_t[hd * B_V_DIM:(hd + 1) * B_V_DIM].astype(_BF)
        kn_t.append(_col_rms(vk_t[B_WIDTH + hd * B_NOPE:B_WIDTH + (hd + 1) * B_NOPE]) * gkn)
    kn = jnp.concatenate(kn_t, axis=0).T
    kr = tmj[:, B_KV_LORA:TM_COLS]
    y = kr * lax.rsqrt(jnp.sum(kr * kr, axis=-1, keepdims=True) * (1.0 / B_ROPE) + EPS) * gkr_ref[...]
    lane = lax.broadcasted_iota(jnp.int32, y.shape, 1)
    swapped = jnp.where(lane < B_NOPE + half,
                        pltpu.roll(y, LANES - half, axis=1), pltpu.roll(y, half, axis=1))
    kpe = y * ck_ref[...] + swapped * sk_ref[...] + kone_ref[...]
    for hd in range(B_HEADS):
        pair_tile = kn[:, (hd // 2) * LANES:(hd // 2 + 1) * LANES]
        if hd % 2 == 1:
            pair_tile = pltpu.roll(pair_tile, B_NOPE, axis=1)
        kb_ref[0, hd] = jnp.where(lane < B_NOPE, pair_tile, kpe).astype(_BF)


def _bias_kernel(tab_ref, o_ref):
    u = lax.broadcasted_iota(jnp.int32, (LANES, A_Q_TILE), 0)
    qc = lax.broadcasted_iota(jnp.int32, (LANES, A_Q_TILE), 1) // CHUNK
    for tile in range(A_BIAS_TILES):
        a = A_TAB_ZERO + A_LEFT - LANES * tile
        w = jnp.concatenate([tab_ref[0, :, a:a + A_Q_TILE], tab_ref[0, :, a - A_Q_TILE:a]], axis=1)
        rolled = pltpu.roll(jnp.broadcast_to(w, (LANES, 2 * A_Q_TILE)), 0, 1,
                            stride=1, stride_axis=0)
        kc = (LANES * tile + u) // CHUNK - A_LEFT_CHUNKS
        valid = (kc <= qc) & (kc >= qc - A_LEFT_CHUNKS)
        o_ref[0, tile] = jnp.where(valid, rolled[:, :A_Q_TILE], NEG_INF)


def _rel_bias_tiles(rel_bias, qk_bound):
    pad_lo = A_TAB_ZERO - A_MAX_REL
    pad_hi = A_TAB_LEN - pad_lo - (2 * A_MAX_REL + 1)
    rb = rel_bias.astype(_F32) * LOG2E
    rb = rb - (qk_bound + jnp.max(rb, axis=1, keepdims=True))
    table = jnp.concatenate([jnp.broadcast_to(rb[:, :1], (A_HEADS, pad_lo)), rb,
                             jnp.broadcast_to(rb[:, -1:], (A_HEADS, pad_hi))], axis=1)
    return pl.pallas_call(
        _bias_kernel,
        grid=(A_HEADS,),
        in_specs=[pl.BlockSpec((1, 1, A_TAB_LEN), lambda h: (h, 0, 0))],
        out_specs=pl.BlockSpec((1, A_BIAS_TILES, LANES, A_Q_TILE), lambda h: (h, 0, 0, 0)),
        out_shape=jax.ShapeDtypeStruct((A_HEADS, A_BIAS_TILES, LANES, A_Q_TILE), _F32),
        name="rel_bias",
    )(table.reshape(A_HEADS, 1, A_TAB_LEN))


def _attn_a_kernel(qt_ref, k_ref, vt_ref, bias_ref, o_ref):
    def one_tile(sub, carry):
        _attn_a_tile(pl.program_id(1) * A_TILES_PER_STEP + sub, sub,
                     qt_ref, k_ref, vt_ref, bias_ref, o_ref)
        return carry

    lax.fori_loop(0, A_TILES_PER_STEP, one_tile, 0)


def _attn_a_tile(i, sub, qt_ref, k_ref, vt_ref, bias_ref, o_ref):
    left_tiles = A_LEFT // A_Q_TILE
    g0 = jnp.maximum(i - left_tiles, 0)
    start = pl.multiple_of(g0 * A_Q_TILE, A_Q_TILE)
    u0 = (left_tiles - jnp.minimum(i, left_tiles)) * (A_Q_TILE // LANES)
    zeros = jnp.zeros((A_HEAD_DIM, A_Q_TILE), _BF)
    ones_rows = jnp.ones((ACC_ROWS - A_HEAD_DIM, A_WIN), _BF)

    def scores(pair):
        q2 = qt_ref[0, sub, pair * LANES:(pair + 1) * LANES, :]
        q_even = jnp.concatenate([q2[:A_HEAD_DIM], zeros], axis=0)
        q_odd = jnp.concatenate([zeros, q2[A_HEAD_DIM:]], axis=0)
        qq = jnp.concatenate([q_even, q_odd], axis=1)
        parts = []
        for r0, r1 in A_KEY_SPLITS:
            k = k_ref[0, pl.ds(pl.multiple_of(start + r0, LANES), r1 - r0),
                      pair * LANES:(pair + 1) * LANES]
            parts.append(_dot(k, qq))
        return jnp.concatenate(parts, axis=0)

    def attend(subtract_max):
        n_pairs = A_HEADS // 2
        pending = [scores(0)]
        outs, denoms = [], []
        for pair in range(n_pairs):
            s = pending.pop(0)
            if pair + 1 < n_pairs:
                pending.append(scores(pair + 1))
            bias = jnp.concatenate(
                [jnp.concatenate([bias_ref[2 * pair + e, u0 + j] for j in range(A_WIN // LANES)],
                                 axis=0) for e in range(2)], axis=1)
            s = s + bias
            if subtract_max:
                s = s - jnp.max(s, axis=0, keepdims=True)
            p = jnp.exp2(s).astype(_BF)
            for e in range(2):
                hd = 2 * pair + e
                v = jnp.concatenate([vt_ref[0, hd, g0 + g] for g in range(A_WIN // A_Q_TILE)],
                                    axis=1)
                o = _dot(jnp.concatenate([v, ones_rows], axis=0),
                         p[:, e * A_Q_TILE:(e + 1) * A_Q_TILE])
                denoms.append(o[A_HEAD_DIM:A_HEAD_DIM + 1])
                outs.append(o[:A_HEAD_DIM] / denoms[-1])
        o_ref[0, pl.ds(pl.multiple_of(sub * A_Q_TILE, A_Q_TILE), A_Q_TILE), :] = (
            jnp.concatenate(outs, axis=0).T.astype(_BF))
        return jnp.min(jnp.concatenate(denoms, axis=0))

    smallest = attend(subtract_max=False)

    @pl.when(jnp.logical_not(smallest > SAFE_DENOM))
    def _():
        attend(subtract_max=True)


def _attn_b_kernel(qt_ref, k_ref, vt_ref, o_ref, m_ref, l_ref, acc_ref):
    i = pl.program_id(1)
    t = TOK_TILE

    def acc_rows(hd):
        return slice(hd * B_V_DIM, (hd + 1) * B_V_DIM)

    def diag_mask(s):
        krow = lax.broadcasted_iota(jnp.int32, (t, t), 0) // CHUNK
        qcol = lax.broadcasted_iota(jnp.int32, (t, t), 1) // CHUNK
        return jnp.where(krow <= qcol, s, NEG_INF)

    def sweep(kt, probs, commit):
        def scores(hd):
            k = k_ref[0, hd, pl.ds(pl.multiple_of(kt * t, t), t), :]
            return _dot(k, qt_ref[0, hd])

        pending = [scores(hd) for hd in range(QK_AHEAD)]
        for hd in range(B_HEADS):
            s = pending.pop(0)
            if hd + QK_AHEAD < B_HEADS:
                pending.append(scores(hd + QK_AHEAD))
            p = probs(hd, s)
            commit(hd, _dot(vt_ref[0, hd, kt], p.astype(_BF)), jnp.sum(p, axis=0, keepdims=True))

    def assign(hd, pv, ps):
        acc_ref[acc_rows(hd), :] = pv
        l_ref[hd:hd + 1, :] = ps

    def add(hd, pv, ps):
        acc_ref[acc_rows(hd), :] += pv
        l_ref[hd:hd + 1, :] += ps

    sweep(i, lambda hd, s: jnp.exp2(diag_mask(s)), assign)
    safe = jnp.min(l_ref[...]) > SAFE_DENOM

    @pl.when(safe)
    def _():
        def full_tile(kt, carry):
            sweep(kt, lambda hd, s: jnp.exp2(s), add)
            return carry

        lax.fori_loop(0, i, full_tile, 0)

    @pl.when(jnp.logical_not(safe))
    def _():
        m_ref[...] = jnp.full(m_ref.shape, NEG_INF, _F32)
        l_ref[...] = jnp.zeros(l_ref.shape, _F32)
        acc_ref[...] = jnp.zeros(acc_ref.shape, _F32)

        def online_step(kt, masked):
            alphas = {}

            def probs(hd, s):
                if masked:
                    s = diag_mask(s)
                m = m_ref[hd:hd + 1, :]
                m_new = jnp.maximum(m, jnp.max(s, axis=0, keepdims=True))
                m_ref[hd:hd + 1, :] = m_new
                alphas[hd] = jnp.exp2(m - m_new)
                return jnp.exp2(s - m_new)

            def rescale_add(hd, pv, ps):
                acc_ref[acc_rows(hd), :] = alphas[hd] * acc_ref[acc_rows(hd), :] + pv
                l_ref[hd:hd + 1, :] = alphas[hd] * l_ref[hd:hd + 1, :] + ps

            sweep(kt, probs, rescale_add)

        def online_full(kt, carry):
            online_step(kt, masked=False)
            return carry

        lax.fori_loop(0, i, online_full, 0)
        online_step(i, masked=True)

    outs = [acc_ref[acc_rows(hd), :] / l_ref[hd:hd + 1, :] for hd in range(B_HEADS)]
    o_ref[0] = jnp.concatenate(outs, axis=0).T.astype(_BF)


def _post_kernel(x1_ref, oa_ref, ob_ref, wout_ref, g2_ref, wg_ref, wu_ref, wd_ref,
                 gf_ref, y_ref, act_ref):
    x2 = (x1_ref[...] + _dot(oa_ref[...], wout_ref[0:A_WIDTH, :])
          + _dot(ob_ref[...], wout_ref[A_WIDTH:A_WIDTH + B_WIDTH, :]))
    x3 = x2 + 0.5 * _norm_swiglu(x2, g2_ref[...], wg_ref, wu_ref, wd_ref, act_ref)
    y_ref[...] = _rms(x3, gf_ref[...])


def _rope_tables(s):
    half = B_ROPE // 2
    inv = (1.0 / (ROPE_THETA ** (np.arange(0, B_ROPE, 2, dtype=np.float32) / B_ROPE))).astype(np.float32)
    ang = np.arange(s, dtype=np.float32)[:, None] * inv[None, :]
    cos, sin = np.cos(ang).astype(np.float32), np.sin(ang).astype(np.float32)
    ck = np.zeros((s, LANES), np.float32)
    sk = np.zeros((s, LANES), np.float32)
    ck[:, B_NOPE:B_NOPE + half] = cos
    ck[:, B_NOPE + half:B_QK_DIM] = cos
    sk[:, B_NOPE:B_NOPE + half] = -sin
    sk[:, B_NOPE + half:B_QK_DIM] = sin
    return (jnp.asarray(np.ascontiguousarray(cos.T)), jnp.asarray(np.ascontiguousarray(sin.T)),
            jnp.asarray(ck), jnp.asarray(sk))


def _const_spec(shape):
    n = len(shape)
    return pl.BlockSpec(shape, lambda *_: (0,) * n, pipeline_mode=pl.Buffered(1))


def kernel(x, ffn1_norm, ffn1_w_gate, ffn1_w_up, ffn1_w_down, mix_norm, w_in, a_q_norm, a_k_norm, a_rel_bias, b_q_lat_norm, b_w_uq, b_kv_lat_norm, b_w_ukv, b_q_nope_norm, b_q_rope_norm, b_k_nope_norm, b_k_rope_norm, w_out, ffn2_norm, ffn2_w_gate, ffn2_w_up, ffn2_w_down, final_norm):
    b, s, d = x.shape
    assert d == D_MODEL and s % TOK_TILE == 0 and s >= A_WIN and ffn1_norm.shape[0] == 1
    assert (b * s) % POST_TILE == 0 and s % (A_Q_TILE * A_TILES_PER_STEP) == 0
    tm = TOK_TILE
    nt = s // tm
    half = B_ROPE // 2

    def row(v):
        return v.astype(_F32)[None, :]

    w_in_l = w_in[0]
    o_va, o_cq = 2 * A_WIDTH, 3 * A_WIDTH
    o_ckv = o_cq + B_Q_LORA
    o_kr = o_ckv + B_KV_LORA
    w_fm = jnp.concatenate([w_in_l[:, 0:A_WIDTH], w_in_l[:, o_va:o_cq], w_in_l[:, A_WIDTH:o_va],
                            w_in_l[:, o_cq:o_ckv]], axis=1).T.astype(_BF)
    kr_cols = jnp.zeros((d, LANES), _F32).at[:, B_NOPE:B_QK_DIM].set(w_in_l[:, o_kr:o_kr + B_ROPE])
    w_tm = jnp.concatenate([w_in_l[:, o_ckv:o_kr], kr_cols], axis=1).astype(_BF)

    def col_tab(v):
        return jnp.broadcast_to(v.astype(_F32)[:, None], (v.shape[0], tm))

    gaq_tab = col_tab(a_q_norm[0] * (A_HEAD_DIM ** -0.5 * LOG2E))
    gak_tab = col_tab(a_k_norm[0])

    w_uq = (b_w_uq[0] * b_q_lat_norm[0][:, None]).reshape(B_Q_LORA, B_HEADS, B_QK_DIM)
    w_uq = jnp.pad(w_uq, ((0, 0), (0, 0), (0, HEAD_PAD - B_QK_DIM)))
    wq_t = w_uq.reshape(B_Q_LORA, B_HEADS * HEAD_PAD).T.astype(_BF)
    scale_b = (B_QK_DIM ** -0.5) * LOG2E
    gq_col = jnp.concatenate([b_q_nope_norm[0] * scale_b, b_q_rope_norm[0] * scale_b,
                              jnp.zeros((HEAD_PAD - B_QK_DIM,), _F32)])
    gq_tab = jnp.broadcast_to(gq_col[:, None], (HEAD_PAD, tm)).astype(_F32)
    q_norm2 = B_NOPE * jnp.max(b_q_nope_norm[0] ** 2) + B_ROPE * jnp.max(b_q_rope_norm[0] ** 2)
    k_norm2 = B_NOPE * jnp.max(b_k_nope_norm[0] ** 2) + B_ROPE * jnp.max(b_k_rope_norm[0] ** 2)
    bound_b = (scale_b * jnp.sqrt(q_norm2 * k_norm2) * BOUND_SLACK).astype(_BF).astype(_F32)
    qpad_tab = jnp.zeros((HEAD_PAD - B_QK_DIM, tm), _F32).at[0, :].set(-bound_b)
    kone = jnp.zeros((1, LANES), _F32).at[0, B_QK_DIM].set(1.0)

    cos_t, sin_t, ck, sk = _rope_tables(s)

    w_ukv = b_w_ukv[0].reshape(B_KV_LORA, B_HEADS, B_NOPE + B_V_DIM)
    wvk_t = jnp.concatenate([w_ukv[..., B_NOPE:].reshape(B_KV_LORA, B_WIDTH),
                             w_ukv[..., :B_NOPE].reshape(B_KV_LORA, B_HEADS * B_NOPE)],
                            axis=1).T.astype(_BF)
    gkn_tab = col_tab(b_k_nope_norm[0])
    gkr = row(jnp.concatenate([jnp.zeros((B_NOPE,), _F32), b_k_rope_norm[0],
                               jnp.zeros((LANES - B_QK_DIM,), _F32)]))

    bound_a = ((A_HEAD_DIM ** -0.5 * LOG2E) * A_HEAD_DIM * jnp.max(jnp.abs(a_q_norm[0]))
               * jnp.max(jnp.abs(a_k_norm[0])) * BOUND_SLACK)
    bias_tiles = _rel_bias_tiles(a_rel_bias[0], bound_a)

    tok = lambda bi, j: (bi, j, 0)
    pre_in_specs = [
        pl.BlockSpec((None, tm, d), tok),
        _const_spec((1, d)), _const_spec((d, D_FF)), _const_spec((d, D_FF)), _const_spec((D_FF, d)),
        _const_spec((1, d)), _const_spec((FM_ROWS, d)), _const_spec((d, TM_COLS)),
        _const_spec((A_HEAD_DIM, tm)), _const_spec((A_HEAD_DIM, tm)),
        _const_spec((B_HEADS * HEAD_PAD, B_Q_LORA)), _const_spec((HEAD_PAD, tm)),
        _const_spec((HEAD_PAD - B_QK_DIM, tm)),
        pl.BlockSpec((half, tm), lambda bi, j: (0, j)), pl.BlockSpec((half, tm), lambda bi, j: (0, j)),
        _const_spec((1, B_KV_LORA)), _const_spec((B_WIDTH + B_HEADS * B_NOPE, B_KV_LORA)),
        _const_spec((B_NOPE, tm)),
        _const_spec((1, LANES)), _const_spec((1, LANES)),
        pl.BlockSpec((tm, LANES), lambda bi, j: (j, 0)), pl.BlockSpec((tm, LANES), lambda bi, j: (j, 0)),
    ]
    pre_out_shapes = [
        jax.ShapeDtypeStruct((b, s, d), _F32),
        jax.ShapeDtypeStruct((b, s // A_Q_TILE, A_WIDTH, A_Q_TILE), _BF),
        jax.ShapeDtypeStruct((b, s, A_WIDTH), _BF),
        jax.ShapeDtypeStruct((b, A_HEADS, s // A_Q_TILE, A_HEAD_DIM, A_Q_TILE), _BF),
        jax.ShapeDtypeStruct((b, B_HEADS, HEAD_PAD, s), _BF),
        jax.ShapeDtypeStruct((b, B_HEADS, s, HEAD_PAD), _BF),
        jax.ShapeDtypeStruct((b, B_HEADS, nt, B_V_DIM, tm), _BF),
    ]
    pre_out_specs = [
        pl.BlockSpec((None, tm, d), tok),
        pl.BlockSpec((1, tm // A_Q_TILE, A_WIDTH, A_Q_TILE), lambda bi, j: (bi, j, 0, 0)),
        pl.BlockSpec((1, tm, A_WIDTH), tok),
        pl.BlockSpec((1, A_HEADS, tm // A_Q_TILE, A_HEAD_DIM, A_Q_TILE),
                     lambda bi, j: (bi, 0, j, 0, 0)),
        pl.BlockSpec((1, B_HEADS, HEAD_PAD, tm), lambda bi, j: (bi, 0, 0, j)),
        pl.BlockSpec((1, B_HEADS, tm, HEAD_PAD), lambda bi, j: (bi, 0, j, 0)),
        pl.BlockSpec((1, B_HEADS, 1, B_V_DIM, tm), lambda bi, j: (bi, 0, j, 0, 0)),
    ]
    x1, qat, ka, vat, qbt, kb, vbt = pl.pallas_call(
        _pre_kernel,
        grid=(b, nt),
        in_specs=pre_in_specs,
        out_specs=pre_out_specs,
        out_shape=pre_out_shapes,
        scratch_shapes=[pltpu.VMEM((tm, D_FF), _BF)],
        compiler_params=pltpu.CompilerParams(
            dimension_semantics=("arbitrary", "arbitrary"), vmem_limit_bytes=VMEM_LIMIT),
        name="pre",
    )(x, row(ffn1_norm[0]), ffn1_w_gate[0].astype(_BF), ffn1_w_up[0].astype(_BF),
      ffn1_w_down[0].astype(_BF), row(mix_norm[0]), w_fm, w_tm,
      gaq_tab, gak_tab,
      wq_t, gq_tab, qpad_tab, cos_t, sin_t,
      row(b_kv_lat_norm[0]), wvk_t, gkn_tab,
      gkr, kone, ck, sk)

    na = s // A_Q_TILE
    out_a = pl.pallas_call(
        _attn_a_kernel,
        grid=(b, na // A_TILES_PER_STEP),
        in_specs=[
            pl.BlockSpec((1, A_TILES_PER_STEP, A_WIDTH, A_Q_TILE), lambda bi, j: (bi, j, 0, 0)),
            pl.BlockSpec((1, s, A_WIDTH), lambda bi, j: (bi, 0, 0)),
            pl.BlockSpec((1, A_HEADS, na, A_HEAD_DIM, A_Q_TILE), lambda bi, j: (bi, 0, 0, 0, 0)),
            _const_spec((A_HEADS, A_BIAS_TILES, LANES, A_Q_TILE)),
        ],
        out_specs=pl.BlockSpec((1, A_TILES_PER_STEP * A_Q_TILE, A_WIDTH), tok),
        out_shape=jax.ShapeDtypeStruct((b, s, A_WIDTH), _BF),
        compiler_params=pltpu.CompilerParams(
            dimension_semantics=("arbitrary", "arbitrary"), vmem_limit_bytes=VMEM_LIMIT),
        name="attn_a",
    )(qat, ka, vat, bias_tiles)

    out_b = pl.pallas_call(
        _attn_b_kernel,
        grid=(b, nt),
        in_specs=[
            pl.BlockSpec((1, B_HEADS, HEAD_PAD, tm), lambda bi, j: (bi, 0, 0, j)),
            pl.BlockSpec((1, B_HEADS, s, HEAD_PAD), lambda bi, j: (bi, 0, 0, 0)),
            pl.BlockSpec((1, B_HEADS, nt, B_V_DIM, tm), lambda bi, j: (bi, 0, 0, 0, 0)),
        ],
        out_specs=pl.BlockSpec((1, tm, B_WIDTH), tok),
        out_shape=jax.ShapeDtypeStruct((b, s, B_WIDTH), _BF),
        scratch_shapes=[pltpu.VMEM((B_HEADS, tm), _F32), pltpu.VMEM((B_HEADS, tm), _F32),
                        pltpu.VMEM((B_WIDTH, tm), _F32)],
        compiler_params=pltpu.CompilerParams(
            dimension_semantics=("arbitrary", "arbitrary"), vmem_limit_bytes=VMEM_LIMIT),
        name="attn_b",
    )(qbt, kb, vbt)

    n = b * s
    tp = POST_TILE
    flat = lambda j: (j, 0)
    y = pl.pallas_call(
        _post_kernel,
        grid=(n // tp,),
        in_specs=[
            pl.BlockSpec((tp, d), flat),
            pl.BlockSpec((tp, A_WIDTH), flat),
            pl.BlockSpec((tp, B_WIDTH), flat),
            _const_spec((A_WIDTH + B_WIDTH, d)),
            _const_spec((1, d)), _const_spec((d, D_FF)), _const_spec((d, D_FF)), _const_spec((D_FF, d)),
            _const_spec((1, d)),
        ],
        out_specs=pl.BlockSpec((tp, d), flat),
        out_shape=jax.ShapeDtypeStruct((n, d), _F32),
        scratch_shapes=[pltpu.VMEM((tp, D_FF), _BF)],
        compiler_params=pltpu.CompilerParams(
            dimension_semantics=("arbitrary",), vmem_limit_bytes=VMEM_LIMIT),
        name="post",
    )(x1.reshape(n, d), out_a.reshape(n, A_WIDTH), out_b.reshape(n, B_WIDTH),
      w_out[0].astype(_BF), row(ffn2_norm[0]), ffn2_w_gate[0].astype(_BF),
      ffn2_w_up[0].astype(_BF), ffn2_w_down[0].astype(_BF), row(final_norm[0]))
    return y.reshape(b, s, d)
```

```python
import jax
import jax.numpy as jnp
import numpy as np
from jax import lax
from jax.experimental import pallas as pl
from jax.experimental.pallas import tpu as pltpu

D_MODEL = 1024
D_FF = 2816
CHUNK = 64
A_HEADS = 8
A_HEAD_DIM = 64
A_LEFT_CHUNKS = 8
A_MAX_REL = 128
A_WIDTH = A_HEADS * A_HEAD_DIM
B_HEADS = 8
B_Q_LORA = 256
B_KV_LORA = 128
B_NOPE = 64
B_ROPE = 32
B_V_DIM = 64
B_QK_DIM = B_NOPE + B_ROPE
B_WIDTH = B_HEADS * B_V_DIM
ROPE_THETA = 10000.0
EPS = 1e-6
NEG_INF = -1e30
LOG2E = 1.4426950408889634
BOUND_SLACK = 1.0 + 2.0 ** -5
SAFE_DENOM = 2.0 ** -80

LANES = 128
BF16_SUBLANES = 16
TOK_TILE = 512
POST_TILE = 1024
FF_CHUNK = 256
QK_AHEAD = 2

A_Q_TILE = 256
A_TILES_PER_STEP = 2
A_LEFT = A_LEFT_CHUNKS * CHUNK
A_WIN = A_Q_TILE + A_LEFT
A_KEY_SPLITS = ((0, 384), (384, 768))
A_BIAS_ROWS = A_WIN + A_LEFT
A_BIAS_TILES = A_BIAS_ROWS // LANES
A_TAB_ZERO = A_BIAS_ROWS - A_LEFT + LANES
A_TAB_LEN = A_TAB_ZERO + A_LEFT + A_Q_TILE

HEAD_PAD = 128
ACC_ROWS = A_HEAD_DIM + BF16_SUBLANES
FM_ROWS = 3 * A_WIDTH + B_Q_LORA
TM_COLS = B_KV_LORA + LANES
VMEM_LIMIT = 60 * 1024 * 1024

_BF = jnp.bfloat16
_F32 = jnp.float32


def _dot(a, b):
    return jnp.dot(a, b, preferred_element_type=_F32)


def _dot_nt(a, b):
    return lax.dot_general(a, b, (((1,), (1,)), ((), ())), preferred_element_type=_F32)


def _rms(x, g):
    ms = jnp.mean(x * x, axis=-1, keepdims=True)
    return x * lax.rsqrt(ms + EPS) * g


def _col_rms(x_t):
    return x_t * lax.rsqrt(jnp.mean(x_t * x_t, axis=0, keepdims=True) + EPS)


def _norm_swiglu(x, gain, wg_ref, wu_ref, wd_ref, act_ref):
    h = (x * gain).astype(_BF)
    r = lax.rsqrt(jnp.mean(x * x, axis=-1, keepdims=True) + EPS)
    for c in range(D_FF // FF_CHUNK):
        sl = slice(c * FF_CHUNK, (c + 1) * FF_CHUNK)
        g = _dot(h, wg_ref[:, sl]) * r
        u = _dot(h, wu_ref[:, sl]) * r
        act_ref[:, sl] = (jax.nn.silu(g) * u).astype(_BF)
    return _dot(act_ref[...], wd_ref[...])


def _pre_kernel(x_ref, g1_ref, wg_ref, wu_ref, wd_ref, gmix_ref, wfm_ref, wtm_ref,
                gaq_ref, gak_ref,
                wqt_ref, gqtab_ref, qpad_ref, cost_ref, sint_ref,
                gckv_ref, wvk_ref, gkn_ref,
                gkr_ref, kone_ref, ck_ref, sk_ref,
                x1_ref, qat_ref, ka_ref, vat_ref, qbt_ref, kb_ref, vbt_ref,
                act_ref):
    x = x_ref[...]
    tm = x.shape[0]
    x1 = x + 0.5 * _norm_swiglu(x, g1_ref[...], wg_ref, wu_ref, wd_ref, act_ref)
    x1_ref[...] = x1
    h2f = _rms(x1, gmix_ref[...])
    h2 = h2f.astype(_BF)

    tmj = _dot(h2, wtm_ref[...])
    ckvn = _rms(tmj[:, 0:B_KV_LORA], gckv_ref[...]).astype(_BF)
    cq_t = _dot_nt(wfm_ref[3 * A_WIDTH:FM_ROWS, :], h2)
    vk_t = _dot_nt(wvk_ref[...], ckvn)
    q_t = _dot(wqt_ref[...], _col_rms(cq_t).astype(_BF))
    ka_fm = _dot_nt(wfm_ref[2 * A_WIDTH:3 * A_WIDTH, :], h2)
    qa_fm = _dot_nt(wfm_ref[0:A_WIDTH, :], h2)
    va_fm = _dot_nt(wfm_ref[A_WIDTH:2 * A_WIDTH, :], h2)

    gaq = gaq_ref[...]
    gak = gak_ref[...]
    ka_t = []
    for hd in range(A_HEADS):
        rows = slice(hd * A_HEAD_DIM, (hd + 1) * A_HEAD_DIM)
        qa_hd = (_col_rms(qa_fm[rows]) * gaq).astype(_BF)
        for g in range(tm // A_Q_TILE):
            cols = slice(g * A_Q_TILE, (g + 1) * A_Q_TILE)
            qat_ref[0, g, rows, :] = qa_hd[:, cols]
            vat_ref[0, hd, g] = va_fm[rows, cols].astype(_BF)
        ka_t.append(_col_rms(ka_fm[rows]) * gak)
    ka_ref[0] = jnp.concatenate(ka_t, axis=0).T.astype(_BF)

    cos_t = cost_ref[...]
    sin_t = sint_ref[...]
    gq = gqtab_ref[...]
    half = B_ROPE // 2
    for hd in range(B_HEADS):
        r0 = hd * HEAD_PAD
        nope = _col_rms(q_t[r0:r0 + B_NOPE]) * gq[0:B_NOPE]
        rope = _col_rms(q_t[r0 + B_NOPE:r0 + B_QK_DIM]) * gq[B_NOPE:B_QK_DIM]
        r1 = rope[0:half]
        r2 = rope[half:B_ROPE]
        blk = jnp.concatenate(
            [nope, r1 * cos_t - r2 * sin_t, r1 * sin_t + r2 * cos_t, qpad_ref[...]], axis=0)
        qbt_ref[0, hd] = blk.astype(_BF)

    gkn = gkn_ref[...]
    kn_t = []
    for hd in range(B_HEADS):
        vbt_ref[0, hd, 0] = vk_t[hd * B_V_DIM:(hd + 1) * B_V_DIM].astype(_BF)
        kn_t.append(_col_rms(vk_t[B_WIDTH + hd * B_NOPE:B_WIDTH + (hd + 1) * B_NOPE]) * gkn)
    kn = jnp.concatenate(kn_t, axis=0).T
    kr = tmj[:, B_KV_LORA:TM_COLS]
    y = kr * lax.rsqrt(jnp.sum(kr * kr, axis=-1, keepdims=True) * (1.0 / B_ROPE) + EPS) * gkr_ref[...]
    lane = lax.broadcasted_iota(jnp.int32, y.shape, 1)
    swapped = jnp.where(lane < B_NOPE + half,
                        pltpu.roll(y, LANES - half, axis=1), pltpu.roll(y, half, axis=1))
    kpe = y * ck_ref[...] + swapped * sk_ref[...] + kone_ref[...]
    for hd in range(B_HEADS):
        pair_tile = kn[:, (hd // 2) * LANES:(hd // 2 + 1) * LANES]
        if hd % 2 == 1:
            pair_tile = pltpu.roll(pair_tile, B_NOPE, axis=1)
        kb_ref[0, hd] = jnp.where(lane < B_NOPE, pair_tile, kpe).astype(_BF)


def _bias_kernel(tab_ref, o_ref):
    u = lax.broadcasted_iota(jnp.int32, (LANES, A_Q_TILE), 0)
    qc = lax.broadcasted_iota(jnp.int32, (LANES, A_Q_TILE), 1) // CHUNK
    for tile in range(A_BIAS_TILES):
        a = A_TAB_ZERO + A_LEFT - LANES * tile
        w = jnp.concatenate([tab_ref[0, :, a:a + A_Q_TILE], tab_ref[0, :, a - A_Q_TILE:a]], axis=1)
        rolled = pltpu.roll(jnp.broadcast_to(w, (LANES, 2 * A_Q_TILE)), 0, 1,
                            stride=1, stride_axis=0)
        kc = (LANES * tile + u) // CHUNK - A_LEFT_CHUNKS
        valid = (kc <= qc) & (kc >= qc - A_LEFT_CHUNKS)
        o_ref[0, tile] = jnp.where(valid, rolled[:, :A_Q_TILE], NEG_INF)


def _rel_bias_tiles(rel_bias, qk_bound):
    pad_lo = A_TAB_ZERO - A_MAX_REL
    pad_hi = A_TAB_LEN - pad_lo - (2 * A_MAX_REL + 1)
    rb = rel_bias.astype(_F32) * LOG2E
    rb = rb - (qk_bound + jnp.max(rb, axis=1, keepdims=True))
    table = jnp.concatenate([jnp.broadcast_to(rb[:, :1], (A_HEADS, pad_lo)), rb,
                             jnp.broadcast_to(rb[:, -1:], (A_HEADS, pad_hi))], axis=1)
    return pl.pallas_call(
        _bias_kernel,
        grid=(A_HEADS,),
        in_specs=[pl.BlockSpec((1, 1, A_TAB_LEN), lambda h: (h, 0, 0))],
        out_specs=pl.BlockSpec((1, A_BIAS_TILES, LANES, A_Q_TILE), lambda h: (h, 0, 0, 0)),
        out_shape=jax.ShapeDtypeStruct((A_HEADS, A_BIAS_TILES, LANES, A_Q_TILE), _F32),
        name="rel_bias",
    )(table.reshape(A_HEADS, 1, A_TAB_LEN))


def _attn_a_kernel(qt_ref, k_ref, vt_ref, bias_ref, o_ref):
    def one_tile(sub, carry):
        _attn_a_tile(pl.program_id(1) * A_TILES_PER_STEP + sub, sub,
                     qt_ref, k_ref, vt_ref, bias_ref, o_ref)
        return carry

    lax.fori_loop(0, A_TILES_PER_STEP, one_tile, 0)


def _attn_a_tile(i, sub, qt_ref, k_ref, vt_ref, bias_ref, o_ref):
    left_tiles = A_LEFT // A_Q_TILE
    g0 = jnp.maximum(i - left_tiles, 0)
    start = pl.multiple_of(g0 * A_Q_TILE, A_Q_TILE)
    u0 = (left_tiles - jnp.minimum(i, left_tiles)) * (A_Q_TILE // LANES)
    zeros = jnp.zeros((A_HEAD_DIM, A_Q_TILE), _BF)
    ones_rows = jnp.ones((ACC_ROWS - A_HEAD_DIM, A_WIN), _BF)

    def scores(pair):
        q2 = qt_ref[0, sub, pair * LANES:(pair + 1) * LANES, :]
        q_even = jnp.concatenate([q2[:A_HEAD_DIM], zeros], axis=0)
        q_odd = jnp.concatenate([zeros, q2[A_HEAD_DIM:]], axis=0)
        qq = jnp.concatenate([q_even, q_odd], axis=1)
        parts = []
        for r0, r1 in A_KEY_SPLITS:
            k = k_ref[0, pl.ds(pl.multiple_of(start + r0, LANES), r1 - r0),
                      pair * LANES:(pair + 1) * LANES]
            parts.append(_dot(k, qq))
        return jnp.concatenate(parts, axis=0)

    def attend(subtract_max):
        n_pairs = A_HEADS // 2
        pending = [scores(0)]
        outs, denoms = [], []
        for pair in range(n_pairs):
            s = pending.pop(0)
            if pair + 1 < n_pairs:
                pending.append(scores(pair + 1))
            bias = jnp.concatenate(
                [jnp.concatenate([bias_ref[2 * pair + e, u0 + j] for j in range(A_WIN // LANES)],
                                 axis=0) for e in range(2)], axis=1)
            s = s + bias
            if subtract_max:
                s = s - jnp.max(s, axis=0, keepdims=True)
            p = jnp.exp2(s).astype(_BF)
            for e in range(2):
                hd = 2 * pair + e
                v = jnp.concatenate([vt_ref[0, hd, g0 + g] for g in range(A_WIN // A_Q_TILE)],
                                    axis=1)
                o = _dot(jnp.concatenate([v, ones_rows], axis=0),
                         p[:, e * A_Q_TILE:(e + 1) * A_Q_TILE])
                denoms.append(o[A_HEAD_DIM:A_HEAD_DIM + 1])
                outs.append(o[:A_HEAD_DIM] / denoms[-1])
        o_ref[0, pl.ds(pl.multiple_of(sub * A_Q_TILE, A_Q_TILE), A_Q_TILE), :] = (
            jnp.concatenate(outs, axis=0).T.astype(_BF))
        return jnp.min(jnp.concatenate(denoms, axis=0))

    smallest = attend(subtract_max=False)

    @pl.when(jnp.logical_not(smallest > SAFE_DENOM))
    def _():
        attend(subtract_max=True)


def _attn_b_kernel(qt_ref, k_ref, vt_ref, wg_ref, wu_ref, wd_ref, wo_ref,
                   o_ref, wg_bf_ref, wu_bf_ref, wd_bf_ref, wo_bf_ref, m_ref, l_ref, acc_ref):
    i = pl.program_id(1)
    t = TOK_TILE
    for src, dst in ((wg_ref, wg_bf_ref), (wu_ref, wu_bf_ref), (wd_ref, wd_bf_ref),
                     (wo_ref, wo_bf_ref)):
        dst[...] = src[...].astype(_BF)

    def acc_rows(hd):
        return slice(hd * B_V_DIM, (hd + 1) * B_V_DIM)

    def diag_mask(s):
        krow = lax.broadcasted_iota(jnp.int32, (t, t), 0) // CHUNK
        qcol = lax.broadcasted_iota(jnp.int32, (t, t), 1) // CHUNK
        return jnp.where(krow <= qcol, s, NEG_INF)

    def sweep(kt, probs, commit):
        def scores(hd):
            k = k_ref[0, hd, pl.ds(pl.multiple_of(kt * t, t), t), :]
            return _dot(k, qt_ref[0, hd])

        pending = [scores(hd) for hd in range(QK_AHEAD)]
        for hd in range(B_HEADS):
            s = pending.pop(0)
            if hd + QK_AHEAD < B_HEADS:
                pending.append(scores(hd + QK_AHEAD))
            p = probs(hd, s)
            commit(hd, _dot(vt_ref[0, hd, kt], p.astype(_BF)), jnp.sum(p, axis=0, keepdims=True))

    def assign(hd, pv, ps):
        acc_ref[acc_rows(hd), :] = pv
        l_ref[hd:hd + 1, :] = ps

    def add(hd, pv, ps):
        acc_ref[acc_rows(hd), :] += pv
        l_ref[hd:hd + 1, :] += ps

    sweep(i, lambda hd, s: jnp.exp2(diag_mask(s)), assign)
    safe = jnp.min(l_ref[...]) > SAFE_DENOM

    @pl.when(safe)
    def _():
        def full_tile(kt, carry):
            sweep(kt, lambda hd, s: jnp.exp2(s), add)
            return carry

        lax.fori_loop(0, i, full_tile, 0)

    @pl.when(jnp.logical_not(safe))
    def _():
        m_ref[...] = jnp.full(m_ref.shape, NEG_INF, _F32)
        l_ref[...] = jnp.zeros(l_ref.shape, _F32)
        acc_ref[...] = jnp.zeros(acc_ref.shape, _F32)

        def online_step(kt, masked):
            alphas = {}

            def probs(hd, s):
                if masked:
                    s = diag_mask(s)
                m = m_ref[hd:hd + 1, :]
                m_new = jnp.maximum(m, jnp.max(s, axis=0, keepdims=True))
                m_ref[hd:hd + 1, :] = m_new
                alphas[hd] = jnp.exp2(m - m_new)
                return jnp.exp2(s - m_new)

            def rescale_add(hd, pv, ps):
                acc_ref[acc_rows(hd), :] = alphas[hd] * acc_ref[acc_rows(hd), :] + pv
                l_ref[hd:hd + 1, :] = alphas[hd] * l_ref[hd:hd + 1, :] + ps

            sweep(kt, probs, rescale_add)

        def online_full(kt, carry):
            online_step(kt, masked=False)
            return carry

        lax.fori_loop(0, i, online_full, 0)
        online_step(i, masked=True)

    outs = [acc_ref[acc_rows(hd), :] / l_ref[hd:hd + 1, :] for hd in range(B_HEADS)]
    o_ref[0] = jnp.concatenate(outs, axis=0).T.astype(_BF)


def _post_kernel(x1_ref, oa_ref, ob_ref, wout_ref, g2_ref, wg_ref, wu_ref, wd_ref,
                 gf_ref, y_ref, act_ref):
    x2 = (x1_ref[...] + _dot(oa_ref[...], wout_ref[0:A_WIDTH, :])
          + _dot(ob_ref[...], wout_ref[A_WIDTH:A_WIDTH + B_WIDTH, :]))
    x3 = x2 + 0.5 * _norm_swiglu(x2, g2_ref[...], wg_ref, wu_ref, wd_ref, act_ref)
    y_ref[...] = _rms(x3, gf_ref[...])


def _rope_tables(s):
    half = B_ROPE // 2
    inv = (1.0 / (ROPE_THETA ** (np.arange(0, B_ROPE, 2, dtype=np.float32) / B_ROPE))).astype(np.float32)
    ang = np.arange(s, dtype=np.float32)[:, None] * inv[None, :]
    cos, sin = np.cos(ang).astype(np.float32), np.sin(ang).astype(np.float32)
    ck = np.zeros((s, LANES), np.float32)
    sk = np.zeros((s, LANES), np.float32)
    ck[:, B_NOPE:B_NOPE + half] = cos
    ck[:, B_NOPE + half:B_QK_DIM] = cos
    sk[:, B_NOPE:B_NOPE + half] = -sin
    sk[:, B_NOPE + half:B_QK_DIM] = sin
    return (jnp.asarray(np.ascontiguousarray(cos.T)), jnp.asarray(np.ascontiguousarray(sin.T)),
            jnp.asarray(ck), jnp.asarray(sk))


def _const_spec(shape):
    n = len(shape)
    return pl.BlockSpec(shape, lambda *_: (0,) * n, pipeline_mode=pl.Buffered(1))


def kernel(x, ffn1_norm, ffn1_w_gate, ffn1_w_up, ffn1_w_down, mix_norm, w_in, a_q_norm, a_k_norm, a_rel_bias, b_q_lat_norm, b_w_uq, b_kv_lat_norm, b_w_ukv, b_q_nope_norm, b_q_rope_norm, b_k_nope_norm, b_k_rope_norm, w_out, ffn2_norm, ffn2_w_gate, ffn2_w_up, ffn2_w_down, final_norm):
    b, s, d = x.shape
    assert d == D_MODEL and s % TOK_TILE == 0 and s >= A_WIN and ffn1_norm.shape[0] == 1
    assert (b * s) % POST_TILE == 0 and s % (A_Q_TILE * A_TILES_PER_STEP) == 0
    tm = TOK_TILE
    nt = s // tm
    half = B_ROPE // 2

    def row(v):
        return v.astype(_F32)[None, :]

    w_in_l = w_in[0]
    o_va, o_cq = 2 * A_WIDTH, 3 * A_WIDTH
    o_ckv = o_cq + B_Q_LORA
    o_kr = o_ckv + B_KV_LORA
    w_fm = jnp.concatenate([w_in_l[:, 0:A_WIDTH], w_in_l[:, o_va:o_cq], w_in_l[:, A_WIDTH:o_va],
                            w_in_l[:, o_cq:o_ckv]], axis=1).T.astype(_BF)
    kr_cols = jnp.zeros((d, LANES), _F32).at[:, B_NOPE:B_QK_DIM].set(w_in_l[:, o_kr:o_kr + B_ROPE])
    w_tm = jnp.concatenate([w_in_l[:, o_ckv:o_kr], kr_cols], axis=1).astype(_BF)

    def col_tab(v):
        return jnp.broadcast_to(v.astype(_F32)[:, None], (v.shape[0], tm))

    gaq_tab = col_tab(a_q_norm[0] * (A_HEAD_DIM ** -0.5 * LOG2E))
    gak_tab = col_tab(a_k_norm[0])

    w_uq = (b_w_uq[0] * b_q_lat_norm[0][:, None]).reshape(B_Q_LORA, B_HEADS, B_QK_DIM)
    w_uq = jnp.pad(w_uq, ((0, 0), (0, 0), (0, HEAD_PAD - B_QK_DIM)))
    wq_t = w_uq.reshape(B_Q_LORA, B_HEADS * HEAD_PAD).T.astype(_BF)
    scale_b = (B_QK_DIM ** -0.5) * LOG2E
    gq_col = jnp.concatenate([b_q_nope_norm[0] * scale_b, b_q_rope_norm[0] * scale_b,
                              jnp.zeros((HEAD_PAD - B_QK_DIM,), _F32)])
    gq_tab = jnp.broadcast_to(gq_col[:, None], (HEAD_PAD, tm)).astype(_F32)
    q_norm2 = B_NOPE * jnp.max(b_q_nope_norm[0] ** 2) + B_ROPE * jnp.max(b_q_rope_norm[0] ** 2)
    k_norm2 = B_NOPE * jnp.max(b_k_nope_norm[0] ** 2) + B_ROPE * jnp.max(b_k_rope_norm[0] ** 2)
    bound_b = (scale_b * jnp.sqrt(q_norm2 * k_norm2) * BOUND_SLACK).astype(_BF).astype(_F32)
    qpad_tab = jnp.zeros((HEAD_PAD - B_QK_DIM, tm), _F32).at[0, :].set(-bound_b)
    kone = jnp.zeros((1, LANES), _F32).at[0, B_QK_DIM].set(1.0)

    cos_t, sin_t, ck, sk = _rope_tables(s)

    w_ukv = b_w_ukv[0].reshape(B_KV_LORA, B_HEADS, B_NOPE + B_V_DIM)
    wvk_t = jnp.concatenate([w_ukv[..., B_NOPE:].reshape(B_KV_LORA, B_WIDTH),
                             w_ukv[..., :B_NOPE].reshape(B_KV_LORA, B_HEADS * B_NOPE)],
                            axis=1).T.astype(_BF)
    gkn_tab = col_tab(b_k_nope_norm[0])
    gkr = row(jnp.concatenate([jnp.zeros((B_NOPE,), _F32), b_k_rope_norm[0],
                               jnp.zeros((LANES - B_QK_DIM,), _F32)]))

    bound_a = ((A_HEAD_DIM ** -0.5 * LOG2E) * A_HEAD_DIM * jnp.max(jnp.abs(a_q_norm[0]))
               * jnp.max(jnp.abs(a_k_norm[0])) * BOUND_SLACK)
    bias_tiles = _rel_bias_tiles(a_rel_bias[0], bound_a)

    tok = lambda bi, j: (bi, j, 0)
    pre_in_specs = [
        pl.BlockSpec((None, tm, d), tok),
        _const_spec((1, d)), _const_spec((d, D_FF)), _const_spec((d, D_FF)), _const_spec((D_FF, d)),
        _const_spec((1, d)), _const_spec((FM_ROWS, d)), _const_spec((d, TM_COLS)),
        _const_spec((A_HEAD_DIM, tm)), _const_spec((A_HEAD_DIM, tm)),
        _const_spec((B_HEADS * HEAD_PAD, B_Q_LORA)), _const_spec((HEAD_PAD, tm)),
        _const_spec((HEAD_PAD - B_QK_DIM, tm)),
        pl.BlockSpec((half, tm), lambda bi, j: (0, j)), pl.BlockSpec((half, tm), lambda bi, j: (0, j)),
        _const_spec((1, B_KV_LORA)), _const_spec((B_WIDTH + B_HEADS * B_NOPE, B_KV_LORA)),
        _const_spec((B_NOPE, tm)),
        _const_spec((1, LANES)), _const_spec((1, LANES)),
        pl.BlockSpec((tm, LANES), lambda bi, j: (j, 0)), pl.BlockSpec((tm, LANES), lambda bi, j: (j, 0)),
    ]
    pre_out_shapes = [
        jax.ShapeDtypeStruct((b, s, d), _F32),
        jax.ShapeDtypeStruct((b, s // A_Q_TILE, A_WIDTH, A_Q_TILE), _BF),
        jax.ShapeDtypeStruct((b, s, A_WIDTH), _BF),
        jax.ShapeDtypeStruct((b, A_HEADS, s // A_Q_TILE, A_HEAD_DIM, A_Q_TILE), _BF),
        jax.ShapeDtypeStruct((b, B_HEADS, HEAD_PAD, s), _BF),
        jax.ShapeDtypeStruct((b, B_HEADS, s, HEAD_PAD), _BF),
        jax.ShapeDtypeStruct((b, B_HEADS, nt, B_V_DIM, tm), _BF),
    ]
    pre_out_specs = [
        pl.BlockSpec((None, tm, d), tok),
        pl.BlockSpec((1, tm // A_Q_TILE, A_WIDTH, A_Q_TILE), lambda bi, j: (bi, j, 0, 0)),
        pl.BlockSpec((1, tm, A_WIDTH), tok),
        pl.BlockSpec((1, A_HEADS, tm // A_Q_TILE, A_HEAD_DIM, A_Q_TILE),
                     lambda bi, j: (bi, 0, j, 0, 0)),
        pl.BlockSpec((1, B_HEADS, HEAD_PAD, tm), lambda bi, j: (bi, 0, 0, j)),
        pl.BlockSpec((1, B_HEADS, tm, HEAD_PAD), lambda bi, j: (bi, 0, j, 0)),
        pl.BlockSpec((1, B_HEADS, 1, B_V_DIM, tm), lambda bi, j: (bi, 0, j, 0, 0)),
    ]
    x1, qat, ka, vat, qbt, kb, vbt = pl.pallas_call(
        _pre_kernel,
        grid=(b, nt),
        in_specs=pre_in_specs,
        out_specs=pre_out_specs,
        out_shape=pre_out_shapes,
        scratch_shapes=[pltpu.VMEM((tm, D_FF), _BF)],
        compiler_params=pltpu.CompilerParams(
            dimension_semantics=("arbitrary", "arbitrary"), vmem_limit_bytes=VMEM_LIMIT),
        name="pre",
    )(x, row(ffn1_norm[0]), ffn1_w_gate[0].astype(_BF), ffn1_w_up[0].astype(_BF),
      ffn1_w_down[0].astype(_BF), row(mix_norm[0]), w_fm, w_tm,
      gaq_tab, gak_tab,
      wq_t, gq_tab, qpad_tab, cos_t, sin_t,
      row(b_kv_lat_norm[0]), wvk_t, gkn_tab,
      gkr, kone, ck, sk)

    na = s // A_Q_TILE
    out_a = pl.pallas_call(
        _attn_a_kernel,
        grid=(b, na // A_TILES_PER_STEP),
        in_specs=[
            pl.BlockSpec((1, A_TILES_PER_STEP, A_WIDTH, A_Q_TILE), lambda bi, j: (bi, j, 0, 0)),
            pl.BlockSpec((1, s, A_WIDTH), lambda bi, j: (bi, 0, 0)),
            pl.BlockSpec((1, A_HEADS, na, A_HEAD_DIM, A_Q_TILE), lambda bi, j: (bi, 0, 0, 0, 0)),
            _const_spec((A_HEADS, A_BIAS_TILES, LANES, A_Q_TILE)),
        ],
        out_specs=pl.BlockSpec((1, A_TILES_PER_STEP * A_Q_TILE, A_WIDTH), tok),
        out_shape=jax.ShapeDtypeStruct((b, s, A_WIDTH), _BF),
        compiler_params=pltpu.CompilerParams(
            dimension_semantics=("arbitrary", "arbitrary"), vmem_limit_bytes=VMEM_LIMIT),
        name="attn_a",
    )(qat, ka, vat, bias_tiles)

    steps = b * nt
    wrows = d // steps
    assert d % steps == 0 and wrows % BF16_SUBLANES == 0
    wslab = lambda bi, j: (bi * nt + j, 0)
    post_w = [ffn2_w_gate[0], ffn2_w_up[0], ffn2_w_down[0].reshape(d, D_FF), w_out[0]]
    out_b, wg2, wu2, wd2, wo = pl.pallas_call(
        _attn_b_kernel,
        grid=(b, nt),
        in_specs=[
            pl.BlockSpec((1, B_HEADS, HEAD_PAD, tm), lambda bi, j: (bi, 0, 0, j)),
            pl.BlockSpec((1, B_HEADS, s, HEAD_PAD), lambda bi, j: (bi, 0, 0, 0)),
            pl.BlockSpec((1, B_HEADS, nt, B_V_DIM, tm), lambda bi, j: (bi, 0, 0, 0, 0)),
        ] + [pl.BlockSpec((wrows, w.shape[1]), wslab) for w in post_w],
        out_specs=[pl.BlockSpec((1, tm, B_WIDTH), tok)]
        + [pl.BlockSpec((wrows, w.shape[1]), wslab) for w in post_w],
        out_shape=[jax.ShapeDtypeStruct((b, s, B_WIDTH), _BF)]
        + [jax.ShapeDtypeStruct(w.shape, _BF) for w in post_w],
        scratch_shapes=[pltpu.VMEM((B_HEADS, tm), _F32), pltpu.VMEM((B_HEADS, tm), _F32),
                        pltpu.VMEM((B_WIDTH, tm), _F32)],
        compiler_params=pltpu.CompilerParams(
            dimension_semantics=("arbitrary", "arbitrary"), vmem_limit_bytes=VMEM_LIMIT),
        name="attn_b",
    )(qbt, kb, vbt, *post_w)
    wd2 = wd2.reshape(D_FF, d)

    n = b * s
    tp = POST_TILE
    flat = lambda j: (j, 0)
    y = pl.pallas_call(
        _post_kernel,
        grid=(n // tp,),
        in_specs=[
            pl.BlockSpec((tp, d), flat),
            pl.BlockSpec((tp, A_WIDTH), flat),
            pl.BlockSpec((tp, B_WIDTH), flat),
            _const_spec((A_WIDTH + B_WIDTH, d)),
            _const_spec((1, d)), _const_spec((d, D_FF)), _const_spec((d, D_FF)), _const_spec((D_FF, d)),
            _const_spec((1, d)),
        ],
        out_specs=pl.BlockSpec((tp, d), flat),
        out_shape=jax.ShapeDtypeStruct((n, d), _F32),
        scratch_shapes=[pltpu.VMEM((tp, D_FF), _BF)],
        compiler_params=pltpu.CompilerParams(
            dimension_semantics=("arbitrary",), vmem_limit_bytes=VMEM_LIMIT),
        name="post",
    )(x1.reshape(n, d), out_a.reshape(n, A_WIDTH), out_b.reshape(n, B_WIDTH),
      wo, row(ffn2_norm[0]), wg2, wu2, wd2, row(final_norm[0]))
    return y.reshape(b, s, d)
```

```python
import jax
import jax.numpy as jnp
import numpy as np
from jax import lax
from jax.experimental import pallas as pl
from jax.experimental.pallas import tpu as pltpu

D_MODEL = 1024
D_FF = 2816
CHUNK = 64
A_HEADS = 8
A_HEAD_DIM = 64
A_LEFT_CHUNKS = 8
A_MAX_REL = 128
A_WIDTH = A_HEADS * A_HEAD_DIM
B_HEADS = 8
B_Q_LORA = 256
B_KV_LORA = 128
B_NOPE = 64
B_ROPE = 32
B_V_DIM = 64
B_QK_DIM = B_NOPE + B_ROPE
B_WIDTH = B_HEADS * B_V_DIM
ROPE_THETA = 10000.0
EPS = 1e-6
NEG_INF = -1e30
LOG2E = 1.4426950408889634
BOUND_SLACK = 1.0 + 2.0 ** -5
SAFE_DENOM = 2.0 ** -80

LANES = 128
BF16_SUBLANES = 16
TOK_TILE = 512
POST_TILE = 1024
FF_CHUNK = 256
QK_AHEAD = 2

A_Q_TILE = 256
A_TILES_PER_STEP = 2
A_LEFT = A_LEFT_CHUNKS * CHUNK
A_WIN = A_Q_TILE + A_LEFT
A_KEY_SPLITS = ((0, 384), (384, 768))
A_BIAS_ROWS = A_WIN + A_LEFT
A_BIAS_TILES = A_BIAS_ROWS // LANES
A_TAB_ZERO = A_BIAS_ROWS - A_LEFT + LANES
A_TAB_LEN = A_TAB_ZERO + A_LEFT + A_Q_TILE

HEAD_PAD = 128
ACC_ROWS = A_HEAD_DIM + BF16_SUBLANES
FM_ROWS = 3 * A_WIDTH + B_Q_LORA
TM_COLS = B_KV_LORA + LANES
VMEM_LIMIT = 60 * 1024 * 1024

_BF = jnp.bfloat16
_F32 = jnp.float32


def _dot(a, b):
    return jnp.dot(a, b, preferred_element_type=_F32)


def _dot_nt(a, b):
    return lax.dot_general(a, b, (((1,), (1,)), ((), ())), preferred_element_type=_F32)


def _rms(x, g):
    ms = jnp.mean(x * x, axis=-1, keepdims=True)
    return x * lax.rsqrt(ms + EPS) * g


def _col_rms(x_t):
    return x_t * lax.rsqrt(jnp.mean(x_t * x_t, axis=0, keepdims=True) + EPS)


def _norm_swiglu(x, gain, wg_ref, wu_ref, wd_ref, act_ref):
    h = (x * gain).astype(_BF)
    r = lax.rsqrt(jnp.mean(x * x, axis=-1, keepdims=True) + EPS)
    for c in range(D_FF // FF_CHUNK):
        sl = slice(c * FF_CHUNK, (c + 1) * FF_CHUNK)
        g = _dot(h, wg_ref[:, sl]) * r
        u = _dot(h, wu_ref[:, sl]) * r
        act_ref[:, sl] = (jax.nn.silu(g) * u).astype(_BF)
    return _dot(act_ref[...], wd_ref[...])


def _pre_kernel(x_ref, g1_ref, wg_ref, wu_ref, wd_ref, gmix_ref, wfm_ref, wtm_ref,
                gaq_ref, gak_ref,
                wqt_ref, gqtab_ref, qpad_ref, cost_ref, sint_ref,
                gckv_ref, wvk_ref, gkn_ref,
                gkr_ref, kone_ref, ck_ref, sk_ref,
                x1_ref, qat_ref, ka_ref, vat_ref, qbt_ref, kb_ref, vbt_ref,
                act_ref):
    x = x_ref[...]
    tm = x.shape[0]
    x1 = x + 0.5 * _norm_swiglu(x, g1_ref[...], wg_ref, wu_ref, wd_ref, act_ref)
    x1_ref[...] = x1
    h2f = _rms(x1, gmix_ref[...])
    h2 = h2f.astype(_BF)

    tmj = _dot(h2, wtm_ref[...])
    ckvn = _rms(tmj[:, 0:B_KV_LORA], gckv_ref[...]).astype(_BF)
    cq_t = _dot_nt(wfm_ref[3 * A_WIDTH:FM_ROWS, :], h2)
    vk_t = _dot_nt(wvk_ref[...], ckvn)
    q_t = _dot(wqt_ref[...], _col_rms(cq_t).astype(_BF))
    ka_fm = _dot_nt(wfm_ref[2 * A_WIDTH:3 * A_WIDTH, :], h2)
    qa_fm = _dot_nt(wfm_ref[0:A_WIDTH, :], h2)
    va_fm = _dot_nt(wfm_ref[A_WIDTH:2 * A_WIDTH, :], h2)

    gaq = gaq_ref[...]
    gak = gak_ref[...]
    ka_t = []
    for hd in range(A_HEADS):
        rows = slice(hd * A_HEAD_DIM, (hd + 1) * A_HEAD_DIM)
        qa_hd = (_col_rms(qa_fm[rows]) * gaq).astype(_BF)
        for g in range(tm // A_Q_TILE):
            cols = slice(g * A_Q_TILE, (g + 1) * A_Q_TILE)
            qat_ref[0, g, rows, :] = qa_hd[:, cols]
            vat_ref[0, hd, g] = va_fm[rows, cols].astype(_BF)
        ka_t.append(_col_rms(ka_fm[rows]) * gak)
    ka_ref[0] = jnp.concatenate(ka_t, axis=0).T.astype(_BF)

    cos_t = cost_ref[...]
    sin_t = sint_ref[...]
    gq = gqtab_ref[...]
    half = B_ROPE // 2
    for hd in range(B_HEADS):
        r0 = hd * HEAD_PAD
        nope = _col_rms(q_t[r0:r0 + B_NOPE]) * gq[0:B_NOPE]
        rope = _col_rms(q_t[r0 + B_NOPE:r0 + B_QK_DIM]) * gq[B_NOPE:B_QK_DIM]
        r1 = rope[0:half]
        r2 = rope[half:B_ROPE]
        blk = jnp.concatenate(
            [nope, r1 * cos_t - r2 * sin_t, r1 * sin_t + r2 * cos_t, qpad_ref[...]], axis=0)
        qbt_ref[0, hd] = blk.astype(_BF)

    gkn = gkn_ref[...]
    kn_t = []
    for hd in range(B_HEADS):
        vbt_ref[0, hd, 0] = vk_t[hd * B_V_DIM:(hd + 1) * B_V_DIM].astype(_BF)
        kn_t.append(_col_rms(vk_t[B_WIDTH + hd * B_NOPE:B_WIDTH + (hd + 1) * B_NOPE]) * gkn)
    kn = jnp.concatenate(kn_t, axis=0).T
    kr = tmj[:, B_KV_LORA:TM_COLS]
    y = kr * lax.rsqrt(jnp.sum(kr * kr, axis=-1, keepdims=True) * (1.0 / B_ROPE) + EPS) * gkr_ref[...]
    lane = lax.broadcasted_iota(jnp.int32, y.shape, 1)
    swapped = jnp.where(lane < B_NOPE + half,
                        pltpu.roll(y, LANES - half, axis=1), pltpu.roll(y, half, axis=1))
    kpe = y * ck_ref[...] + swapped * sk_ref[...] + kone_ref[...]
    for hd in range(B_HEADS):
        pair_tile = kn[:, (hd // 2) * LANES:(hd // 2 + 1) * LANES]
        if hd % 2 == 1:
            pair_tile = pltpu.roll(pair_tile, B_NOPE, axis=1)
        kb_ref[0, hd] = jnp.where(lane < B_NOPE, pair_tile, kpe).astype(_BF)


def _bias_kernel(tab_ref, o_ref):
    u = lax.broadcasted_iota(jnp.int32, (LANES, A_Q_TILE), 0)
    qc = lax.broadcasted_iota(jnp.int32, (LANES, A_Q_TILE), 1) // CHUNK
    for tile in range(A_BIAS_TILES):
        a = A_TAB_ZERO + A_LEFT - LANES * tile
        w = jnp.concatenate([tab_ref[0, :, a:a + A_Q_TILE], tab_ref[0, :, a - A_Q_TILE:a]], axis=1)
        rolled = pltpu.roll(jnp.broadcast_to(w, (LANES, 2 * A_Q_TILE)), 0, 1,
                            stride=1, stride_axis=0)
        kc = (LANES * tile + u) // CHUNK - A_LEFT_CHUNKS
        valid = (kc <= qc) & (kc >= qc - A_LEFT_CHUNKS)
        o_ref[0, tile] = jnp.where(valid, rolled[:, :A_Q_TILE], NEG_INF)


def _rel_bias_tiles(rel_bias, qk_bound):
    pad_lo = A_TAB_ZERO - A_MAX_REL
    pad_hi = A_TAB_LEN - pad_lo - (2 * A_MAX_REL + 1)
    rb = rel_bias.astype(_F32) * LOG2E
    rb = rb - (qk_bound + jnp.max(rb, axis=1, keepdims=True))
    table = jnp.concatenate([jnp.broadcast_to(rb[:, :1], (A_HEADS, pad_lo)), rb,
                             jnp.broadcast_to(rb[:, -1:], (A_HEADS, pad_hi))], axis=1)
    return pl.pallas_call(
        _bias_kernel,
        grid=(A_HEADS,),
        in_specs=[pl.BlockSpec((1, 1, A_TAB_LEN), lambda h: (h, 0, 0))],
        out_specs=pl.BlockSpec((1, A_BIAS_TILES, LANES, A_Q_TILE), lambda h: (h, 0, 0, 0)),
        out_shape=jax.ShapeDtypeStruct((A_HEADS, A_BIAS_TILES, LANES, A_Q_TILE), _F32),
        name="rel_bias",
    )(table.reshape(A_HEADS, 1, A_TAB_LEN))


def _attn_a_kernel(qt_ref, k_ref, vt_ref, bias_ref, o_ref):
    def one_tile(sub, carry):
        _attn_a_tile(pl.program_id(1) * A_TILES_PER_STEP + sub, sub,
                     qt_ref, k_ref, vt_ref, bias_ref, o_ref)
        return carry

    lax.fori_loop(0, A_TILES_PER_STEP, one_tile, 0)


def _attn_a_tile(i, sub, qt_ref, k_ref, vt_ref, bias_ref, o_ref):
    left_tiles = A_LEFT // A_Q_TILE
    g0 = jnp.maximum(i - left_tiles, 0)
    start = pl.multiple_of(g0 * A_Q_TILE, A_Q_TILE)
    u0 = (left_tiles - jnp.minimum(i, left_tiles)) * (A_Q_TILE // LANES)
    zeros = jnp.zeros((A_HEAD_DIM, A_Q_TILE), _BF)
    ones_rows = jnp.ones((ACC_ROWS - A_HEAD_DIM, A_WIN), _BF)

    def scores(pair):
        q2 = qt_ref[0, sub, pair * LANES:(pair + 1) * LANES, :]
        q_even = jnp.concatenate([q2[:A_HEAD_DIM], zeros], axis=0)
        q_odd = jnp.concatenate([zeros, q2[A_HEAD_DIM:]], axis=0)
        qq = jnp.concatenate([q_even, q_odd], axis=1)
        parts = []
        for r0, r1 in A_KEY_SPLITS:
            k = k_ref[0, pl.ds(pl.multiple_of(start + r0, LANES), r1 - r0),
                      pair * LANES:(pair + 1) * LANES]
            parts.append(_dot(k, qq))
        return jnp.concatenate(parts, axis=0)

    def attend(subtract_max):
        n_pairs = A_HEADS // 2
        pending = [scores(0)]
        outs, denoms = [], []
        for pair in range(n_pairs):
            s = pending.pop(0)
            if pair + 1 < n_pairs:
                pending.append(scores(pair + 1))
            bias = jnp.concatenate(
                [jnp.concatenate([bias_ref[2 * pair + e, u0 + j] for j in range(A_WIN // LANES)],
                                 axis=0) for e in range(2)], axis=1)
            s = s + bias
            if subtract_max:
                s = s - jnp.max(s, axis=0, keepdims=True)
            p = jnp.exp2(s).astype(_BF)
            for e in range(2):
                hd = 2 * pair + e
                v = jnp.concatenate([vt_ref[0, hd, g0 + g] for g in range(A_WIN // A_Q_TILE)],
                                    axis=1)
                o = _dot(jnp.concatenate([v, ones_rows], axis=0),
                         p[:, e * A_Q_TILE:(e + 1) * A_Q_TILE])
                denoms.append(o[A_HEAD_DIM:A_HEAD_DIM + 1])
                outs.append(o[:A_HEAD_DIM] / denoms[-1])
        o_ref[0, pl.ds(pl.multiple_of(sub * A_Q_TILE, A_Q_TILE), A_Q_TILE), :] = (
            jnp.concatenate(outs, axis=0).T.astype(_BF))
        return jnp.min(jnp.concatenate(denoms, axis=0))

    smallest = attend(subtract_max=False)

    @pl.when(jnp.logical_not(smallest > SAFE_DENOM))
    def _():
        attend(subtract_max=True)


def _attn_b_kernel(qt_ref, k_ref, vt_ref, o_ref, m_ref, l_ref, acc_ref):
    i = pl.program_id(1)
    t = TOK_TILE

    def acc_rows(hd):
        return slice(hd * B_V_DIM, (hd + 1) * B_V_DIM)

    def diag_mask(s):
        krow = lax.broadcasted_iota(jnp.int32, (t, t), 0) // CHUNK
        qcol = lax.broadcasted_iota(jnp.int32, (t, t), 1) // CHUNK
        return jnp.where(krow <= qcol, s, NEG_INF)

    def sweep(kt, probs, commit):
        def scores(hd):
            k = k_ref[0, hd, pl.ds(pl.multiple_of(kt * t, t), t), :]
            return _dot(k, qt_ref[0, hd])

        pending = [scores(hd) for hd in range(QK_AHEAD)]
        for hd in range(B_HEADS):
            s = pending.pop(0)
            if hd + QK_AHEAD < B_HEADS:
                pending.append(scores(hd + QK_AHEAD))
            p = probs(hd, s)
            commit(hd, _dot(vt_ref[0, hd, kt], p.astype(_BF)), jnp.sum(p, axis=0, keepdims=True))

    def assign(hd, pv, ps):
        acc_ref[acc_rows(hd), :] = pv
        l_ref[hd:hd + 1, :] = ps

    def add(hd, pv, ps):
        acc_ref[acc_rows(hd), :] += pv
        l_ref[hd:hd + 1, :] += ps

    sweep(i, lambda hd, s: jnp.exp2(diag_mask(s)), assign)
    safe = jnp.min(l_ref[...]) > SAFE_DENOM

    @pl.when(safe)
    def _():
        def full_tile(kt, carry):
            sweep(kt, lambda hd, s: jnp.exp2(s), add)
            return carry

        lax.fori_loop(0, i, full_tile, 0)

    @pl.when(jnp.logical_not(safe))
    def _():
        m_ref[...] = jnp.full(m_ref.shape, NEG_INF, _F32)
        l_ref[...] = jnp.zeros(l_ref.shape, _F32)
        acc_ref[...] = jnp.zeros(acc_ref.shape, _F32)

        def online_step(kt, masked):
            alphas = {}

            def probs(hd, s):
                if masked:
                    s = diag_mask(s)
                m = m_ref[hd:hd + 1, :]
                m_new = jnp.maximum(m, jnp.max(s, axis=0, keepdims=True))
                m_ref[hd:hd + 1, :] = m_new
                alphas[hd] = jnp.exp2(m - m_new)
                return jnp.exp2(s - m_new)

            def rescale_add(hd, pv, ps):
                acc_ref[acc_rows(hd), :] = alphas[hd] * acc_ref[acc_rows(hd), :] + pv
                l_ref[hd:hd + 1, :] = alphas[hd] * l_ref[hd:hd + 1, :] + ps

            sweep(kt, probs, rescale_add)

        def online_full(kt, carry):
            online_step(kt, masked=False)
            return carry

        lax.fori_loop(0, i, online_full, 0)
        online_step(i, masked=True)

    outs = [acc_ref[acc_rows(hd), :] / l_ref[hd:hd + 1, :] for hd in range(B_HEADS)]
    o_ref[0] = jnp.concatenate(outs, axis=0).T.astype(_BF)


def _post_kernel(x1_ref, oa_ref, ob_ref, wout_ref, g2_ref, wg_ref, wu_ref, wd_ref,
                 gf_ref, y_ref, act_ref):
    x2 = (x1_ref[...] + _dot(oa_ref[...], wout_ref[0:A_WIDTH, :])
          + _dot(ob_ref[...], wout_ref[A_WIDTH:A_WIDTH + B_WIDTH, :]))
    x3 = x2 + 0.5 * _norm_swiglu(x2, g2_ref[...], wg_ref, wu_ref, wd_ref, act_ref)
    y_ref[...] = _rms(x3, gf_ref[...])


def _rope_tables(s):
    half = B_ROPE // 2
    inv = (1.0 / (ROPE_THETA ** (np.arange(0, B_ROPE, 2, dtype=np.float32) / B_ROPE))).astype(np.float32)
    ang = np.arange(s, dtype=np.float32)[:, None] * inv[None, :]
    cos, sin = np.cos(ang).astype(np.float32), np.sin(ang).astype(np.float32)
    ck = np.zeros((s, LANES), np.float32)
    sk = np.zeros((s, LANES), np.float32)
    ck[:, B_NOPE:B_NOPE + half] = cos
    ck[:, B_NOPE + half:B_QK_DIM] = cos
    sk[:, B_NOPE:B_NOPE + half] = -sin
    sk[:, B_NOPE + half:B_QK_DIM] = sin
    return (jnp.asarray(np.ascontiguousarray(cos.T)), jnp.asarray(np.ascontiguousarray(sin.T)),
            jnp.asarray(ck), jnp.asarray(sk))


def _const_spec(shape):
    n = len(shape)
    return pl.BlockSpec(shape, lambda *_: (0,) * n, pipeline_mode=pl.Buffered(1))


def kernel(x, ffn1_norm, ffn1_w_gate, ffn1_w_up, ffn1_w_down, mix_norm, w_in, a_q_norm, a_k_norm, a_rel_bias, b_q_lat_norm, b_w_uq, b_kv_lat_norm, b_w_ukv, b_q_nope_norm, b_q_rope_norm, b_k_nope_norm, b_k_rope_norm, w_out, ffn2_norm, ffn2_w_gate, ffn2_w_up, ffn2_w_down, final_norm):
    b, s, d = x.shape
    assert d == D_MODEL and s % TOK_TILE == 0 and s >= A_WIN and ffn1_norm.shape[0] == 1
    assert (b * s) % POST_TILE == 0 and s % (A_Q_TILE * A_TILES_PER_STEP) == 0
    tm = TOK_TILE
    nt = s // tm
    half = B_ROPE // 2

    def row(v):
        return v.astype(_F32)[None, :]

    w_in_l = w_in[0]
    o_va, o_cq = 2 * A_WIDTH, 3 * A_WIDTH
    o_ckv = o_cq + B_Q_LORA
    o_kr = o_ckv + B_KV_LORA
    w_fm = jnp.concatenate([w_in_l[:, 0:A_WIDTH], w_in_l[:, o_va:o_cq], w_in_l[:, A_WIDTH:o_va],
                            w_in_l[:, o_cq:o_ckv]], axis=1).T.astype(_BF)
    kr_cols = jnp.zeros((d, LANES), _F32).at[:, B_NOPE:B_QK_DIM].set(w_in_l[:, o_kr:o_kr + B_ROPE])
    w_tm = jnp.concatenate([w_in_l[:, o_ckv:o_kr], kr_cols], axis=1).astype(_BF)

    def col_tab(v):
        return jnp.broadcast_to(v.astype(_F32)[:, None], (v.shape[0], tm))

    gaq_tab = col_tab(a_q_norm[0] * (A_HEAD_DIM ** -0.5 * LOG2E))
    gak_tab = col_tab(a_k_norm[0])

    w_uq = (b_w_uq[0] * b_q_lat_norm[0][:, None]).reshape(B_Q_LORA, B_HEADS, B_QK_DIM)
    w_uq = jnp.pad(w_uq, ((0, 0), (0, 0), (0, HEAD_PAD - B_QK_DIM)))
    wq_t = w_uq.reshape(B_Q_LORA, B_HEADS * HEAD_PAD).T.astype(_BF)
    scale_b = (B_QK_DIM ** -0.5) * LOG2E
    gq_col = jnp.concatenate([b_q_nope_norm[0] * scale_b, b_q_rope_norm[0] * scale_b,
                              jnp.zeros((HEAD_PAD - B_QK_DIM,), _F32)])
    gq_tab = jnp.broadcast_to(gq_col[:, None], (HEAD_PAD, tm)).astype(_F32)
    q_norm2 = B_NOPE * jnp.max(b_q_nope_norm[0] ** 2) + B_ROPE * jnp.max(b_q_rope_norm[0] ** 2)
    k_norm2 = B_NOPE * jnp.max(b_k_nope_norm[0] ** 2) + B_ROPE * jnp.max(b_k_rope_norm[0] ** 2)
    bound_b = (scale_b * jnp.sqrt(q_norm2 * k_norm2) * BOUND_SLACK).astype(_BF).astype(_F32)
    qpad_tab = jnp.zeros((HEAD_PAD - B_QK_DIM, tm), _F32).at[0, :].set(-bound_b)
    kone = jnp.zeros((1, LANES), _F32).at[0, B_QK_DIM].set(1.0)

    cos_t, sin_t, ck, sk = _rope_tables(s)

    w_ukv = b_w_ukv[0].reshape(B_KV_LORA, B_HEADS, B_NOPE + B_V_DIM)
    wvk_t = jnp.concatenate([w_ukv[..., B_NOPE:].reshape(B_KV_LORA, B_WIDTH),
                             w_ukv[..., :B_NOPE].reshape(B_KV_LORA, B_HEADS * B_NOPE)],
                            axis=1).T.astype(_BF)
    gkn_tab = col_tab(b_k_nope_norm[0])
    gkr = row(jnp.concatenate([jnp.zeros((B_NOPE,), _F32), b_k_rope_norm[0],
                               jnp.zeros((LANES - B_QK_DIM,), _F32)]))

    bound_a = ((A_HEAD_DIM ** -0.5 * LOG2E) * A_HEAD_DIM * jnp.max(jnp.abs(a_q_norm[0]))
               * jnp.max(jnp.abs(a_k_norm[0])) * BOUND_SLACK)
    bias_tiles = _rel_bias_tiles(a_rel_bias[0], bound_a)

    tok = lambda bi, j: (bi, j, 0)
    pre_in_specs = [
        pl.BlockSpec((None, tm, d), tok),
        _const_spec((1, d)), _const_spec((d, D_FF)), _const_spec((d, D_FF)), _const_spec((D_FF, d)),
        _const_spec((1, d)), _const_spec((FM_ROWS, d)), _const_spec((d, TM_COLS)),
        _const_spec((A_HEAD_DIM, tm)), _const_spec((A_HEAD_DIM, tm)),
        _const_spec((B_HEADS * HEAD_PAD, B_Q_LORA)), _const_spec((HEAD_PAD, tm)),
        _const_spec((HEAD_PAD - B_QK_DIM, tm)),
        pl.BlockSpec((half, tm), lambda bi, j: (0, j)), pl.BlockSpec((half, tm), lambda bi, j: (0, j)),
        _const_spec((1, B_KV_LORA)), _const_spec((B_WIDTH + B_HEADS * B_NOPE, B_KV_LORA)),
        _const_spec((B_NOPE, tm)),
        _const_spec((1, LANES)), _const_spec((1, LANES)),
        pl.BlockSpec((tm, LANES), lambda bi, j: (j, 0)), pl.BlockSpec((tm, LANES), lambda bi, j: (j, 0)),
    ]
    pre_out_shapes = [
        jax.ShapeDtypeStruct((b, s, d), _F32),
        jax.ShapeDtypeStruct((b, s // A_Q_TILE, A_WIDTH, A_Q_TILE), _BF),
        jax.ShapeDtypeStruct((b, s, A_WIDTH), _BF),
        jax.ShapeDtypeStruct((b, A_HEADS, s // A_Q_TILE, A_HEAD_DIM, A_Q_TILE), _BF),
        jax.ShapeDtypeStruct((b, B_HEADS, HEAD_PAD, s), _BF),
        jax.ShapeDtypeStruct((b, B_HEADS, s, HEAD_PAD), _BF),
        jax.ShapeDtypeStruct((b, B_HEADS, nt, B_V_DIM, tm), _BF),
    ]
    pre_out_specs = [
        pl.BlockSpec((None, tm, d), tok),
        pl.BlockSpec((1, tm // A_Q_TILE, A_WIDTH, A_Q_TILE), lambda bi, j: (bi, j, 0, 0)),
        pl.BlockSpec((1, tm, A_WIDTH), tok),
        pl.BlockSpec((1, A_HEADS, tm // A_Q_TILE, A_HEAD_DIM, A_Q_TILE),
                     lambda bi, j: (bi, 0, j, 0, 0)),
        pl.BlockSpec((1, B_HEADS, HEAD_PAD, tm), lambda bi, j: (bi, 0, 0, j)),
        pl.BlockSpec((1, B_HEADS, tm, HEAD_PAD), lambda bi, j: (bi, 0, j, 0)),
        pl.BlockSpec((1, B_HEADS, 1, B_V_DIM, tm), lambda bi, j: (bi, 0, j, 0, 0)),
    ]
    x1, qat, ka, vat, qbt, kb, vbt = pl.pallas_call(
        _pre_kernel,
        grid=(b, nt),
        in_specs=pre_in_specs,
        out_specs=pre_out_specs,
        out_shape=pre_out_shapes,
        scratch_shapes=[pltpu.VMEM((tm, D_FF), _BF)],
        compiler_params=pltpu.CompilerParams(
            dimension_semantics=("arbitrary", "arbitrary"), vmem_limit_bytes=VMEM_LIMIT),
        name="pre",
    )(x, row(ffn1_norm[0]), ffn1_w_gate[0].astype(_BF), ffn1_w_up[0].astype(_BF),
      ffn1_w_down[0].astype(_BF), row(mix_norm[0]), w_fm, w_tm,
      gaq_tab, gak_tab,
      wq_t, gq_tab, qpad_tab, cos_t, sin_t,
      row(b_kv_lat_norm[0]), wvk_t, gkn_tab,
      gkr, kone, ck, sk)

    na = s // A_Q_TILE
    out_a = pl.pallas_call(
        _attn_a_kernel,
        grid=(b, na // A_TILES_PER_STEP),
        in_specs=[
            pl.BlockSpec((1, A_TILES_PER_STEP, A_WIDTH, A_Q_TILE), lambda bi, j: (bi, j, 0, 0)),
            pl.BlockSpec((1, s, A_WIDTH), lambda bi, j: (bi, 0, 0)),
            pl.BlockSpec((1, A_HEADS, na, A_HEAD_DIM, A_Q_TILE), lambda bi, j: (bi, 0, 0, 0, 0)),
            _const_spec((A_HEADS, A_BIAS_TILES, LANES, A_Q_TILE)),
        ],
        out_specs=pl.BlockSpec((1, A_TILES_PER_STEP * A_Q_TILE, A_WIDTH), tok),
        out_shape=jax.ShapeDtypeStruct((b, s, A_WIDTH), _BF),
        compiler_params=pltpu.CompilerParams(
            dimension_semantics=("arbitrary", "arbitrary"), vmem_limit_bytes=VMEM_LIMIT),
        name="attn_a",
    )(qat, ka, vat, bias_tiles)

    out_b = pl.pallas_call(
        _attn_b_kernel,
        grid=(b, nt),
        in_specs=[
            pl.BlockSpec((1, B_HEADS, HEAD_PAD, tm), lambda bi, j: (bi, 0, 0, j)),
            pl.BlockSpec((1, B_HEADS, s, HEAD_PAD), lambda bi, j: (bi, 0, 0, 0)),
            pl.BlockSpec((1, B_HEADS, nt, B_V_DIM, tm), lambda bi, j: (bi, 0, 0, 0, 0)),
        ],
        out_specs=pl.BlockSpec((1, tm, B_WIDTH), tok),
        out_shape=jax.ShapeDtypeStruct((b, s, B_WIDTH), _BF),
        scratch_shapes=[pltpu.VMEM((B_HEADS, tm), _F32), pltpu.VMEM((B_HEADS, tm), _F32),
                        pltpu.VMEM((B_WIDTH, tm), _F32)],
        compiler_params=pltpu.CompilerParams(
            dimension_semantics=("arbitrary", "arbitrary"), vmem_limit_bytes=VMEM_LIMIT),
        name="attn_b",
    )(qbt, kb, vbt)

    n = b * s
    tp = POST_TILE
    flat = lambda j: (j, 0)
    y = pl.pallas_call(
        _post_kernel,
        grid=(n // tp,),
        in_specs=[
            pl.BlockSpec((tp, d), flat),
            pl.BlockSpec((tp, A_WIDTH), flat),
            pl.BlockSpec((tp, B_WIDTH), flat),
            _const_spec((A_WIDTH + B_WIDTH, d)),
            _const_spec((1, d)), _const_spec((d, D_FF)), _const_spec((d, D_FF)), _const_spec((D_FF, d)),
            _const_spec((1, d)),
        ],
        out_specs=pl.BlockSpec((tp, d), flat),
        out_shape=jax.ShapeDtypeStruct((n, d), _F32),
        scratch_shapes=[pltpu.VMEM((tp, D_FF), _BF)],
        compiler_params=pltpu.CompilerParams(
            dimension_semantics=("arbitrary",), vmem_limit_bytes=VMEM_LIMIT),
        name="post",
    )(x1.reshape(n, d), out_a.reshape(n, A_WIDTH), out_b.reshape(n, B_WIDTH),
      w_out[0].astype(_BF), row(ffn2_norm[0]), ffn2_w_gate[0].astype(_BF),
      ffn2_w_up[0].astype(_BF), ffn2_w_down[0].astype(_BF), row(final_norm[0]))
    return y.reshape(b, s, d)
```

```python
import jax
import jax.numpy as jnp
import numpy as np
from jax import lax
from jax.experimental import pallas as pl
from jax.experimental.pallas import tpu as pltpu

D_MODEL = 1024
D_FF = 2816
CHUNK = 64
A_HEADS = 8
A_HEAD_DIM = 64
A_LEFT_CHUNKS = 8
A_MAX_REL = 128
A_WIDTH = A_HEADS * A_HEAD_DIM
B_HEADS = 8
B_Q_LORA = 256
B_KV_LORA = 128
B_NOPE = 64
B_ROPE = 32
B_V_DIM = 64
B_QK_DIM = B_NOPE + B_ROPE
B_WIDTH = B_HEADS * B_V_DIM
ROPE_THETA = 10000.0
EPS = 1e-6
NEG_INF = -1e30
LOG2E = 1.4426950408889634
BOUND_SLACK = 1.0 + 2.0 ** -5
SAFE_DENOM = 2.0 ** -80

LANES = 128
BF16_SUBLANES = 16
TOK_TILE = 512
POST_TILE = 1024
FF_CHUNK = 256
PRE_ROW_BLOCKS = 2
QK_AHEAD = 1

A_Q_TILE = 256
A_TILES_PER_STEP = 4
A_LEFT = A_LEFT_CHUNKS * CHUNK
A_WIN = A_Q_TILE + A_LEFT
A_KEY_SPLITS = ((0, 384), (384, 768))
A_BIAS_ROWS = A_WIN + A_LEFT
A_BIAS_TILES = A_BIAS_ROWS // LANES
A_TAB_ZERO = A_BIAS_ROWS - A_LEFT + LANES
A_TAB_LEN = A_TAB_ZERO + A_LEFT + A_Q_TILE

HEAD_PAD = 128
ACC_ROWS = A_HEAD_DIM + BF16_SUBLANES
FM_ROWS = 3 * A_WIDTH + B_Q_LORA
TM_COLS = B_KV_LORA + LANES
VMEM_LIMIT = 60 * 1024 * 1024

_BF = jnp.bfloat16
_F32 = jnp.float32


def _dot(a, b):
    return jnp.dot(a, b, preferred_element_type=_F32)


def _dot_nt(a, b):
    return lax.dot_general(a, b, (((1,), (1,)), ((), ())), preferred_element_type=_F32)


def _rms(x, g):
    ms = jnp.mean(x * x, axis=-1, keepdims=True)
    return x * lax.rsqrt(ms + EPS) * g


def _col_rms(x_t):
    return x_t * lax.rsqrt(jnp.mean(x_t * x_t, axis=0, keepdims=True) + EPS)


def _norm_swiglu(x, gain, wg_ref, wu_ref, wd_ref, act_ref, row_blocks=1):
    h = (x * gain).astype(_BF)
    r = lax.rsqrt(jnp.mean(x * x, axis=-1, keepdims=True) + EPS)
    for c in range(D_FF // FF_CHUNK):
        sl = slice(c * FF_CHUNK, (c + 1) * FF_CHUNK)
        g = _dot(h, wg_ref[:, sl]) * r
        u = _dot(h, wu_ref[:, sl]) * r
        act_ref[:, sl] = (jax.nn.silu(g) * u).astype(_BF)
    if row_blocks == 1:
        return _dot(act_ref[...], wd_ref[...])
    step = x.shape[0] // row_blocks
    return [_dot(act_ref[r:r + step, :], wd_ref[...]) for r in range(0, x.shape[0], step)]


def _pre_kernel(x_ref, g1_ref, wg_ref, wu_ref, wd_ref, gmix_ref, wfm_ref, wtm_ref,
                gaq_ref, gak_ref,
                wqt_ref, gqtab_ref, qpad_ref, cost_ref, sint_ref,
                gckv_ref, wvk_ref, gkn_ref,
                gkr_ref, kone_ref, ck_ref, sk_ref,
                x1_ref, qat_ref, ka_ref, vat_ref, qbt_ref, kb_ref, vbt_ref,
                act_ref):
    x = x_ref[...]
    tm = x.shape[0]
    ffn = _norm_swiglu(x, g1_ref[...], wg_ref, wu_ref, wd_ref, act_ref, row_blocks=PRE_ROW_BLOCKS)
    step = tm // PRE_ROW_BLOCKS
    h2_blocks = []
    for blk, f in enumerate(ffn):
        rows = slice(blk * step, (blk + 1) * step)
        x1 = x[rows] + 0.5 * f
        x1_ref[rows, :] = x1
        h2_blocks.append(_rms(x1, gmix_ref[...]).astype(_BF))
    h2 = jnp.concatenate(h2_blocks, axis=0)

    tmj = _dot(h2, wtm_ref[...])
    ckvn = _rms(tmj[:, 0:B_KV_LORA], gckv_ref[...]).astype(_BF)
    cq_t = _dot_nt(wfm_ref[3 * A_WIDTH:FM_ROWS, :], h2)
    vk_t = _dot_nt(wvk_ref[...], ckvn)
    q_t = _dot(wqt_ref[...], _col_rms(cq_t).astype(_BF))
    ka_fm = _dot_nt(wfm_ref[2 * A_WIDTH:3 * A_WIDTH, :], h2)
    qa_fm = _dot_nt(wfm_ref[0:A_WIDTH, :], h2)
    va_fm = _dot_nt(wfm_ref[A_WIDTH:2 * A_WIDTH, :], h2)

    gaq = gaq_ref[...]
    gak = gak_ref[...]
    ka_t = []
    for hd in range(A_HEADS):
        rows = slice(hd * A_HEAD_DIM, (hd + 1) * A_HEAD_DIM)
        qa_hd = (_col_rms(qa_fm[rows]) * gaq).astype(_BF)
        for g in range(tm // A_Q_TILE):
            cols = slice(g * A_Q_TILE, (g + 1) * A_Q_TILE)
            qat_ref[0, g, rows, :] = qa_hd[:, cols]
            vat_ref[0, hd, g] = va_fm[rows, cols].astype(_BF)
        ka_t.append(_col_rms(ka_fm[rows]) * gak)
    ka_ref[0] = jnp.concatenate(ka_t, axis=0).T.astype(_BF)

    cos_t = cost_ref[...]
    sin_t = sint_ref[...]
    gq = gqtab_ref[...]
    half = B_ROPE // 2
    for hd in range(B_HEADS):
        r0 = hd * HEAD_PAD
        nope = _col_rms(q_t[r0:r0 + B_NOPE]) * gq[0:B_NOPE]
        rope = _col_rms(q_t[r0 + B_NOPE:r0 + B_QK_DIM]) * gq[B_NOPE:B_QK_DIM]
        r1 = rope[0:half]
        r2 = rope[half:B_ROPE]
        blk = jnp.concatenate(
            [nope, r1 * cos_t - r2 * sin_t, r1 * sin_t + r2 * cos_t, qpad_ref[...]], axis=0)
        qbt_ref[0, hd] = blk.astype(_BF)

    gkn = gkn_ref[...]
    kn_t = []
    for hd in range(B_HEADS):
        vbt_ref[0, hd, 0] = vk_t[hd * B_V_DIM:(hd + 1) * B_V_DIM].astype(_BF)
        kn_t.append(_col_rms(vk_t[B_WIDTH + hd * B_NOPE:B_WIDTH + (hd + 1) * B_NOPE]) * gkn)
    kn = jnp.concatenate(kn_t, axis=0).T
    kr = tmj[:, B_KV_LORA:TM_COLS]
    y = kr * lax.rsqrt(jnp.sum(kr * kr, axis=-1, keepdims=True) * (1.0 / B_ROPE) + EPS) * gkr_ref[...]
    lane = lax.broadcasted_iota(jnp.int32, y.shape, 1)
    swapped = jnp.where(lane < B_NOPE + half,
                        pltpu.roll(y, LANES - half, axis=1), pltpu.roll(y, half, axis=1))
    kpe = y * ck_ref[...] + swapped * sk_ref[...] + kone_ref[...]
    for hd in range(B_HEADS):
        pair_tile = kn[:, (hd // 2) * LANES:(hd // 2 + 1) * LANES]
        if hd % 2 == 1:
            pair_tile = pltpu.roll(pair_tile, B_NOPE, axis=1)
        kb_ref[0, hd] = jnp.where(lane < B_NOPE, pair_tile, kpe).astype(_BF)


def _bias_kernel(tab_ref, o_ref):
    u = lax.broadcasted_iota(jnp.int32, (LANES, A_Q_TILE), 0)
    qc = lax.broadcasted_iota(jnp.int32, (LANES, A_Q_TILE), 1) // CHUNK
    for tile in range(A_BIAS_TILES):
        a = A_TAB_ZERO + A_LEFT - LANES * tile
        w = jnp.concatenate([tab_ref[0, :, a:a + A_Q_TILE], tab_ref[0, :, a - A_Q_TILE:a]], axis=1)
        rolled = pltpu.roll(jnp.broadcast_to(w, (LANES, 2 * A_Q_TILE)), 0, 1,
                            stride=1, stride_axis=0)
        kc = (LANES * tile + u) // CHUNK - A_LEFT_CHUNKS
        valid = (kc <= qc) & (kc >= qc - A_LEFT_CHUNKS)
        o_ref[0, tile] = jnp.where(valid, rolled[:, :A_Q_TILE], NEG_INF)


def _rel_bias_tiles(rel_bias, qk_bound):
    pad_lo = A_TAB_ZERO - A_MAX_REL
    pad_hi = A_TAB_LEN - pad_lo - (2 * A_MAX_REL + 1)
    rb = rel_bias.astype(_F32) * LOG2E
    rb = rb - (qk_bound + jnp.max(rb, axis=1, keepdims=True))
    table = jnp.concatenate([jnp.broadcast_to(rb[:, :1], (A_HEADS, pad_lo)), rb,
                             jnp.broadcast_to(rb[:, -1:], (A_HEADS, pad_hi))], axis=1)
    return pl.pallas_call(
        _bias_kernel,
        grid=(A_HEADS,),
        in_specs=[pl.BlockSpec((1, 1, A_TAB_LEN), lambda h: (h, 0, 0))],
        out_specs=pl.BlockSpec((1, A_BIAS_TILES, LANES, A_Q_TILE), lambda h: (h, 0, 0, 0)),
        out_shape=jax.ShapeDtypeStruct((A_HEADS, A_BIAS_TILES, LANES, A_Q_TILE), _F32),
        name="rel_bias",
    )(table.reshape(A_HEADS, 1, A_TAB_LEN))


def _attn_a_kernel(qt_ref, k_ref, vt_ref, bias_ref, o_ref):
    def one_tile(sub, carry):
        _attn_a_tile(pl.program_id(1) * A_TILES_PER_STEP + sub, sub,
                     qt_ref, k_ref, vt_ref, bias_ref, o_ref)
        return carry

    lax.fori_loop(0, A_TILES_PER_STEP, one_tile, 0)


def _attn_a_tile(i, sub, qt_ref, k_ref, vt_ref, bias_ref, o_ref):
    left_tiles = A_LEFT // A_Q_TILE
    g0 = jnp.maximum(i - left_tiles, 0)
    start = pl.multiple_of(g0 * A_Q_TILE, A_Q_TILE)
    u0 = (left_tiles - jnp.minimum(i, left_tiles)) * (A_Q_TILE // LANES)
    zeros = jnp.zeros((A_HEAD_DIM, A_Q_TILE), _BF)
    ones_rows = jnp.ones((ACC_ROWS - A_HEAD_DIM, A_WIN), _BF)

    def scores(pair):
        q2 = qt_ref[0, sub, pair * LANES:(pair + 1) * LANES, :]
        q_even = jnp.concatenate([q2[:A_HEAD_DIM], zeros], axis=0)
        q_odd = jnp.concatenate([zeros, q2[A_HEAD_DIM:]], axis=0)
        qq = jnp.concatenate([q_even, q_odd], axis=1)
        parts = []
        for r0, r1 in A_KEY_SPLITS:
            k = k_ref[0, pl.ds(pl.multiple_of(start + r0, LANES), r1 - r0),
                      pair * LANES:(pair + 1) * LANES]
            parts.append(_dot(k, qq))
        return jnp.concatenate(parts, axis=0)

    def attend(subtract_max):
        n_pairs = A_HEADS // 2
        pending = [scores(0)]
        outs, denoms = [], []
        for pair in range(n_pairs):
            s = pending.pop(0)
            if pair + 1 < n_pairs:
                pending.append(scores(pair + 1))
            bias = jnp.concatenate(
                [jnp.concatenate([bias_ref[2 * pair + e, u0 + j] for j in range(A_WIN // LANES)],
                                 axis=0) for e in range(2)], axis=1)
            s = s + bias
            if subtract_max:
                s = s - jnp.max(s, axis=0, keepdims=True)
            p = jnp.exp2(s).astype(_BF)
            for e in range(2):
                hd = 2 * pair + e
                v = jnp.concatenate([vt_ref[0, hd, g0 + g] for g in range(A_WIN // A_Q_TILE)],
                                    axis=1)
                o = _dot(jnp.concatenate([v, ones_rows], axis=0),
                         p[:, e * A_Q_TILE:(e + 1) * A_Q_TILE])
                denoms.append(o[A_HEAD_DIM:A_HEAD_DIM + 1])
                outs.append(o[:A_HEAD_DIM] / denoms[-1])
        o_ref[0, pl.ds(pl.multiple_of(sub * A_Q_TILE, A_Q_TILE), A_Q_TILE), :] = (
            jnp.concatenate(outs, axis=0).T.astype(_BF))
        return jnp.min(jnp.concatenate(denoms, axis=0))

    smallest = attend(subtract_max=False)

    @pl.when(jnp.logical_not(smallest > SAFE_DENOM))
    def _():
        attend(subtract_max=True)


def _attn_b_kernel(qt_ref, k_ref, vt_ref, o_ref, m_ref, l_ref, acc_ref):
    i = pl.program_id(1)
    t = TOK_TILE

    def acc_rows(hd):
        return slice(hd * B_V_DIM, (hd + 1) * B_V_DIM)

    def diag_mask(s):
        krow = lax.broadcasted_iota(jnp.int32, (t, t), 0) // CHUNK
        qcol = lax.broadcasted_iota(jnp.int32, (t, t), 1) // CHUNK
        return jnp.where(krow <= qcol, s, NEG_INF)

    def sweep(kt, probs, commit):
        def scores(hd):
            k = k_ref[0, hd, pl.ds(pl.multiple_of(kt * t, t), t), :]
            return _dot(k, qt_ref[0, hd])

        pending = [scores(hd) for hd in range(QK_AHEAD)]
        for hd in range(B_HEADS):
            s = pending.pop(0)
            if hd + QK_AHEAD < B_HEADS:
                pending.append(scores(hd + QK_AHEAD))
            p = probs(hd, s)
            commit(hd, _dot(vt_ref[0, hd, kt], p.astype(_BF)), jnp.sum(p, axis=0, keepdims=True))

    def assign(hd, pv, ps):
        acc_ref[acc_rows(hd), :] = pv
        l_ref[hd:hd + 1, :] = ps

    def add(hd, pv, ps):
        acc_ref[acc_rows(hd), :] += pv
        l_ref[hd:hd + 1, :] += ps

    sweep(i, lambda hd, s: jnp.exp2(diag_mask(s)), assign)
    safe = jnp.min(l_ref[...]) > SAFE_DENOM

    @pl.when(safe)
    def _():
        def full_tile(kt, carry):
            sweep(kt, lambda hd, s: jnp.exp2(s), add)
            return carry

        lax.fori_loop(0, i, full_tile, 0)

    @pl.when(jnp.logical_not(safe))
    def _():
        m_ref[...] = jnp.full(m_ref.shape, NEG_INF, _F32)
        l_ref[...] = jnp.zeros(l_ref.shape, _F32)
        acc_ref[...] = jnp.zeros(acc_ref.shape, _F32)

        def online_step(kt, masked):
            alphas = {}

            def probs(hd, s):
                if masked:
                    s = diag_mask(s)
                m = m_ref[hd:hd + 1, :]
                m_new = jnp.maximum(m, jnp.max(s, axis=0, keepdims=True))
                m_ref[hd:hd + 1, :] = m_new
                alphas[hd] = jnp.exp2(m - m_new)
                return jnp.exp2(s - m_new)

            def rescale_add(hd, pv, ps):
                acc_ref[acc_rows(hd), :] = alphas[hd] * acc_ref[acc_rows(hd), :] + pv
                l_ref[hd:hd + 1, :] = alphas[hd] * l_ref[hd:hd + 1, :] + ps

            sweep(kt, probs, rescale_add)

        def online_full(kt, carry):
            online_step(kt, masked=False)
            return carry

        lax.fori_loop(0, i, online_full, 0)
        online_step(i, masked=True)

    outs = [acc_ref[acc_rows(hd), :] / l_ref[hd:hd + 1, :] for hd in range(B_HEADS)]
    o_ref[0] = jnp.concatenate(outs, axis=0).T.astype(_BF)


def _post_kernel(x1_ref, oa_ref, ob_ref, wout_ref, g2_ref, wg_ref, wu_ref, wd_ref,
                 gf_ref, y_ref, act_ref):
    x2 = (x1_ref[...] + _dot(oa_ref[...], wout_ref[0:A_WIDTH, :])
          + _dot(ob_ref[...], wout_ref[A_WIDTH:A_WIDTH + B_WIDTH, :]))
    x3 = x2 + 0.5 * _norm_swiglu(x2, g2_ref[...], wg_ref, wu_ref, wd_ref, act_ref)
    y_ref[...] = _rms(x3, gf_ref[...])


def _rope_tables(s):
    half = B_ROPE // 2
    inv = (1.0 / (ROPE_THETA ** (np.arange(0, B_ROPE, 2, dtype=np.float32) / B_ROPE))).astype(np.float32)
    ang = np.arange(s, dtype=np.float32)[:, None] * inv[None, :]
    cos, sin = np.cos(ang).astype(np.float32), np.sin(ang).astype(np.float32)
    ck = np.zeros((s, LANES), np.float32)
    sk = np.zeros((s, LANES), np.float32)
    ck[:, B_NOPE:B_NOPE + half] = cos
    ck[:, B_NOPE + half:B_QK_DIM] = cos
    sk[:, B_NOPE:B_NOPE + half] = -sin
    sk[:, B_NOPE + half:B_QK_DIM] = sin
    return (jnp.asarray(np.ascontiguousarray(cos.T)), jnp.asarray(np.ascontiguousarray(sin.T)),
            jnp.asarray(ck), jnp.asarray(sk))


def _const_spec(shape):
    n = len(shape)
    return pl.BlockSpec(shape, lambda *_: (0,) * n, pipeline_mode=pl.Buffered(1))


def kernel(x, ffn1_norm, ffn1_w_gate, ffn1_w_up, ffn1_w_down, mix_norm, w_in, a_q_norm, a_k_norm, a_rel_bias, b_q_lat_norm, b_w_uq, b_kv_lat_norm, b_w_ukv, b_q_nope_norm, b_q_rope_norm, b_k_nope_norm, b_k_rope_norm, w_out, ffn2_norm, ffn2_w_gate, ffn2_w_up, ffn2_w_down, final_norm):
    b, s, d = x.shape
    assert d == D_MODEL and s % TOK_TILE == 0 and s >= A_WIN and ffn1_norm.shape[0] == 1
    assert (b * s) % POST_TILE == 0 and s % (A_Q_TILE * A_TILES_PER_STEP) == 0
    tm = TOK_TILE
    nt = s // tm
    half = B_ROPE // 2

    def row(v):
        return v.astype(_F32)[None, :]

    w_in_l = w_in[0]
    o_va, o_cq = 2 * A_WIDTH, 3 * A_WIDTH
    o_ckv = o_cq + B_Q_LORA
    o_kr = o_ckv + B_KV_LORA
    w_fm = jnp.concatenate([w_in_l[:, 0:A_WIDTH], w_in_l[:, o_va:o_cq], w_in_l[:, A_WIDTH:o_va],
                            w_in_l[:, o_cq:o_ckv]], axis=1).T.astype(_BF)
    kr_cols = jnp.zeros((d, LANES), _F32).at[:, B_NOPE:B_QK_DIM].set(w_in_l[:, o_kr:o_kr + B_ROPE])
    w_tm = jnp.concatenate([w_in_l[:, o_ckv:o_kr], kr_cols], axis=1).astype(_BF)

    def col_tab(v):
        return jnp.broadcast_to(v.astype(_F32)[:, None], (v.shape[0], tm))

    gaq_tab = col_tab(a_q_norm[0] * (A_HEAD_DIM ** -0.5 * LOG2E))
    gak_tab = col_tab(a_k_norm[0])

    w_uq = (b_w_uq[0] * b_q_lat_norm[0][:, None]).reshape(B_Q_LORA, B_HEADS, B_QK_DIM)
    w_uq = jnp.pad(w_uq, ((0, 0), (0, 0), (0, HEAD_PAD - B_QK_DIM)))
    wq_t = w_uq.reshape(B_Q_LORA, B_HEADS * HEAD_PAD).T.astype(_BF)
    scale_b = (B_QK_DIM ** -0.5) * LOG2E
    gq_col = jnp.concatenate([b_q_nope_norm[0] * scale_b, b_q_rope_norm[0] * scale_b,
                              jnp.zeros((HEAD_PAD - B_QK_DIM,), _F32)])
    gq_tab = jnp.broadcast_to(gq_col[:, None], (HEAD_PAD, tm)).astype(_F32)
    q_norm2 = B_NOPE * jnp.max(b_q_nope_norm[0] ** 2) + B_ROPE * jnp.max(b_q_rope_norm[0] ** 2)
    k_norm2 = B_NOPE * jnp.max(b_k_nope_norm[0] ** 2) + B_ROPE * jnp.max(b_k_rope_norm[0] ** 2)
    bound_b = (scale_b * jnp.sqrt(q_norm2 * k_norm2) * BOUND_SLACK).astype(_BF).astype(_F32)
    qpad_tab = jnp.zeros((HEAD_PAD - B_QK_DIM, tm), _F32).at[0, :].set(-bound_b)
    kone = jnp.zeros((1, LANES), _F32).at[0, B_QK_DIM].set(1.0)

    cos_t, sin_t, ck, sk = _rope_tables(s)

    w_ukv = b_w_ukv[0].reshape(B_KV_LORA, B_HEADS, B_NOPE + B_V_DIM)
    wvk_t = jnp.concatenate([w_ukv[..., B_NOPE:].reshape(B_KV_LORA, B_WIDTH),
                             w_ukv[..., :B_NOPE].reshape(B_KV_LORA, B_HEADS * B_NOPE)],
                            axis=1).T.astype(_BF)
    gkn_tab = col_tab(b_k_nope_norm[0])
    gkr = row(jnp.concatenate([jnp.zeros((B_NOPE,), _F32), b_k_rope_norm[0],
                               jnp.zeros((LANES - B_QK_DIM,), _F32)]))

    bound_a = ((A_HEAD_DIM ** -0.5 * LOG2E) * A_HEAD_DIM * jnp.max(jnp.abs(a_q_norm[0]))
               * jnp.max(jnp.abs(a_k_norm[0])) * BOUND_SLACK)
    bias_tiles = _rel_bias_tiles(a_rel_bias[0], bound_a)

    tok = lambda bi, j: (bi, j, 0)
    pre_in_specs = [
        pl.BlockSpec((None, tm, d), tok),
        _const_spec((1, d)), _const_spec((d, D_FF)), _const_spec((d, D_FF)), _const_spec((D_FF, d)),
        _const_spec((1, d)), _const_spec((FM_ROWS, d)), _const_spec((d, TM_COLS)),
        _const_spec((A_HEAD_DIM, tm)), _const_spec((A_HEAD_DIM, tm)),
        _const_spec((B_HEADS * HEAD_PAD, B_Q_LORA)), _const_spec((HEAD_PAD, tm)),
        _const_spec((HEAD_PAD - B_QK_DIM, tm)),
        pl.BlockSpec((half, tm), lambda bi, j: (0, j)), pl.BlockSpec((half, tm), lambda bi, j: (0, j)),
        _const_spec((1, B_KV_LORA)), _const_spec((B_WIDTH + B_HEADS * B_NOPE, B_KV_LORA)),
        _const_spec((B_NOPE, tm)),
        _const_spec((1, LANES)), _const_spec((1, LANES)),
        pl.BlockSpec((tm, LANES), lambda bi, j: (j, 0)), pl.BlockSpec((tm, LANES), lambda bi, j: (j, 0)),
    ]
    pre_out_shapes = [
        jax.ShapeDtypeStruct((b, s, d), _F32),
        jax.ShapeDtypeStruct((b, s // A_Q_TILE, A_WIDTH, A_Q_TILE), _BF),
        jax.ShapeDtypeStruct((b, s, A_WIDTH), _BF),
        jax.ShapeDtypeStruct((b, A_HEADS, s // A_Q_TILE, A_HEAD_DIM, A_Q_TILE), _BF),
        jax.ShapeDtypeStruct((b, B_HEADS, HEAD_PAD, s), _BF),
        jax.ShapeDtypeStruct((b, B_HEADS, s, HEAD_PAD), _BF),
        jax.ShapeDtypeStruct((b, B_HEADS, nt, B_V_DIM, tm), _BF),
    ]
    pre_out_specs = [
        pl.BlockSpec((None, tm, d), tok),
        pl.BlockSpec((1, tm // A_Q_TILE, A_WIDTH, A_Q_TILE), lambda bi, j: (bi, j, 0, 0)),
        pl.BlockSpec((1, tm, A_WIDTH), tok),
        pl.BlockSpec((1, A_HEADS, tm // A_Q_TILE, A_HEAD_DIM, A_Q_TILE),
                     lambda bi, j: (bi, 0, j, 0, 0)),
        pl.BlockSpec((1, B_HEADS, HEAD_PAD, tm), lambda bi, j: (bi, 0, 0, j)),
        pl.BlockSpec((1, B_HEADS, tm, HEAD_PAD), lambda bi, j: (bi, 0, j, 0)),
        pl.BlockSpec((1, B_HEADS, 1, B_V_DIM, tm), lambda bi, j: (bi, 0, j, 0, 0)),
    ]
    x1, qat, ka, vat, qbt, kb, vbt = pl.pallas_call(
        _pre_kernel,
        grid=(b, nt),
        in_specs=pre_in_specs,
        out_specs=pre_out_specs,
        out_shape=pre_out_shapes,
        scratch_shapes=[pltpu.VMEM((tm, D_FF), _BF)],
        compiler_params=pltpu.CompilerParams(
            dimension_semantics=("arbitrary", "arbitrary"), vmem_limit_bytes=VMEM_LIMIT),
        name="pre",
    )(x, row(ffn1_norm[0]), ffn1_w_gate[0].astype(_BF), ffn1_w_up[0].astype(_BF),
      ffn1_w_down[0].astype(_BF), row(mix_norm[0]), w_fm, w_tm,
      gaq_tab, gak_tab,
      wq_t, gq_tab, qpad_tab, cos_t, sin_t,
      row(b_kv_lat_norm[0]), wvk_t, gkn_tab,
      gkr, kone, ck, sk)

    na = s // A_Q_TILE
    out_a = pl.pallas_call(
        _attn_a_kernel,
        grid=(b, na // A_TILES_PER_STEP),
        in_specs=[
            pl.BlockSpec((1, A_TILES_PER_STEP, A_WIDTH, A_Q_TILE), lambda bi, j: (bi, j, 0, 0)),
            pl.BlockSpec((1, s, A_WIDTH), lambda bi, j: (bi, 0, 0)),
            pl.BlockSpec((1, A_HEADS, na, A_HEAD_DIM, A_Q_TILE), lambda bi, j: (bi, 0, 0, 0, 0)),
            _const_spec((A_HEADS, A_BIAS_TILES, LANES, A_Q_TILE)),
        ],
        out_specs=pl.BlockSpec((1, A_TILES_PER_STEP * A_Q_TILE, A_WIDTH), tok),
        out_shape=jax.ShapeDtypeStruct((b, s, A_WIDTH), _BF),
        compiler_params=pltpu.CompilerParams(
            dimension_semantics=("arbitrary", "arbitrary"), vmem_limit_bytes=VMEM_LIMIT),
        name="attn_a",
    )(qat, ka, vat, bias_tiles)

    out_b = pl.pallas_call(
        _attn_b_kernel,
        grid=(b, nt),
        in_specs=[
            pl.BlockSpec((1, B_HEADS, HEAD_PAD, tm), lambda bi, j: (bi, 0, 0, j)),
            pl.BlockSpec((1, B_HEADS, s, HEAD_PAD), lambda bi, j: (bi, 0, 0, 0)),
            pl.BlockSpec((1, B_HEADS, nt, B_V_DIM, tm), lambda bi, j: (bi, 0, 0, 0, 0)),
        ],
        out_specs=pl.BlockSpec((1, tm, B_WIDTH), tok),
        out_shape=jax.ShapeDtypeStruct((b, s, B_WIDTH), _BF),
        scratch_shapes=[pltpu.VMEM((B_HEADS, tm), _F32), pltpu.VMEM((B_HEADS, tm), _F32),
                        pltpu.VMEM((B_WIDTH, tm), _F32)],
        compiler_params=pltpu.CompilerParams(
            dimension_semantics=("arbitrary", "arbitrary"), vmem_limit_bytes=VMEM_LIMIT),
        name="attn_b",
    )(qbt, kb, vbt)

    n = b * s
    tp = POST_TILE
    flat = lambda j: (j, 0)
    y = pl.pallas_call(
        _post_kernel,
        grid=(n // tp,),
        in_specs=[
            pl.BlockSpec((tp, d), flat),
            pl.BlockSpec((tp, A_WIDTH), flat),
            pl.BlockSpec((tp, B_WIDTH), flat),
            _const_spec((A_WIDTH + B_WIDTH, d)),
            _const_spec((1, d)), _const_spec((d, D_FF)), _const_spec((d, D_FF)), _const_spec((D_FF, d)),
            _const_spec((1, d)),
        ],
        out_specs=pl.BlockSpec((tp, d), flat),
        out_shape=jax.ShapeDtypeStruct((n, d), _F32),
        scratch_shapes=[pltpu.VMEM((tp, D_FF), _BF)],
        compiler_params=pltpu.CompilerParams(
            dimension_semantics=("arbitrary",), vmem_limit_bytes=VMEM_LIMIT),
        name="post",
    )(x1.reshape(n, d), out_a.reshape(n, A_WIDTH), out_b.reshape(n, B_WIDTH),
      w_out[0].astype(_BF), row(ffn2_norm[0]), ffn2_w_gate[0].astype(_BF),
      ffn2_w_up[0].astype(_BF), ffn2_w_down[0].astype(_BF), row(final_norm[0]))
    return y.reshape(b, s, d)
```

```python
import jax
import jax.numpy as jnp
import numpy as np
from jax import lax
from jax.experimental import pallas as pl
from jax.experimental.pallas import tpu as pltpu

D_MODEL = 1024
D_FF = 2816
CHUNK = 64
A_HEADS = 8
A_HEAD_DIM = 64
A_LEFT_CHUNKS = 8
A_MAX_REL = 128
A_WIDTH = A_HEADS * A_HEAD_DIM
B_HEADS = 8
B_Q_LORA = 256
B_KV_LORA = 128
B_NOPE = 64
B_ROPE = 32
B_V_DIM = 64
B_QK_DIM = B_NOPE + B_ROPE
B_WIDTH = B_HEADS * B_V_DIM
ROPE_THETA = 10000.0
EPS = 1e-6
NEG_INF = -1e30
LOG2E = 1.4426950408889634
BOUND_SLACK = 1.0 + 2.0 ** -5
SAFE_DENOM = 2.0 ** -80

LANES = 128
BF16_SUBLANES = 16
TOK_TILE = 512
POST_TILE = 1024
FF_CHUNK = 256
QK_AHEAD = 2

A_Q_TILE = 256
A_TILES_PER_STEP = 4
A_LEFT = A_LEFT_CHUNKS * CHUNK
A_WIN = A_Q_TILE + A_LEFT
A_KEY_SPLITS = ((0, 384), (384, 768))
A_BIAS_ROWS = A_WIN + A_LEFT
A_BIAS_TILES = A_BIAS_ROWS // LANES
A_TAB_ZERO = A_BIAS_ROWS - A_LEFT + LANES
A_TAB_LEN = A_TAB_ZERO + A_LEFT + A_Q_TILE

HEAD_PAD = 128
ACC_ROWS = A_HEAD_DIM + BF16_SUBLANES
FM_ROWS = 3 * A_WIDTH + B_Q_LORA
TM_COLS = B_KV_LORA + LANES
VMEM_LIMIT = 60 * 1024 * 1024

_BF = jnp.bfloat16
_F32 = jnp.float32


def _dot(a, b):
    return jnp.dot(a, b, preferred_element_type=_F32)


def _dot_nt(a, b):
    return lax.dot_general(a, b, (((1,), (1,)), ((), ())), preferred_element_type=_F32)


def _rms(x, g):
    ms = jnp.mean(x * x, axis=-1, keepdims=True)
    return x * lax.rsqrt(ms + EPS) * g


def _col_rms(x_t):
    return x_t * lax.rsqrt(jnp.mean(x_t * x_t, axis=0, keepdims=True) + EPS)


def _norm_swiglu(x, gain, wg_ref, wu_ref, wd_ref, act_ref):
    h = (x * gain).astype(_BF)
    r = lax.rsqrt(jnp.mean(x * x, axis=-1, keepdims=True) + EPS)
    for c in range(D_FF // FF_CHUNK):
        sl = slice(c * FF_CHUNK, (c + 1) * FF_CHUNK)
        g = _dot(h, wg_ref[:, sl]) * r
        u = _dot(h, wu_ref[:, sl]) * r
        act_ref[:, sl] = (jax.nn.silu(g) * u).astype(_BF)
    return _dot(act_ref[...], wd_ref[...])


def _pre_kernel(x_ref, g1_ref, wg_ref, wu_ref, wd_ref, gmix_ref, wfm_ref, wtm_ref,
                gaq_ref, gak_ref,
                wqt_ref, gqtab_ref, qpad_ref, cost_ref, sint_ref,
                gckv_ref, wvk_ref, gkn_ref,
                gkr_ref, kone_ref, ck_ref, sk_ref,
                x1_ref, qat_ref, ka_ref, vat_ref, qbt_ref, kb_ref, vbt_ref,
                act_ref):
    x = x_ref[...]
    tm = x.shape[0]
    x1 = x + 0.5 * _norm_swiglu(x, g1_ref[...], wg_ref, wu_ref, wd_ref, act_ref)
    x1_ref[...] = x1
    h2 = _rms(x1, gmix_ref[...]).astype(_BF)

    tmj = _dot(h2, wtm_ref[...])
    ckvn = _rms(tmj[:, 0:B_KV_LORA], gckv_ref[...]).astype(_BF)
    cq_t = _dot_nt(wfm_ref[3 * A_WIDTH:FM_ROWS, :], h2)
    vk_t = _dot_nt(wvk_ref[...], ckvn)
    q_t = _dot(wqt_ref[...], _col_rms(cq_t).astype(_BF))
    ka_fm = _dot_nt(wfm_ref[2 * A_WIDTH:3 * A_WIDTH, :], h2)
    qa_fm = _dot_nt(wfm_ref[0:A_WIDTH, :], h2)
    va_fm = _dot_nt(wfm_ref[A_WIDTH:2 * A_WIDTH, :], h2)

    gaq = gaq_ref[...]
    gak = gak_ref[...]
    ka_t = []
    for hd in range(A_HEADS):
        rows = slice(hd * A_HEAD_DIM, (hd + 1) * A_HEAD_DIM)
        qa_hd = (_col_rms(qa_fm[rows]) * gaq).astype(_BF)
        for g in range(tm // A_Q_TILE):
            cols = slice(g * A_Q_TILE, (g + 1) * A_Q_TILE)
            qat_ref[0, g, rows, :] = qa_hd[:, cols]
            vat_ref[0, hd, g] = va_fm[rows, cols].astype(_BF)
        ka_t.append(_col_rms(ka_fm[rows]) * gak)
    ka_ref[0] = jnp.concatenate(ka_t, axis=0).T.astype(_BF)

    cos_t = cost_ref[...]
    sin_t = sint_ref[...]
    gq = gqtab_ref[...]
    half = B_ROPE // 2
    for hd in range(B_HEADS):
        r0 = hd * HEAD_PAD
        nope = _col_rms(q_t[r0:r0 + B_NOPE]) * gq[0:B_NOPE]
        rope = _col_rms(q_t[r0 + B_NOPE:r0 + B_QK_DIM]) * gq[B_NOPE:B_QK_DIM]
        r1 = rope[0:half]
        r2 = rope[half:B_ROPE]
        blk = jnp.concatenate(
            [nope, r1 * cos_t - r2 * sin_t, r1 * sin_t + r2 * cos_t, qpad_ref[...]], axis=0)
        qbt_ref[0, hd] = blk.astype(_BF)

    gkn = gkn_ref[...]
    kn_t = []
    for hd in range(B_HEADS):
        vbt_ref[0, hd, 0] = vk_t[hd * B_V_DIM:(hd + 1) * B_V_DIM].astype(_BF)
        kn_t.append(_col_rms(vk_t[B_WIDTH + hd * B_NOPE:B_WIDTH + (hd + 1) * B_NOPE]) * gkn)
    kn = jnp.concatenate(kn_t, axis=0).T
    kr = tmj[:, B_KV_LORA:TM_COLS]
    y = kr * lax.rsqrt(jnp.sum(kr * kr, axis=-1, keepdims=True) * (1.0 / B_ROPE) + EPS) * gkr_ref[...]
    lane = lax.broadcasted_iota(jnp.int32, y.shape, 1)
    swapped = jnp.where(lane < B_NOPE + half,
                        pltpu.roll(y, LANES - half, axis=1), pltpu.roll(y, half, axis=1))
    kpe = y * ck_ref[...] + swapped * sk_ref[...] + kone_ref[...]
    for hd in range(B_HEADS):
        pair_tile = kn[:, (hd // 2) * LANES:(hd // 2 + 1) * LANES]
        if hd % 2 == 1:
            pair_tile = pltpu.roll(pair_tile, B_NOPE, axis=1)
        kb_ref[0, hd] = jnp.where(lane < B_NOPE, pair_tile, kpe).astype(_BF)


def _bias_kernel(tab_ref, o_ref):
    u = lax.broadcasted_iota(jnp.int32, (LANES, A_Q_TILE), 0)
    qc = lax.broadcasted_iota(jnp.int32, (LANES, A_Q_TILE), 1) // CHUNK
    for tile in range(A_BIAS_TILES):
        a = A_TAB_ZERO + A_LEFT - LANES * tile
        w = jnp.concatenate([tab_ref[0, :, a:a + A_Q_TILE], tab_ref[0, :, a - A_Q_TILE:a]], axis=1)
        rolled = pltpu.roll(jnp.broadcast_to(w, (LANES, 2 * A_Q_TILE)), 0, 1,
                            stride=1, stride_axis=0)
        kc = (LANES * tile + u) // CHUNK - A_LEFT_CHUNKS
        valid = (kc <= qc) & (kc >= qc - A_LEFT_CHUNKS)
        o_ref[0, tile] = jnp.where(valid, rolled[:, :A_Q_TILE], NEG_INF)


def _rel_bias_tiles(rel_bias, qk_bound):
    pad_lo = A_TAB_ZERO - A_MAX_REL
    pad_hi = A_TAB_LEN - pad_lo - (2 * A_MAX_REL + 1)
    rb = rel_bias.astype(_F32) * LOG2E
    rb = rb - (qk_bound + jnp.max(rb, axis=1, keepdims=True))
    table = jnp.concatenate([jnp.broadcast_to(rb[:, :1], (A_HEADS, pad_lo)), rb,
                             jnp.broadcast_to(rb[:, -1:], (A_HEADS, pad_hi))], axis=1)
    return pl.pallas_call(
        _bias_kernel,
        grid=(A_HEADS,),
        in_specs=[pl.BlockSpec((1, 1, A_TAB_LEN), lambda h: (h, 0, 0))],
        out_specs=pl.BlockSpec((1, A_BIAS_TILES, LANES, A_Q_TILE), lambda h: (h, 0, 0, 0)),
        out_shape=jax.ShapeDtypeStruct((A_HEADS, A_BIAS_TILES, LANES, A_Q_TILE), _F32),
        name="rel_bias",
    )(table.reshape(A_HEADS, 1, A_TAB_LEN))


def _attn_a_kernel(qt_ref, k_ref, vt_ref, bias_ref, o_ref):
    def one_tile(sub, carry):
        _attn_a_tile(pl.program_id(1) * A_TILES_PER_STEP + sub, sub,
                     qt_ref, k_ref, vt_ref, bias_ref, o_ref)
        return carry

    lax.fori_loop(0, A_TILES_PER_STEP, one_tile, 0)


def _attn_a_tile(i, sub, qt_ref, k_ref, vt_ref, bias_ref, o_ref):
    left_tiles = A_LEFT // A_Q_TILE
    g0 = jnp.maximum(i - left_tiles, 0)
    start = pl.multiple_of(g0 * A_Q_TILE, A_Q_TILE)
    u0 = (left_tiles - jnp.minimum(i, left_tiles)) * (A_Q_TILE // LANES)
    zeros = jnp.zeros((A_HEAD_DIM, A_Q_TILE), _BF)
    ones_rows = jnp.ones((ACC_ROWS - A_HEAD_DIM, A_WIN), _BF)

    def scores(pair):
        q2 = qt_ref[0, sub, pair * LANES:(pair + 1) * LANES, :]
        q_even = jnp.concatenate([q2[:A_HEAD_DIM], zeros], axis=0)
        q_odd = jnp.concatenate([zeros, q2[A_HEAD_DIM:]], axis=0)
        qq = jnp.concatenate([q_even, q_odd], axis=1)
        parts = []
        for r0, r1 in A_KEY_SPLITS:
            k = k_ref[0, pl.ds(pl.multiple_of(start + r0, LANES), r1 - r0),
                      pair * LANES:(pair + 1) * LANES]
            parts.append(_dot(k, qq))
        return jnp.concatenate(parts, axis=0)

    def attend(subtract_max):
        n_pairs = A_HEADS // 2
        pending = [scores(0)]
        outs, denoms = [], []
        for pair in range(n_pairs):
            s = pending.pop(0)
            if pair + 1 < n_pairs:
                pending.append(scores(pair + 1))
            bias = jnp.concatenate(
                [jnp.concatenate([bias_ref[2 * pair + e, u0 + j] for j in range(A_WIN // LANES)],
                                 axis=0) for e in range(2)], axis=1)
            s = s + bias
            if subtract_max:
                s = s - jnp.max(s, axis=0, keepdims=True)
            p = jnp.exp2(s).astype(_BF)
            for e in range(2):
                hd = 2 * pair + e
                v = jnp.concatenate([vt_ref[0, hd, g0 + g] for g in range(A_WIN // A_Q_TILE)],
                                    axis=1)
                o = _dot(jnp.concatenate([v, ones_rows], axis=0),
                         p[:, e * A_Q_TILE:(e + 1) * A_Q_TILE])
                denoms.append(o[A_HEAD_DIM:A_HEAD_DIM + 1])
                outs.append(o[:A_HEAD_DIM] / denoms[-1])
        o_ref[0, pl.ds(pl.multiple_of(sub * A_Q_TILE, A_Q_TILE), A_Q_TILE), :] = (
            jnp.concatenate(outs, axis=0).T.astype(_BF))
        return jnp.min(jnp.concatenate(denoms, axis=0))

    smallest = attend(subtract_max=False)

    @pl.when(jnp.logical_not(smallest > SAFE_DENOM))
    def _():
        attend(subtract_max=True)


def _attn_b_kernel(qt_ref, k_ref, vt_ref, o_ref, m_ref, l_ref, acc_ref):
    i = pl.program_id(1)
    t = TOK_TILE

    def acc_rows(hd):
        return slice(hd * B_V_DIM, (hd + 1) * B_V_DIM)

    def diag_mask(s):
        krow = lax.broadcasted_iota(jnp.int32, (t, t), 0) // CHUNK
        qcol = lax.broadcasted_iota(jnp.int32, (t, t), 1) // CHUNK
        return jnp.where(krow <= qcol, s, NEG_INF)

    def sweep(kt, probs, commit):
        def scores(hd):
            k = k_ref[0, hd, pl.ds(pl.multiple_of(kt * t, t), t), :]
            return _dot(k, qt_ref[0, hd])

        pending = [scores(hd) for hd in range(QK_AHEAD)]
        for hd in range(B_HEADS):
            s = pending.pop(0)
            if hd + QK_AHEAD < B_HEADS:
                pending.append(scores(hd + QK_AHEAD))
            p = probs(hd, s)
            commit(hd, _dot(vt_ref[0, hd, kt], p.astype(_BF)), jnp.sum(p, axis=0, keepdims=True))

    def assign(hd, pv, ps):
        acc_ref[acc_rows(hd), :] = pv
        l_ref[hd:hd + 1, :] = ps

    def add(hd, pv, ps):
        acc_ref[acc_rows(hd), :] += pv
        l_ref[hd:hd + 1, :] += ps

    sweep(i, lambda hd, s: jnp.exp2(diag_mask(s)), assign)
    safe = jnp.min(l_ref[...]) > SAFE_DENOM

    @pl.when(safe)
    def _():
        def full_tile(kt, carry):
            sweep(kt, lambda hd, s: jnp.exp2(s), add)
            return carry

        lax.fori_loop(0, i, full_tile, 0)

    @pl.when(jnp.logical_not(safe))
    def _():
        m_ref[...] = jnp.full(m_ref.shape, NEG_INF, _F32)
        l_ref[...] = jnp.zeros(l_ref.shape, _F32)
        acc_ref[...] = jnp.zeros(acc_ref.shape, _F32)

        def online_step(kt, masked):
            alphas = {}

            def probs(hd, s):
                if masked:
                    s = diag_mask(s)
                m = m_ref[hd:hd + 1, :]
                m_new = jnp.maximum(m, jnp.max(s, axis=0, keepdims=True))
                m_ref[hd:hd + 1, :] = m_new
                alphas[hd] = jnp.exp2(m - m_new)
                return jnp.exp2(s - m_new)

            def rescale_add(hd, pv, ps):
                acc_ref[acc_rows(hd), :] = alphas[hd] * acc_ref[acc_rows(hd), :] + pv
                l_ref[hd:hd + 1, :] = alphas[hd] * l_ref[hd:hd + 1, :] + ps

            sweep(kt, probs, rescale_add)

        def online_full(kt, carry):
            online_step(kt, masked=False)
            return carry

        lax.fori_loop(0, i, online_full, 0)
        online_step(i, masked=True)

    outs = [acc_ref[acc_rows(hd), :] / l_ref[hd:hd + 1, :] for hd in range(B_HEADS)]
    o_ref[0] = jnp.concatenate(outs, axis=0).T.astype(_BF)


def _post_kernel(x1_ref, oa_ref, ob_ref, wout_ref, g2_ref, wg_ref, wu_ref, wd_ref,
                 gf_ref, y_ref, act_ref):
    x2 = (x1_ref[...] + _dot(oa_ref[...], wout_ref[0:A_WIDTH, :])
          + _dot(ob_ref[...], wout_ref[A_WIDTH:A_WIDTH + B_WIDTH, :]))
    x3 = x2 + 0.5 * _norm_swiglu(x2, g2_ref[...], wg_ref, wu_ref, wd_ref, act_ref)
    y_ref[...] = _rms(x3, gf_ref[...])


def _rope_tables(s):
    half = B_ROPE // 2
    inv = (1.0 / (ROPE_THETA ** (np.arange(0, B_ROPE, 2, dtype=np.float32) / B_ROPE))).astype(np.float32)
    ang = np.arange(s, dtype=np.float32)[:, None] * inv[None, :]
    cos, sin = np.cos(ang).astype(np.float32), np.sin(ang).astype(np.float32)
    ck = np.zeros((s, LANES), np.float32)
    sk = np.zeros((s, LANES), np.float32)
    ck[:, B_NOPE:B_NOPE + half] = cos
    ck[:, B_NOPE + half:B_QK_DIM] = cos
    sk[:, B_NOPE:B_NOPE + half] = -sin
    sk[:, B_NOPE + half:B_QK_DIM] = sin
    return (jnp.asarray(np.ascontiguousarray(cos.T)), jnp.asarray(np.ascontiguousarray(sin.T)),
            jnp.asarray(ck), jnp.asarray(sk))


def _const_spec(shape):
    n = len(shape)
    return pl.BlockSpec(shape, lambda *_: (0,) * n, pipeline_mode=pl.Buffered(1))


def kernel(x, ffn1_norm, ffn1_w_gate, ffn1_w_up, ffn1_w_down, mix_norm, w_in, a_q_norm, a_k_norm, a_rel_bias, b_q_lat_norm, b_w_uq, b_kv_lat_norm, b_w_ukv, b_q_nope_norm, b_q_rope_norm, b_k_nope_norm, b_k_rope_norm, w_out, ffn2_norm, ffn2_w_gate, ffn2_w_up, ffn2_w_down, final_norm):
    b, s, d = x.shape
    assert d == D_MODEL and s % TOK_TILE == 0 and s >= A_WIN and ffn1_norm.shape[0] == 1
    assert (b * s) % POST_TILE == 0 and s % (A_Q_TILE * A_TILES_PER_STEP) == 0
    tm = TOK_TILE
    nt = s // tm
    half = B_ROPE // 2

    def row(v):
        return v.astype(_F32)[None, :]

    w_in_l = w_in[0]
    o_va, o_cq = 2 * A_WIDTH, 3 * A_WIDTH
    o_ckv = o_cq + B_Q_LORA
    o_kr = o_ckv + B_KV_LORA
    w_fm = jnp.concatenate([w_in_l[:, 0:A_WIDTH], w_in_l[:, o_va:o_cq], w_in_l[:, A_WIDTH:o_va],
                            w_in_l[:, o_cq:o_ckv]], axis=1).T.astype(_BF)
    kr_cols = jnp.zeros((d, LANES), _F32).at[:, B_NOPE:B_QK_DIM].set(w_in_l[:, o_kr:o_kr + B_ROPE])
    w_tm = jnp.concatenate([w_in_l[:, o_ckv:o_kr], kr_cols], axis=1).astype(_BF)

    def col_tab(v):
        return jnp.broadcast_to(v.astype(_F32)[:, None], (v.shape[0], tm))

    gaq_tab = col_tab(a_q_norm[0] * (A_HEAD_DIM ** -0.5 * LOG2E))
    gak_tab = col_tab(a_k_norm[0])

    w_uq = (b_w_uq[0] * b_q_lat_norm[0][:, None]).reshape(B_Q_LORA, B_HEADS, B_QK_DIM)
    w_uq = jnp.pad(w_uq, ((0, 0), (0, 0), (0, HEAD_PAD - B_QK_DIM)))
    wq_t = w_uq.reshape(B_Q_LORA, B_HEADS * HEAD_PAD).T.astype(_BF)
    scale_b = (B_QK_DIM ** -0.5) * LOG2E
    gq_col = jnp.concatenate([b_q_nope_norm[0] * scale_b, b_q_rope_norm[0] * scale_b,
                              jnp.zeros((HEAD_PAD - B_QK_DIM,), _F32)])
    gq_tab = jnp.broadcast_to(gq_col[:, None], (HEAD_PAD, tm)).astype(_F32)
    q_norm2 = B_NOPE * jnp.max(b_q_nope_norm[0] ** 2) + B_ROPE * jnp.max(b_q_rope_norm[0] ** 2)
    k_norm2 = B_NOPE * jnp.max(b_k_nope_norm[0] ** 2) + B_ROPE * jnp.max(b_k_rope_norm[0] ** 2)
    bound_b = (scale_b * jnp.sqrt(q_norm2 * k_norm2) * BOUND_SLACK).astype(_BF).astype(_F32)
    qpad_tab = jnp.zeros((HEAD_PAD - B_QK_DIM, tm), _F32).at[0, :].set(-bound_b)
    kone = jnp.zeros((1, LANES), _F32).at[0, B_QK_DIM].set(1.0)

    cos_t, sin_t, ck, sk = _rope_tables(s)

    w_ukv = b_w_ukv[0].reshape(B_KV_LORA, B_HEADS, B_NOPE + B_V_DIM)
    wvk_t = jnp.concatenate([w_ukv[..., B_NOPE:].reshape(B_KV_LORA, B_WIDTH),
                             w_ukv[..., :B_NOPE].reshape(B_KV_LORA, B_HEADS * B_NOPE)],
                            axis=1).T.astype(_BF)
    gkn_tab = col_tab(b_k_nope_norm[0])
    gkr = row(jnp.concatenate([jnp.zeros((B_NOPE,), _F32), b_k_rope_norm[0],
                               jnp.zeros((LANES - B_QK_DIM,), _F32)]))

    bound_a = ((A_HEAD_DIM ** -0.5 * LOG2E) * A_HEAD_DIM * jnp.max(jnp.abs(a_q_norm[0]))
               * jnp.max(jnp.abs(a_k_norm[0])) * BOUND_SLACK)
    bias_tiles = _rel_bias_tiles(a_rel_bias[0], bound_a)

    tok = lambda bi, j: (bi, j, 0)
    pre_in_specs = [
        pl.BlockSpec((None, tm, d), tok),
        _const_spec((1, d)), _const_spec((d, D_FF)), _const_spec((d, D_FF)), _const_spec((D_FF, d)),
        _const_spec((1, d)), _const_spec((FM_ROWS, d)), _const_spec((d, TM_COLS)),
        _const_spec((A_HEAD_DIM, tm)), _const_spec((A_HEAD_DIM, tm)),
        _const_spec((B_HEADS * HEAD_PAD, B_Q_LORA)), _const_spec((HEAD_PAD, tm)),
        _const_spec((HEAD_PAD - B_QK_DIM, tm)),
        pl.BlockSpec((half, tm), lambda bi, j: (0, j)), pl.BlockSpec((half, tm), lambda bi, j: (0, j)),
        _const_spec((1, B_KV_LORA)), _const_spec((B_WIDTH + B_HEADS * B_NOPE, B_KV_LORA)),
        _const_spec((B_NOPE, tm)),
        _const_spec((1, LANES)), _const_spec((1, LANES)),
        pl.BlockSpec((tm, LANES), lambda bi, j: (j, 0)), pl.BlockSpec((tm, LANES), lambda bi, j: (j, 0)),
    ]
    pre_out_shapes = [
        jax.ShapeDtypeStruct((b, s, d), _F32),
        jax.ShapeDtypeStruct((b, s // A_Q_TILE, A_WIDTH, A_Q_TILE), _BF),
        jax.ShapeDtypeStruct((b, s, A_WIDTH), _BF),
        jax.ShapeDtypeStruct((b, A_HEADS, s // A_Q_TILE, A_HEAD_DIM, A_Q_TILE), _BF),
        jax.ShapeDtypeStruct((b, B_HEADS, HEAD_PAD, s), _BF),
        jax.ShapeDtypeStruct((b, B_HEADS, s, HEAD_PAD), _BF),
        jax.ShapeDtypeStruct((b, B_HEADS, nt, B_V_DIM, tm), _BF),
    ]
    pre_out_specs = [
        pl.BlockSpec((None, tm, d), tok),
        pl.BlockSpec((1, tm // A_Q_TILE, A_WIDTH, A_Q_TILE), lambda bi, j: (bi, j, 0, 0)),
        pl.BlockSpec((1, tm, A_WIDTH), tok),
        pl.BlockSpec((1, A_HEADS, tm // A_Q_TILE, A_HEAD_DIM, A_Q_TILE),
                     lambda bi, j: (bi, 0, j, 0, 0)),
        pl.BlockSpec((1, B_HEADS, HEAD_PAD, tm), lambda bi, j: (bi, 0, 0, j)),
        pl.BlockSpec((1, B_HEADS, tm, HEAD_PAD), lambda bi, j: (bi, 0, j, 0)),
        pl.BlockSpec((1, B_HEADS, 1, B_V_DIM, tm), lambda bi, j: (bi, 0, j, 0, 0)),
    ]
    x1, qat, ka, vat, qbt, kb, vbt = pl.pallas_call(
        _pre_kernel,
        grid=(b, nt),
        in_specs=pre_in_specs,
        out_specs=pre_out_specs,
        out_shape=pre_out_shapes,
        scratch_shapes=[pltpu.VMEM((tm, D_FF), _BF)],
        compiler_params=pltpu.CompilerParams(
            dimension_semantics=("arbitrary", "arbitrary"), vmem_limit_bytes=VMEM_LIMIT),
        name="pre",
    )(x, row(ffn1_norm[0]), ffn1_w_gate[0].astype(_BF), ffn1_w_up[0].astype(_BF),
      ffn1_w_down[0].astype(_BF), row(mix_norm[0]), w_fm, w_tm,
      gaq_tab, gak_tab,
      wq_t, gq_tab, qpad_tab, cos_t, sin_t,
      row(b_kv_lat_norm[0]), wvk_t, gkn_tab,
      gkr, kone, ck, sk)

    na = s // A_Q_TILE
    out_a = pl.pallas_call(
        _attn_a_kernel,
        grid=(b, na // A_TILES_PER_STEP),
        in_specs=[
            pl.BlockSpec((1, A_TILES_PER_STEP, A_WIDTH, A_Q_TILE), lambda bi, j: (bi, j, 0, 0)),
            pl.BlockSpec((1, s, A_WIDTH), lambda bi, j: (bi, 0, 0)),
            pl.BlockSpec((1, A_HEADS, na, A_HEAD_DIM, A_Q_TILE), lambda bi, j: (bi, 0, 0, 0, 0)),
            _const_spec((A_HEADS, A_BIAS_TILES, LANES, A_Q_TILE)),
        ],
        out_specs=pl.BlockSpec((1, A_TILES_PER_STEP * A_Q_TILE, A_WIDTH), tok),
        out_shape=jax.ShapeDtypeStruct((b, s, A_WIDTH), _BF),
        compiler_params=pltpu.CompilerParams(
            dimension_semantics=("arbitrary", "arbitrary"), vmem_limit_bytes=VMEM_LIMIT),
        name="attn_a",
    )(qat, ka, vat, bias_tiles)

    out_b = pl.pallas_call(
        _attn_b_kernel,
        grid=(b, nt),
        in_specs=[
            pl.BlockSpec((1, B_HEADS, HEAD_PAD, tm), lambda bi, j: (bi, 0, 0, j)),
            pl.BlockSpec((1, B_HEADS, s, HEAD_PAD), lambda bi, j: (bi, 0, 0, 0)),
            pl.BlockSpec((1, B_HEADS, nt, B_V_DIM, tm), lambda bi, j: (bi, 0, 0, 0, 0)),
        ],
        out_specs=pl.BlockSpec((1, tm, B_WIDTH), tok),
        out_shape=jax.ShapeDtypeStruct((b, s, B_WIDTH), _BF),
        scratch_shapes=[pltpu.VMEM((B_HEADS, tm), _F32), pltpu.VMEM((B_HEADS, tm), _F32),
                        pltpu.VMEM((B_WIDTH, tm), _F32)],
        compiler_params=pltpu.CompilerParams(
            dimension_semantics=("arbitrary", "arbitrary"), vmem_limit_bytes=VMEM_LIMIT),
        name="attn_b",
    )(qbt, kb, vbt)

    n = b * s
    tp = POST_TILE
    flat = lambda j: (j, 0)
    y = pl.pallas_call(
        _post_kernel,
        grid=(n // tp,),
        in_specs=[
            pl.BlockSpec((tp, d), flat),
            pl.BlockSpec((tp, A_WIDTH), flat),
            pl.BlockSpec((tp, B_WIDTH), flat),
            _const_spec((A_WIDTH + B_WIDTH, d)),
            _const_spec((1, d)), _const_spec((d, D_FF)), _const_spec((d, D_FF)), _const_spec((D_FF, d)),
            _const_spec((1, d)),
        ],
        out_specs=pl.BlockSpec((tp, d), flat),
        out_shape=jax.ShapeDtypeStruct((n, d), _F32),
        scratch_shapes=[pltpu.VMEM((tp, D_FF), _BF)],
        compiler_params=pltpu.CompilerParams(
            dimension_semantics=("arbitrary",), vmem_limit_bytes=VMEM_LIMIT),
        name="post",
    )(x1.reshape(n, d), out_a.reshape(n, A_WIDTH), out_b.reshape(n, B_WIDTH),
      w_out[0].astype(_BF), row(ffn2_norm[0]), ffn2_w_gate[0].astype(_BF),
      ffn2_w_up[0].astype(_BF), ffn2_w_down[0].astype(_BF), row(final_norm[0]))
    return y.reshape(b, s, d)
```

```python
import jax
import jax.numpy as jnp
import numpy as np
from jax import lax
from jax.experimental import pallas as pl
from jax.experimental.pallas import tpu as pltpu

D_MODEL = 1024
D_FF = 2816
CHUNK = 64
A_HEADS = 8
A_HEAD_DIM = 64
A_LEFT_CHUNKS = 8
A_MAX_REL = 128
A_WIDTH = A_HEADS * A_HEAD_DIM
B_HEADS = 8
B_Q_LORA = 256
B_KV_LORA = 128
B_NOPE = 64
B_ROPE = 32
B_V_DIM = 64
B_QK_DIM = B_NOPE + B_ROPE
B_WIDTH = B_HEADS * B_V_DIM
ROPE_THETA = 10000.0
EPS = 1e-6
NEG_INF = -1e30
LOG2E = 1.4426950408889634
BOUND_SLACK = 1.0 + 2.0 ** -5
SAFE_DENOM = 2.0 ** -80

LANES = 128
BF16_SUBLANES = 16
TOK_TILE = 512
POST_TILE = 1024
FF_CHUNK = 256
QK_AHEAD = 2

A_Q_TILE = 256
A_TILES_PER_STEP = 4
A_LEFT = A_LEFT_CHUNKS * CHUNK
A_WIN = A_Q_TILE + A_LEFT
A_KEY_SPLITS = ((0, 384), (384, 768))
A_BIAS_ROWS = A_WIN + A_LEFT
A_BIAS_TILES = A_BIAS_ROWS // LANES
A_TAB_ZERO = A_BIAS_ROWS - A_LEFT + LANES
A_TAB_LEN = A_TAB_ZERO + A_LEFT + A_Q_TILE

HEAD_PAD = 128
ACC_ROWS = A_HEAD_DIM + BF16_SUBLANES
FM_ROWS = 3 * A_WIDTH + B_Q_LORA
TM_COLS = B_KV_LORA + LANES
VMEM_LIMIT = 60 * 1024 * 1024

_BF = jnp.bfloat16
_F32 = jnp.float32


def _dot(a, b):
    return jnp.dot(a, b, preferred_element_type=_F32)


def _dot_nt(a, b):
    return lax.dot_general(a, b, (((1,), (1,)), ((), ())), preferred_element_type=_F32)


def _rms(x, g):
    ms = jnp.mean(x * x, axis=-1, keepdims=True)
    return x * lax.rsqrt(ms + EPS) * g


def _col_rms(x_t):
    return x_t * lax.rsqrt(jnp.mean(x_t * x_t, axis=0, keepdims=True) + EPS)


def _norm_swiglu(x, gain, wg_ref, wu_ref, wd_ref, act_ref):
    h = (x * gain).astype(_BF)
    r = lax.rsqrt(jnp.mean(x * x, axis=-1, keepdims=True) + EPS)
    for c in range(D_FF // FF_CHUNK):
        sl = slice(c * FF_CHUNK, (c + 1) * FF_CHUNK)
        g = _dot(h, wg_ref[:, sl]) * r
        u = _dot(h, wu_ref[:, sl]) * r
        act_ref[:, sl] = (jax.nn.silu(g) * u).astype(_BF)
    return _dot(act_ref[...], wd_ref[...])


def _pre_kernel(x_ref, g1_ref, wg_ref, wu_ref, wd_ref, gmix_ref, wfm_ref, wtm_ref,
                gaq_ref, gak_ref,
                wqt_ref, gqtab_ref, qpad_ref, cost_ref, sint_ref,
                gckv_ref, wvk_ref, gkn_ref,
                gkr_ref, kone_ref, ck_ref, sk_ref,
                x1_ref, qat_ref, ka_ref, vat_ref, qbt_ref, kb_ref, vbt_ref,
                act_ref):
    x = x_ref[...]
    tm = x.shape[0]
    x1 = x + 0.5 * _norm_swiglu(x, g1_ref[...], wg_ref, wu_ref, wd_ref, act_ref)
    x1_ref[...] = x1
    h2 = _rms(x1, gmix_ref[...]).astype(_BF)

    tmj = _dot(h2, wtm_ref[...])
    ckvn = _rms(tmj[:, 0:B_KV_LORA], gckv_ref[...]).astype(_BF)
    cq_t = _dot_nt(wfm_ref[3 * A_WIDTH:FM_ROWS, :], h2)
    vk_t = _dot_nt(wvk_ref[...], ckvn)
    q_t = _dot(wqt_ref[...], _col_rms(cq_t).astype(_BF))
    ka_fm = _dot_nt(wfm_ref[2 * A_WIDTH:3 * A_WIDTH, :], h2)
    qa_fm = _dot_nt(wfm_ref[0:A_WIDTH, :], h2)
    va_fm = _dot_nt(wfm_ref[A_WIDTH:2 * A_WIDTH, :], h2)

    gaq = gaq_ref[...]
    gak = gak_ref[...]
    ka_t = []
    for hd in range(A_HEADS):
        rows = slice(hd * A_HEAD_DIM, (hd + 1) * A_HEAD_DIM)
        qa_hd = (_col_rms(qa_fm[rows]) * gaq).astype(_BF)
        for g in range(tm // A_Q_TILE):
            cols = slice(g * A_Q_TILE, (g + 1) * A_Q_TILE)
            qat_ref[0, g, rows, :] = qa_hd[:, cols]
            vat_ref[0, hd, g] = va_fm[rows, cols].astype(_BF)
        ka_t.append(_col_rms(ka_fm[rows]) * gak)
    ka_ref[0] = jnp.concatenate(ka_t, axis=0).T.astype(_BF)

    cos_t = cost_ref[...]
    sin_t = sint_ref[...]
    gq = gqtab_ref[...]
    half = B_ROPE // 2
    for hd in range(B_HEADS):
        r0 = hd * HEAD_PAD
        nope = _col_rms(q_t[r0:r0 + B_NOPE]) * gq[0:B_NOPE]
        rope = _col_rms(q_t[r0 + B_NOPE:r0 + B_QK_DIM]) * gq[B_NOPE:B_QK_DIM]
        r1 = rope[0:half]
        r2 = rope[half:B_ROPE]
        blk = jnp.concatenate(
            [nope, r1 * cos_t - r2 * sin_t, r1 * sin_t + r2 * cos_t, qpad_ref[...]], axis=0)
        qbt_ref[0, hd] = blk.astype(_BF)

    gkn = gkn_ref[...]
    kn_t = []
    for hd in range(B_HEADS):
        vbt_ref[0, hd, 0] = vk_t[hd * B_V_DIM:(hd + 1) * B_V_DIM].astype(_BF)
        kn_t.append(_col_rms(vk_t[B_WIDTH + hd * B_NOPE:B_WIDTH + (hd + 1) * B_NOPE]) * gkn)
    kn = jnp.concatenate(kn_t, axis=0).T
    kr = tmj[:, B_KV_LORA:TM_COLS]
    y = kr * lax.rsqrt(jnp.sum(kr * kr, axis=-1, keepdims=True) * (1.0 / B_ROPE) + EPS) * gkr_ref[...]
    lane = lax.broadcasted_iota(jnp.int32, y.shape, 1)
    swapped = jnp.where(lane < B_NOPE + half,
                        pltpu.roll(y, LANES - half, axis=1), pltpu.roll(y, half, axis=1))
    kpe = y * ck_ref[...] + swapped * sk_ref[...] + kone_ref[...]
    for hd in range(B_HEADS):
        pair_tile = kn[:, (hd // 2) * LANES:(hd // 2 + 1) * LANES]
        if hd % 2 == 1:
            pair_tile = pltpu.roll(pair_tile, B_NOPE, axis=1)
        kb_ref[0, hd] = jnp.where(lane < B_NOPE, pair_tile, kpe).astype(_BF)


def _build_bias_tiles(tab_ref, o_ref):
    u = lax.broadcasted_iota(jnp.int32, (LANES, A_Q_TILE), 0)
    qc = lax.broadcasted_iota(jnp.int32, (LANES, A_Q_TILE), 1) // CHUNK
    for hd in range(A_HEADS):
        for tile in range(A_BIAS_TILES):
            a = A_TAB_ZERO + A_LEFT - LANES * tile
            w = jnp.concatenate([tab_ref[hd, :, a:a + A_Q_TILE], tab_ref[hd, :, a - A_Q_TILE:a]],
                                axis=1)
            rolled = pltpu.roll(jnp.broadcast_to(w, (LANES, 2 * A_Q_TILE)), 0, 1,
                                stride=1, stride_axis=0)
            kc = (LANES * tile + u) // CHUNK - A_LEFT_CHUNKS
            valid = (kc <= qc) & (kc >= qc - A_LEFT_CHUNKS)
            o_ref[hd, tile] = jnp.where(valid, rolled[:, :A_Q_TILE], NEG_INF)


def _rel_bias_table(rel_bias, qk_bound):
    pad_lo = A_TAB_ZERO - A_MAX_REL
    pad_hi = A_TAB_LEN - pad_lo - (2 * A_MAX_REL + 1)
    rb = rel_bias.astype(_F32) * LOG2E
    rb = rb - (qk_bound + jnp.max(rb, axis=1, keepdims=True))
    table = jnp.concatenate([jnp.broadcast_to(rb[:, :1], (A_HEADS, pad_lo)), rb,
                             jnp.broadcast_to(rb[:, -1:], (A_HEADS, pad_hi))], axis=1)
    return table.reshape(A_HEADS, 1, A_TAB_LEN)


def _attn_a_kernel(qt_ref, k_ref, vt_ref, tab_ref, o_ref, bias_ref):
    @pl.when((pl.program_id(0) == 0) & (pl.program_id(1) == 0))
    def _():
        _build_bias_tiles(tab_ref, bias_ref)

    def one_tile(sub, carry):
        _attn_a_tile(pl.program_id(1) * A_TILES_PER_STEP + sub, sub,
                     qt_ref, k_ref, vt_ref, bias_ref, o_ref)
        return carry

    lax.fori_loop(0, A_TILES_PER_STEP, one_tile, 0)


def _attn_a_tile(i, sub, qt_ref, k_ref, vt_ref, bias_ref, o_ref):
    left_tiles = A_LEFT // A_Q_TILE
    g0 = jnp.maximum(i - left_tiles, 0)
    start = pl.multiple_of(g0 * A_Q_TILE, A_Q_TILE)
    u0 = (left_tiles - jnp.minimum(i, left_tiles)) * (A_Q_TILE // LANES)
    zeros = jnp.zeros((A_HEAD_DIM, A_Q_TILE), _BF)
    ones_rows = jnp.ones((ACC_ROWS - A_HEAD_DIM, A_WIN), _BF)

    def scores(pair):
        q2 = qt_ref[0, sub, pair * LANES:(pair + 1) * LANES, :]
        q_even = jnp.concatenate([q2[:A_HEAD_DIM], zeros], axis=0)
        q_odd = jnp.concatenate([zeros, q2[A_HEAD_DIM:]], axis=0)
        qq = jnp.concatenate([q_even, q_odd], axis=1)
        parts = []
        for r0, r1 in A_KEY_SPLITS:
            k = k_ref[0, pl.ds(pl.multiple_of(start + r0, LANES), r1 - r0),
                      pair * LANES:(pair + 1) * LANES]
            parts.append(_dot(k, qq))
        return jnp.concatenate(parts, axis=0)

    def attend(subtract_max):
        n_pairs = A_HEADS // 2
        pending = [scores(0)]
        outs, denoms = [], []
        for pair in range(n_pairs):
            s = pending.pop(0)
            if pair + 1 < n_pairs:
                pending.append(scores(pair + 1))
            bias = jnp.concatenate(
                [jnp.concatenate([bias_ref[2 * pair + e, u0 + j] for j in range(A_WIN // LANES)],
                                 axis=0) for e in range(2)], axis=1)
            s = s + bias
            if subtract_max:
                s = s - jnp.max(s, axis=0, keepdims=True)
            p = jnp.exp2(s).astype(_BF)
            for e in range(2):
                hd = 2 * pair + e
                v = jnp.concatenate([vt_ref[0, hd, g0 + g] for g in range(A_WIN // A_Q_TILE)],
                                    axis=1)
                o = _dot(jnp.concatenate([v, ones_rows], axis=0),
                         p[:, e * A_Q_TILE:(e + 1) * A_Q_TILE])
                denoms.append(o[A_HEAD_DIM:A_HEAD_DIM + 1])
                outs.append(o[:A_HEAD_DIM] / denoms[-1])
        o_ref[0, pl.ds(pl.multiple_of(sub * A_Q_TILE, A_Q_TILE), A_Q_TILE), :] = (
            jnp.concatenate(outs, axis=0).T.astype(_BF))
        return jnp.min(jnp.concatenate(denoms, axis=0))

    smallest = attend(subtract_max=False)

    @pl.when(jnp.logical_not(smallest > SAFE_DENOM))
    def _():
        attend(subtract_max=True)


def _attn_b_kernel(qt_ref, k_ref, vt_ref, o_ref, m_ref, l_ref, acc_ref):
    i = pl.program_id(1)
    t = TOK_TILE

    def acc_rows(hd):
        return slice(hd * B_V_DIM, (hd + 1) * B_V_DIM)

    def diag_mask(s):
        krow = lax.broadcasted_iota(jnp.int32, (t, t), 0) // CHUNK
        qcol = lax.broadcasted_iota(jnp.int32, (t, t), 1) // CHUNK
        return jnp.where(krow <= qcol, s, NEG_INF)

    def sweep(kt, probs, commit):
        def scores(hd):
            k = k_ref[0, hd, pl.ds(pl.multiple_of(kt * t, t), t), :]
            return _dot(k, qt_ref[0, hd])

        pending = [scores(hd) for hd in range(QK_AHEAD)]
        for hd in range(B_HEADS):
            s = pending.pop(0)
            if hd + QK_AHEAD < B_HEADS:
                pending.append(scores(hd + QK_AHEAD))
            p = probs(hd, s)
            commit(hd, _dot(vt_ref[0, hd, kt], p.astype(_BF)), jnp.sum(p, axis=0, keepdims=True))

    def assign(hd, pv, ps):
        acc_ref[acc_rows(hd), :] = pv
        l_ref[hd:hd + 1, :] = ps

    def add(hd, pv, ps):
        acc_ref[acc_rows(hd), :] += pv
        l_ref[hd:hd + 1, :] += ps

    sweep(i, lambda hd, s: jnp.exp2(diag_mask(s)), assign)
    safe = jnp.min(l_ref[...]) > SAFE_DENOM

    @pl.when(safe)
    def _():
        def full_tile(kt, carry):
            sweep(kt, lambda hd, s: jnp.exp2(s), add)
            return carry

        lax.fori_loop(0, i, full_tile, 0)

    @pl.when(jnp.logical_not(safe))
    def _():
        m_ref[...] = jnp.full(m_ref.shape, NEG_INF, _F32)
        l_ref[...] = jnp.zeros(l_ref.shape, _F32)
        acc_ref[...] = jnp.zeros(acc_ref.shape, _F32)

        def online_step(kt, masked):
            alphas = {}

            def probs(hd, s):
                if masked:
                    s = diag_mask(s)
                m = m_ref[hd:hd + 1, :]
                m_new = jnp.maximum(m, jnp.max(s, axis=0, keepdims=True))
                m_ref[hd:hd + 1, :] = m_new
                alphas[hd] = jnp.exp2(m - m_new)
                return jnp.exp2(s - m_new)

            def rescale_add(hd, pv, ps):
                acc_ref[acc_rows(hd), :] = alphas[hd] * acc_ref[acc_rows(hd), :] + pv
                l_ref[hd:hd + 1, :] = alphas[hd] * l_ref[hd:hd + 1, :] + ps

            sweep(kt, probs, rescale_add)

        def online_full(kt, carry):
            online_step(kt, masked=False)
            return carry

        lax.fori_loop(0, i, online_full, 0)
        online_step(i, masked=True)

    outs = [acc_ref[acc_rows(hd), :] / l_ref[hd:hd + 1, :] for hd in range(B_HEADS)]
    o_ref[0] = jnp.concatenate(outs, axis=0).T.astype(_BF)


def _post_kernel(x1_ref, oa_ref, ob_ref, wout_ref, g2_ref, wg_ref, wu_ref, wd_ref,
                 gf_ref, y_ref, act_ref):
    x2 = (x1_ref[...] + _dot(oa_ref[...], wout_ref[0:A_WIDTH, :])
          + _dot(ob_ref[...], wout_ref[A_WIDTH:A_WIDTH + B_WIDTH, :]))
    x3 = x2 + 0.5 * _norm_swiglu(x2, g2_ref[...], wg_ref, wu_ref, wd_ref, act_ref)
    y_ref[...] = _rms(x3, gf_ref[...])


def _rope_tables(s):
    half = B_ROPE // 2
    inv = (1.0 / (ROPE_THETA ** (np.arange(0, B_ROPE, 2, dtype=np.float32) / B_ROPE))).astype(np.float32)
    ang = np.arange(s, dtype=np.float32)[:, None] * inv[None, :]
    cos, sin = np.cos(ang).astype(np.float32), np.sin(ang).astype(np.float32)
    ck = np.zeros((s, LANES), np.float32)
    sk = np.zeros((s, LANES), np.float32)
    ck[:, B_NOPE:B_NOPE + half] = cos
    ck[:, B_NOPE + half:B_QK_DIM] = cos
    sk[:, B_NOPE:B_NOPE + half] = -sin
    sk[:, B_NOPE + half:B_QK_DIM] = sin
    return (jnp.asarray(np.ascontiguousarray(cos.T)), jnp.asarray(np.ascontiguousarray(sin.T)),
            jnp.asarray(ck), jnp.asarray(sk))


def _const_spec(shape):
    n = len(shape)
    return pl.BlockSpec(shape, lambda *_: (0,) * n, pipeline_mode=pl.Buffered(1))


def kernel(x, ffn1_norm, ffn1_w_gate, ffn1_w_up, ffn1_w_down, mix_norm, w_in, a_q_norm, a_k_norm, a_rel_bias, b_q_lat_norm, b_w_uq, b_kv_lat_norm, b_w_ukv, b_q_nope_norm, b_q_rope_norm, b_k_nope_norm, b_k_rope_norm, w_out, ffn2_norm, ffn2_w_gate, ffn2_w_up, ffn2_w_down, final_norm):
    b, s, d = x.shape
    assert d == D_MODEL and s % TOK_TILE == 0 and s >= A_WIN and ffn1_norm.shape[0] == 1
    assert (b * s) % POST_TILE == 0 and s % (A_Q_TILE * A_TILES_PER_STEP) == 0
    tm = TOK_TILE
    nt = s // tm
    half = B_ROPE // 2

    def row(v):
        return v.astype(_F32)[None, :]

    w_in_l = w_in[0]
    o_va, o_cq = 2 * A_WIDTH, 3 * A_WIDTH
    o_ckv = o_cq + B_Q_LORA
    o_kr = o_ckv + B_KV_LORA
    w_fm = jnp.concatenate([w_in_l[:, 0:A_WIDTH], w_in_l[:, o_va:o_cq], w_in_l[:, A_WIDTH:o_va],
                            w_in_l[:, o_cq:o_ckv]], axis=1).T.astype(_BF)
    kr_cols = jnp.zeros((d, LANES), _F32).at[:, B_NOPE:B_QK_DIM].set(w_in_l[:, o_kr:o_kr + B_ROPE])
    w_tm = jnp.concatenate([w_in_l[:, o_ckv:o_kr], kr_cols], axis=1).astype(_BF)

    def col_tab(v):
        return jnp.broadcast_to(v.astype(_F32)[:, None], (v.shape[0], tm))

    gaq_tab = col_tab(a_q_norm[0] * (A_HEAD_DIM ** -0.5 * LOG2E))
    gak_tab = col_tab(a_k_norm[0])

    w_uq = (b_w_uq[0] * b_q_lat_norm[0][:, None]).reshape(B_Q_LORA, B_HEADS, B_QK_DIM)
    w_uq = jnp.pad(w_uq, ((0, 0), (0, 0), (0, HEAD_PAD - B_QK_DIM)))
    wq_t = w_uq.reshape(B_Q_LORA, B_HEADS * HEAD_PAD).T.astype(_BF)
    scale_b = (B_QK_DIM ** -0.5) * LOG2E
    gq_col = jnp.concatenate([b_q_nope_norm[0] * scale_b, b_q_rope_norm[0] * scale_b,
                              jnp.zeros((HEAD_PAD - B_QK_DIM,), _F32)])
    gq_tab = jnp.broadcast_to(gq_col[:, None], (HEAD_PAD, tm)).astype(_F32)
    q_norm2 = B_NOPE * jnp.max(b_q_nope_norm[0] ** 2) + B_ROPE * jnp.max(b_q_rope_norm[0] ** 2)
    k_norm2 = B_NOPE * jnp.max(b_k_nope_norm[0] ** 2) + B_ROPE * jnp.max(b_k_rope_norm[0] ** 2)
    bound_b = (scale_b * jnp.sqrt(q_norm2 * k_norm2) * BOUND_SLACK).astype(_BF).astype(_F32)
    qpad_tab = jnp.zeros((HEAD_PAD - B_QK_DIM, tm), _F32).at[0, :].set(-bound_b)
    kone = jnp.zeros((1, LANES), _F32).at[0, B_QK_DIM].set(1.0)

    cos_t, sin_t, ck, sk = _rope_tables(s)

    w_ukv = b_w_ukv[0].reshape(B_KV_LORA, B_HEADS, B_NOPE + B_V_DIM)
    wvk_t = jnp.concatenate([w_ukv[..., B_NOPE:].reshape(B_KV_LORA, B_WIDTH),
                             w_ukv[..., :B_NOPE].reshape(B_KV_LORA, B_HEADS * B_NOPE)],
                            axis=1).T.astype(_BF)
    gkn_tab = col_tab(b_k_nope_norm[0])
    gkr = row(jnp.concatenate([jnp.zeros((B_NOPE,), _F32), b_k_rope_norm[0],
                               jnp.zeros((LANES - B_QK_DIM,), _F32)]))

    bound_a = ((A_HEAD_DIM ** -0.5 * LOG2E) * A_HEAD_DIM * jnp.max(jnp.abs(a_q_norm[0]))
               * jnp.max(jnp.abs(a_k_norm[0])) * BOUND_SLACK)
    bias_table = _rel_bias_table(a_rel_bias[0], bound_a)

    tok = lambda bi, j: (bi, j, 0)
    pre_in_specs = [
        pl.BlockSpec((None, tm, d), tok),
        _const_spec((1, d)), _const_spec((d, D_FF)), _const_spec((d, D_FF)), _const_spec((D_FF, d)),
        _const_spec((1, d)), _const_spec((FM_ROWS, d)), _const_spec((d, TM_COLS)),
        _const_spec((A_HEAD_DIM, tm)), _const_spec((A_HEAD_DIM, tm)),
        _const_spec((B_HEADS * HEAD_PAD, B_Q_LORA)), _const_spec((HEAD_PAD, tm)),
        _const_spec((HEAD_PAD - B_QK_DIM, tm)),
        pl.BlockSpec((half, tm), lambda bi, j: (0, j)), pl.BlockSpec((half, tm), lambda bi, j: (0, j)),
        _const_spec((1, B_KV_LORA)), _const_spec((B_WIDTH + B_HEADS * B_NOPE, B_KV_LORA)),
        _const_spec((B_NOPE, tm)),
        _const_spec((1, LANES)), _const_spec((1, LANES)),
        pl.BlockSpec((tm, LANES), lambda bi, j: (j, 0)), pl.BlockSpec((tm, LANES), lambda bi, j: (j, 0)),
    ]
    pre_out_shapes = [
        jax.ShapeDtypeStruct((b, s, d), _F32),
        jax.ShapeDtypeStruct((b, s // A_Q_TILE, A_WIDTH, A_Q_TILE), _BF),
        jax.ShapeDtypeStruct((b, s, A_WIDTH), _BF),
        jax.ShapeDtypeStruct((b, A_HEADS, s // A_Q_TILE, A_HEAD_DIM, A_Q_TILE), _BF),
        jax.ShapeDtypeStruct((b, B_HEADS, HEAD_PAD, s), _BF),
        jax.ShapeDtypeStruct((b, B_HEADS, s, HEAD_PAD), _BF),
        jax.ShapeDtypeStruct((b, B_HEADS, nt, B_V_DIM, tm), _BF),
    ]
    pre_out_specs = [
        pl.BlockSpec((None, tm, d), tok),
        pl.BlockSpec((1, tm // A_Q_TILE, A_WIDTH, A_Q_TILE), lambda bi, j: (bi, j, 0, 0)),
        pl.BlockSpec((1, tm, A_WIDTH), tok),
        pl.BlockSpec((1, A_HEADS, tm // A_Q_TILE, A_HEAD_DIM, A_Q_TILE),
                     lambda bi, j: (bi, 0, j, 0, 0)),
        pl.BlockSpec((1, B_HEADS, HEAD_PAD, tm), lambda bi, j: (bi, 0, 0, j)),
        pl.BlockSpec((1, B_HEADS, tm, HEAD_PAD), lambda bi, j: (bi, 0, j, 0)),
        pl.BlockSpec((1, B_HEADS, 1, B_V_DIM, tm), lambda bi, j: (bi, 0, j, 0, 0)),
    ]
    x1, qat, ka, vat, qbt, kb, vbt = pl.pallas_call(
        _pre_kernel,
        grid=(b, nt),
        in_specs=pre_in_specs,
        out_specs=pre_out_specs,
        out_shape=pre_out_shapes,
        scratch_shapes=[pltpu.VMEM((tm, D_FF), _BF)],
        compiler_params=pltpu.CompilerParams(
            dimension_semantics=("arbitrary", "arbitrary"), vmem_limit_bytes=VMEM_LIMIT),
        name="pre",
    )(x, row(ffn1_norm[0]), ffn1_w_gate[0].astype(_BF), ffn1_w_up[0].astype(_BF),
      ffn1_w_down[0].astype(_BF), row(mix_norm[0]), w_fm, w_tm,
      gaq_tab, gak_tab,
      wq_t, gq_tab, qpad_tab, cos_t, sin_t,
      row(b_kv_lat_norm[0]), wvk_t, gkn_tab,
      gkr, kone, ck, sk)

    na = s // A_Q_TILE
    out_a = pl.pallas_call(
        _attn_a_kernel,
        grid=(b, na // A_TILES_PER_STEP),
        in_specs=[
            pl.BlockSpec((1, A_TILES_PER_STEP, A_WIDTH, A_Q_TILE), lambda bi, j: (bi, j, 0, 0)),
            pl.BlockSpec((1, s, A_WIDTH), lambda bi, j: (bi, 0, 0)),
            pl.BlockSpec((1, A_HEADS, na, A_HEAD_DIM, A_Q_TILE), lambda bi, j: (bi, 0, 0, 0, 0)),
            _const_spec((A_HEADS, 1, A_TAB_LEN)),
        ],
        out_specs=pl.BlockSpec((1, A_TILES_PER_STEP * A_Q_TILE, A_WIDTH), tok),
        out_shape=jax.ShapeDtypeStruct((b, s, A_WIDTH), _BF),
        scratch_shapes=[pltpu.VMEM((A_HEADS, A_BIAS_TILES, LANES, A_Q_TILE), _F32)],
        compiler_params=pltpu.CompilerParams(
            dimension_semantics=("arbitrary", "arbitrary"), vmem_limit_bytes=VMEM_LIMIT),
        name="attn_a",
    )(qat, ka, vat, bias_table)

    out_b = pl.pallas_call(
        _attn_b_kernel,
        grid=(b, nt),
        in_specs=[
            pl.BlockSpec((1, B_HEADS, HEAD_PAD, tm), lambda bi, j: (bi, 0, 0, j)),
            pl.BlockSpec((1, B_HEADS, s, HEAD_PAD), lambda bi, j: (bi, 0, 0, 0)),
            pl.BlockSpec((1, B_HEADS, nt, B_V_DIM, tm), lambda bi, j: (bi, 0, 0, 0, 0)),
        ],
        out_specs=pl.BlockSpec((1, tm, B_WIDTH), tok),
        out_shape=jax.ShapeDtypeStruct((b, s, B_WIDTH), _BF),
        scratch_shapes=[pltpu.VMEM((B_HEADS, tm), _F32), pltpu.VMEM((B_HEADS, tm), _F32),
                        pltpu.VMEM((B_WIDTH, tm), _F32)],
        compiler_params=pltpu.CompilerParams(
            dimension_semantics=("arbitrary", "arbitrary"), vmem_limit_bytes=VMEM_LIMIT),
        name="attn_b",
    )(qbt, kb, vbt)

    n = b * s
    tp = POST_TILE
    flat = lambda j: (j, 0)
    y = pl.pallas_call(
        _post_kernel,
        grid=(n // tp,),
        in_specs=[
            pl.BlockSpec((tp, d), flat),
            pl.BlockSpec((tp, A_WIDTH), flat),
            pl.BlockSpec((tp, B_WIDTH), flat),
            _const_spec((A_WIDTH + B_WIDTH, d)),
            _const_spec((1, d)), _const_spec((d, D_FF)), _const_spec((d, D_FF)), _const_spec((D_FF, d)),
            _const_spec((1, d)),
        ],
        out_specs=pl.BlockSpec((tp, d), flat),
        out_shape=jax.ShapeDtypeStruct((n, d), _F32),
        scratch_shapes=[pltpu.VMEM((tp, D_FF), _BF)],
        compiler_params=pltpu.CompilerParams(
            dimension_semantics=("arbitrary",), vmem_limit_bytes=VMEM_LIMIT),
        name="post",
    )(x1.reshape(n, d), out_a.reshape(n, A_WIDTH), out_b.reshape(n, B_WIDTH),
      w_out[0].astype(_BF), row(ffn2_norm[0]), ffn2_w_gate[0].astype(_BF),
      ffn2_w_up[0].astype(_BF), ffn2_w_down[0].astype(_BF), row(final_norm[0]))
    return y.reshape(b, s, d)
```
